```python
import math
import jax, jax.numpy as jnp
from jax import lax
import numpy as np

D_MODEL = 1024
BATCH = 2
SEQ = 16384
DEPTH = 4
DEC_BATCH = 32
DEC_SEQ = 16
PAST_LEN = 4096

CHUNK = 64
N_MIXERS = 2
N_CONV = (DEPTH + 1) // 2
N_ATTN = DEPTH // 2
CONV_WIDTH = 31
WINDOW = 128
BAND_CHUNKS = WINDOW // CHUNK + 1
N_HEADS = 16
N_KV_HEADS = 4
GROUP = N_HEADS // N_KV_HEADS
HEAD_DIM = 64
D_FF = 4 * D_MODEL
PLE_DIM = 256
NUM_BUCKETS = 32
MAX_DISTANCE = 128
EPS = 1e-6
NEG_INF = -1e30

kernel_name = "hybrid_conformer_swa_sink_stream_step"


def _rmsnorm(x, g):
    xf = x.astype(jnp.float32)
    y = xf * lax.rsqrt(jnp.mean(xf * xf, axis=-1, keepdims=True) + EPS) * g.astype(jnp.float32)
    return y.astype(x.dtype)


def _layernorm(x, g, b):
    xf = x.astype(jnp.float32)
    mu = jnp.mean(xf, axis=-1, keepdims=True)
    var = jnp.mean(jnp.square(xf - mu), axis=-1, keepdims=True)
    y = (xf - mu) * lax.rsqrt(var + EPS) * g.astype(jnp.float32) + b.astype(jnp.float32)
    return y.astype(x.dtype)


def _t5_bucket(rel):
    nb = NUM_BUCKETS // 2
    max_exact = nb // 2
    ret = (rel > 0).astype(jnp.int32) * nb
    n = jnp.abs(rel)
    nf = jnp.maximum(n, 1).astype(jnp.float32)
    large = max_exact + (jnp.log(nf / max_exact) / math.log(MAX_DISTANCE / max_exact)
                         * (nb - max_exact)).astype(jnp.int32)
    large = jnp.minimum(large, nb - 1)
    return ret + jnp.where(n < max_exact, n, large)


def _rel_bias(table, q_len, k_len, offset):
    rel = jnp.arange(k_len, dtype=jnp.int32)[None, :] - offset - jnp.arange(q_len, dtype=jnp.int32)[:, None]
    b = table.astype(jnp.float32)[_t5_bucket(rel)]
    return jnp.transpose(b, (2, 0, 1)).reshape(N_KV_HEADS, GROUP, q_len, k_len)


def _sink_softmax(logits, sinks):
    s = sinks.astype(jnp.float32).reshape(N_KV_HEADS, GROUP)[:, :, None, None]
    m = jnp.maximum(jnp.max(logits, axis=-1, keepdims=True), s)
    e = jnp.exp(logits - m)
    return e / (jnp.sum(e, axis=-1, keepdims=True) + jnp.exp(s - m))


def _conv_module(h, left, pw1, pw1_b, dw, dw_b, ln_g, ln_b, pw2, pw2_b):
    a = h @ pw1 + pw1_b
    u = a[..., :D_MODEL] * jax.nn.sigmoid(a[..., D_MODEL:])
    ext = jnp.concatenate([left.astype(u.dtype), u], axis=1)
    c = lax.conv_general_dilated(ext, dw[:, None, :].astype(ext.dtype), window_strides=(1,), padding='VALID',
                                 dimension_numbers=('NWC', 'WIO', 'NWC'),
                                 feature_group_count=D_MODEL) + dw_b
    c = jax.nn.silu(_layernorm(c, ln_g, ln_b))
    return c @ pw2 + pw2_b, ext[:, -(CONV_WIDTH - 1):]


def _swa_prompt(h, wq, wk, wv, wo, sinks, table):
    B, S, _ = h.shape
    nc = S // CHUNK
    pad = (BAND_CHUNKS - 1) * CHUNK
    q = (h @ wq).reshape(B, nc, CHUNK, N_KV_HEADS, GROUP, HEAD_DIM)
    k = (h @ wk).reshape(B, S, N_KV_HEADS, HEAD_DIM)
    v = (h @ wv).reshape(B, S, N_KV_HEADS, HEAD_DIM)

    def band(t):
        tp = jnp.pad(t, ((0, 0), (pad, 0), (0, 0), (0, 0)))
        tp = tp.reshape(B, nc + BAND_CHUNKS - 1, CHUNK, N_KV_HEADS, HEAD_DIM)
        return jnp.concatenate([tp[:, o:o + nc] for o in range(BAND_CHUNKS)], axis=2)

    kb, vb = band(k), band(v)
    logits = jnp.einsum('bnqkgd,bnskd->bnkgqs', q, kb, preferred_element_type=jnp.float32)
    logits = logits * (HEAD_DIM ** -0.5) + _rel_bias(table, CHUNK, BAND_CHUNKS * CHUNK, pad)
    key_pos = (jnp.arange(nc)[:, None] * CHUNK + jnp.arange(BAND_CHUNKS * CHUNK)[None, :] - pad)
    logits = jnp.where((key_pos >= 0)[None, :, None, None, None, :], logits, NEG_INF)
    probs = _sink_softmax(logits, sinks)
    o = jnp.einsum('bnkgqs,bnskd->bnqkgd', probs.astype(vb.dtype), vb).reshape(B, S, N_HEADS * HEAD_DIM)
    return o @ wo, k[:, -WINDOW:], v[:, -WINDOW:]


def _swa_sample(h, k_left, v_left, wq, wk, wv, wo, sinks, table):
    B, T, _ = h.shape
    L = k_left.shape[1]
    q = (h @ wq).reshape(B, T, N_KV_HEADS, GROUP, HEAD_DIM)
    k_new = (h @ wk).reshape(B, T, N_KV_HEADS, HEAD_DIM)
    v_new = (h @ wv).reshape(B, T, N_KV_HEADS, HEAD_DIM)
    kf = jnp.concatenate([k_left.astype(k_new.dtype), k_new], axis=1)
    vf = jnp.concatenate([v_left.astype(v_new.dtype), v_new], axis=1)
    logits = jnp.einsum('btkgd,bskd->bkgts', q, kf, preferred_element_type=jnp.float32)
    logits = logits * (HEAD_DIM ** -0.5) + _rel_bias(table, T, L + T, L)
    probs = _sink_softmax(logits, sinks)
    o = jnp.einsum('bkgts,bskd->btkgd', probs.astype(vf.dtype), vf).reshape(B, T, N_HEADS * HEAD_DIM)
    return o @ wo, kf[:, -L:], vf[:, -L:]


def _sqrelu_mlp(h, w1, w2):
    return jnp.square(jax.nn.relu(h @ w1)) @ w2


def _trunk(x, p, conv_left, k_left, v_left, w, prompt):
    conv_states, k_states, v_states = [], [], []
    for i in range(DEPTH):
        j = i // N_MIXERS
        h = _rmsnorm(x, w['norm_mix'][i])
        if i % N_MIXERS == 0:
            left = (jnp.zeros((x.shape[0], CONV_WIDTH - 1, D_MODEL), x.dtype) if prompt else conv_left[j])
            mix, cs = _conv_module(h, left, w['conv_pw1'][j], w['conv_pw1_b'][j], w['conv_dw'][j],
                                   w['conv_dw_b'][j], w['conv_ln_g'][j], w['conv_ln_b'][j],
                                   w['conv_pw2'][j], w['conv_pw2_b'][j])
            conv_states.append(cs)
        else:
            if prompt:
                mix, ks, vs = _swa_prompt(h, w['attn_wq'][j], w['attn_wk'][j], w['attn_wv'][j],
                                          w['attn_wo'][j], w['attn_sinks'][j], w['rel_bias'])
            else:
                mix, ks, vs = _swa_sample(h, k_left[j], v_left[j], w['attn_wq'][j], w['attn_wk'][j],
                                          w['attn_wv'][j], w['attn_wo'][j], w['attn_sinks'][j], w['rel_bias'])
            k_states.append(ks)
            v_states.append(vs)
        x = x + mix
        x = x + _sqrelu_mlp(_rmsnorm(x, w['norm_mlp'][i]), w['mlp_w1'][i], w['mlp_w2'][i])
        gate = jax.nn.sigmoid(_rmsnorm(x, w['norm_ple'][i]) @ w['ple_gate'][i])
        x = x + (p[i] @ w['ple_proj'][i]) * gate
    y = _rmsnorm(x, w['norm_final'])
    return y, jnp.stack(conv_states), jnp.stack(k_states), jnp.stack(v_states)


def setup_inputs(seed: int = 0) -> dict:
    key = jax.random.key(seed)
    ks = jax.random.split(key, 40)

    def nrm(k, shape, scale):
        return jax.random.normal(k, shape, jnp.float32) * scale

    cache_len = min(WINDOW, PAST_LEN)
    return {
        'x_prompt': nrm(ks[0], (BATCH, SEQ, D_MODEL), 1.0),
        'x_sample': nrm(ks[1], (DEC_BATCH, DEC_SEQ, D_MODEL), 1.0),
        'cache_conv': nrm(ks[2], (N_CONV, DEC_BATCH, CONV_WIDTH - 1, D_MODEL), 0.5),
        'cache_k': nrm(ks[3], (N_ATTN, DEC_BATCH, cache_len, N_KV_HEADS, HEAD_DIM), 1.0),
        'cache_v': nrm(ks[4], (N_ATTN, DEC_BATCH, cache_len, N_KV_HEADS, HEAD_DIM), 1.0),
        'p_prompt': nrm(ks[5], (DEPTH, BATCH, SEQ, PLE_DIM), 1.0),
        'p_sample': nrm(ks[6], (DEPTH, DEC_BATCH, DEC_SEQ, PLE_DIM), 1.0),
        'rel_bias': nrm(ks[7], (NUM_BUCKETS, N_HEADS), 0.5),
        'norm_mix': 1.0 + nrm(ks[8], (DEPTH, D_MODEL), 0.05),
        'norm_mlp': 1.0 + nrm(ks[9], (DEPTH, D_MODEL), 0.05),
        'norm_ple': 1.0 + nrm(ks[10], (DEPTH, D_MODEL), 0.05),
        'norm_final': 1.0 + nrm(ks[11], (D_MODEL,), 0.05),
        'conv_pw1': nrm(ks[12], (N_CONV, D_MODEL, 2 * D_MODEL), D_MODEL ** -0.5),
        'conv_pw1_b': nrm(ks[13], (N_CONV, 2 * D_MODEL), 0.02),
        'conv_dw': nrm(ks[14], (N_CONV, CONV_WIDTH, D_MODEL), CONV_WIDTH ** -0.5),
        'conv_dw_b': nrm(ks[15], (N_CONV, D_MODEL), 0.02),
        'conv_ln_g': 1.0 + nrm(ks[16], (N_CONV, D_MODEL), 0.05),
        'conv_ln_b': nrm(ks[17], (N_CONV, D_MODEL), 0.02),
        'conv_pw2': nrm(ks[18], (N_CONV, D_MODEL, D_MODEL), D_MODEL ** -0.5),
        'conv_pw2_b': nrm(ks[19], (N_CONV, D_MODEL), 0.02),
        'attn_wq': nrm(ks[20], (N_ATTN, D_MODEL, N_HEADS * HEAD_DIM), D_MODEL ** -0.5),
        'attn_wk': nrm(ks[21], (N_ATTN, D_MODEL, N_KV_HEADS * HEAD_DIM), D_MODEL ** -0.5),
        'attn_wv': nrm(ks[22], (N_ATTN, D_MODEL, N_KV_HEADS * HEAD_DIM), D_MODEL ** -0.5),
        'attn_wo': nrm(ks[23], (N_ATTN, N_HEADS * HEAD_DIM, D_MODEL), (N_HEADS * HEAD_DIM) ** -0.5),
        'attn_sinks': nrm(ks[24], (N_ATTN, N_HEADS), 0.5),
        'mlp_w1': nrm(ks[25], (DEPTH, D_MODEL, D_FF), D_MODEL ** -0.5),
        'mlp_w2': nrm(ks[26], (DEPTH, D_FF, D_MODEL), D_FF ** -0.5),
        'ple_proj': nrm(ks[27], (DEPTH, PLE_DIM, D_MODEL), PLE_DIM ** -0.5),
        'ple_gate': nrm(ks[28], (DEPTH, D_MODEL, D_MODEL), D_MODEL ** -0.5),
    }


def reference(x_prompt, x_sample, cache_conv, cache_k, cache_v, p_prompt, p_sample, rel_bias,
              norm_mix, norm_mlp, norm_ple, norm_final, conv_pw1, conv_pw1_b, conv_dw, conv_dw_b,
              conv_ln_g, conv_ln_b, conv_pw2, conv_pw2_b, attn_wq, attn_wk, attn_wv, attn_wo,
              attn_sinks, mlp_w1, mlp_w2, ple_proj, ple_gate):
    w = dict(rel_bias=rel_bias, norm_mix=norm_mix, norm_mlp=norm_mlp, norm_ple=norm_ple,
             norm_final=norm_final, conv_pw1=conv_pw1, conv_pw1_b=conv_pw1_b, conv_dw=conv_dw,
             conv_dw_b=conv_dw_b, conv_ln_g=conv_ln_g, conv_ln_b=conv_ln_b, conv_pw2=conv_pw2,
             conv_pw2_b=conv_pw2_b, attn_wq=attn_wq, attn_wk=attn_wk, attn_wv=attn_wv,
             attn_wo=attn_wo, attn_sinks=attn_sinks, mlp_w1=mlp_w1, mlp_w2=mlp_w2,
             ple_proj=ple_proj, ple_gate=ple_gate)
    y_prompt, conv_state_prompt, k_state_prompt, v_state_prompt = _trunk(
        x_prompt, p_prompt, None, None, None, w, True)
    y_sample, conv_state_sample, k_state_sample, v_state_sample = _trunk(
        x_sample, p_sample, cache_conv, cache_k, cache_v, w, False)
    return (y_prompt, y_sample, conv_state_prompt, k_state_prompt, v_state_prompt,
            conv_state_sample, k_state_sample, v_state_sample)
```

```python
import functools
import math

import jax
import jax.numpy as jnp
from jax import lax
from jax.experimental import pallas as pl
from jax.experimental.pallas import tpu as pltpu

D_MODEL = 1024
DEPTH = 4
CHUNK = 64
CONV_WIDTH = 31
WINDOW = 128
N_HEADS = 16
N_KV_HEADS = 4
GROUP = N_HEADS // N_KV_HEADS
HEAD_DIM = 64
KV_DIM = N_KV_HEADS * HEAD_DIM
D_FF = 4 * D_MODEL
PLE_DIM = 256
NUM_BUCKETS = 32
MAX_DISTANCE = 128
EPS = 1e-6
NEG_INF = -1e30

BAND = WINDOW + CHUNK
HALO = 32
SUBLANES = 8
CONV_ROWS = 32
FF_CHUNK = 1024
TILE_M = 512
VMEM_LIMIT = 60 * 1024 * 1024

V_NORM_MIX, V_PW1_B_LO, V_PW1_B_HI, V_DW_B, V_LN_G, V_LN_B, V_PW2_B, V_NORM_MLP, V_NORM_PLE, V_NORM_FINAL = range(10)
VEC_ROWS = 16

F32 = jnp.float32
BF16 = jnp.bfloat16


def _dot(a, b):
    return jnp.dot(a, b, preferred_element_type=F32)


def _sigmoid(x):
    return 1.0 / (1.0 + jnp.exp(-x))


def _rms(x, g):
    return x * lax.rsqrt(jnp.mean(x * x, axis=-1, keepdims=True) + EPS) * g


def _row(vec_ref, i):
    return vec_ref[i:i + 1, :]


def _layernorm_silu(c, g, b):
    mu = jnp.mean(c, axis=-1, keepdims=True)
    d = c - mu
    var = jnp.mean(d * d, axis=-1, keepdims=True)
    y = d * lax.rsqrt(var + EPS) * g + b
    return y * _sigmoid(y)


def _sink_softmax(logits, sink):
    m = jnp.maximum(jnp.max(logits, axis=-1, keepdims=True), sink)
    e = jnp.exp(logits - m)
    return e / (jnp.sum(e, axis=-1, keepdims=True) + jnp.exp(sink - m))


def _kv_lane_mask(shape, kh):
    lane = lax.broadcasted_iota(jnp.int32, shape, len(shape) - 1)
    return (lane >= kh * HEAD_DIM) & (lane < (kh + 1) * HEAD_DIM)


def _mlp_ple_kernel(x_ref, p_ref, vec_ref, w1_ref, w2_ref, gate_ref, proj_ref, o_ref, *, final):
    x = x_ref[...]
    h = _rms(x, _row(vec_ref, V_NORM_MLP)).astype(BF16)
    for c in range(D_FF // FF_CHUNK):
        hid = _dot(h, w1_ref[:, c * FF_CHUNK:(c + 1) * FF_CHUNK])
        hid = jnp.square(jnp.maximum(hid, 0.0)).astype(BF16)
        x = x + _dot(hid, w2_ref[c * FF_CHUNK:(c + 1) * FF_CHUNK, :])
    hg = _rms(x, _row(vec_ref, V_NORM_PLE)).astype(BF16)
    gate = _sigmoid(_dot(hg, gate_ref[...]))
    x = x + _dot(p_ref[...].astype(BF16), proj_ref[...]) * gate
    if final:
        x = _rms(x, _row(vec_ref, V_NORM_FINAL))
    o_ref[...] = x


def _const_spec(shape):
    nd = len(shape)
    return pl.BlockSpec(shape, lambda *_: (0,) * nd, pipeline_mode=pl.Buffered(1))


def _mlp_ple(x2d, p4d, layer, lw, tile_m, final):
    rows = x2d.shape[0]
    return pl.pallas_call(
        functools.partial(_mlp_ple_kernel, final=final),
        grid=(rows // tile_m,),
        in_specs=[
            pl.BlockSpec((tile_m, D_MODEL), lambda t: (t, 0)),
            pl.BlockSpec((None, tile_m, PLE_DIM), lambda t: (layer, t, 0)),
            _const_spec((VEC_ROWS, D_MODEL)),
            _const_spec((D_MODEL, D_FF)),
            _const_spec((D_FF, D_MODEL)),
            _const_spec((D_MODEL, D_MODEL)),
            _const_spec((PLE_DIM, D_MODEL)),
        ],
        out_specs=pl.BlockSpec((tile_m, D_MODEL), lambda t: (t, 0)),
        out_shape=jax.ShapeDtypeStruct((rows, D_MODEL), F32),
        compiler_params=pltpu.CompilerParams(
            dimension_semantics=("arbitrary",), vmem_limit_bytes=VMEM_LIMIT),
        name=f"mlp_ple_l{layer}_r{rows}",
    )(x2d, p4d, lw["vec"], lw["w1"], lw["w2"], lw["gate"], lw["proj"])


def _glu(x, vec_ref, pw1_ref):
    h = _rms(x, _row(vec_ref, V_NORM_MIX)).astype(BF16)
    a = _dot(h, pw1_ref[:, :D_MODEL]) + _row(vec_ref, V_PW1_B_LO)
    g = _dot(h, pw1_ref[:, D_MODEL:]) + _row(vec_ref, V_PW1_B_HI)
    return a * _sigmoid(g)


def _conv_prompt_kernel(x_ref, vec_ref, dw_ref, pw1_ref, pw2_ref, o_ref, cs_ref, ext_ref, act_ref, *, tile_m):
    t = pl.program_id(1)
    ext_rows = tile_m + HALO

    @pl.when(t == 0)
    def _():
        ext_ref[0, 0:HALO, :] = jnp.zeros((HALO, D_MODEL), F32)
        ext_ref[0, ext_rows:ext_rows + SUBLANES, :] = jnp.zeros((SUBLANES, D_MODEL), F32)

    x = x_ref[0]
    ext_ref[0, HALO:ext_rows, :] = _glu(x, vec_ref, pw1_ref)
    for s in range(1, SUBLANES):
        ext_ref[s, 0:ext_rows, :] = ext_ref[0, s:s + ext_rows, :]

    dw_b = _row(vec_ref, V_DW_B)
    ln_g = _row(vec_ref, V_LN_G)
    ln_b = _row(vec_ref, V_LN_B)

    def block(r, carry):
        r0 = pl.multiple_of(r * CONV_ROWS, CONV_ROWS)
        acc = jnp.broadcast_to(dw_b, (CONV_ROWS, D_MODEL))
        for k in range(CONV_WIDTH):
            j, s = divmod(k + HALO - (CONV_WIDTH - 1), SUBLANES)
            acc = acc + dw_ref[k:k + 1, :] * ext_ref[s, pl.ds(r0 + SUBLANES * j, CONV_ROWS), :]
        act_ref[pl.ds(r0, CONV_ROWS), :] = _layernorm_silu(acc, ln_g, ln_b).astype(BF16)
        return carry

    lax.fori_loop(0, tile_m // CONV_ROWS, block, 0)

    cs_ref[0] = ext_ref[0, ext_rows - (CONV_WIDTH - 1):ext_rows, :]
    ext_ref[0, 0:HALO, :] = ext_ref[0, tile_m:ext_rows, :]
    o_ref[0] = x + _dot(act_ref[...], pw2_ref[...]) + _row(vec_ref, V_PW2_B)


def _conv_prompt(x, lw, tile_m):
    b, s, _ = x.shape
    return pl.pallas_call(
        functools.partial(_conv_prompt_kernel, tile_m=tile_m),
        grid=(b, s // tile_m),
        in_specs=[
            pl.BlockSpec((1, tile_m, D_MODEL), lambda i, t: (i, t, 0)),
            _const_spec((VEC_ROWS, D_MODEL)),
            _const_spec((CONV_WIDTH + 1, D_MODEL)),
            _const_spec((D_MODEL, 2 * D_MODEL)),
            _const_spec((D_MODEL, D_MODEL)),
        ],
        out_specs=[
            pl.BlockSpec((1, tile_m, D_MODEL), lambda i, t: (i, t, 0)),
            pl.BlockSpec((1, CONV_WIDTH - 1, D_MODEL), lambda i, t: (i, 0, 0)),
        ],
        out_shape=[
            jax.ShapeDtypeStruct((b, s, D_MODEL), F32),
            jax.ShapeDtypeStruct((b, CONV_WIDTH - 1, D_MODEL), F32),
        ],
        scratch_shapes=[
            pltpu.VMEM((SUBLANES, tile_m + HALO + SUBLANES, D_MODEL), F32),
            pltpu.VMEM((tile_m, D_MODEL), BF16),
        ],
        compiler_params=pltpu.CompilerParams(
            dimension_semantics=("arbitrary", "arbitrary"), vmem_limit_bytes=VMEM_LIMIT),
        name="conv_prompt",
    )(x, lw["vec"], lw["dw"], lw["pw1"], lw["pw2"])


def _conv_sample_kernel(x_ref, left_ref, vec_ref, dw_ref, pw1_ref, pw2_ref, o_ref, cs_ref, ext_ref, *, streams, steps):
    x = x_ref[...]
    ext_ref[:, 0:HALO - (CONV_WIDTH - 1), :] = jnp.zeros((streams, HALO - (CONV_WIDTH - 1), D_MODEL), F32)
    ext_ref[:, HALO - (CONV_WIDTH - 1):HALO, :] = left_ref[...]
    ext_ref[:, HALO:HALO + steps, :] = _glu(x, vec_ref, pw1_ref).reshape(streams, steps, D_MODEL)
    acc = jnp.broadcast_to(_row(vec_ref, V_DW_B).reshape(1, 1, D_MODEL), (streams, steps, D_MODEL))
    for k in range(CONV_WIDTH):
        lo = k + HALO - (CONV_WIDTH - 1)
        acc = acc + dw_ref[k:k + 1, :].reshape(1, 1, D_MODEL) * ext_ref[:, lo:lo + steps, :]
    act = _layernorm_silu(acc.reshape(streams * steps, D_MODEL), _row(vec_ref, V_LN_G), _row(vec_ref, V_LN_B))
    cs_ref[...] = ext_ref[:, HALO + steps - (CONV_WIDTH - 1):HALO + steps, :]
    o_ref[...] = x + _dot(act.astype(BF16), pw2_ref[...]) + _row(vec_ref, V_PW2_B)


def _conv_sample(x2d, left, lw, streams, steps):
    rows = streams * steps
    return pl.pallas_call(
        functools.partial(_conv_sample_kernel, streams=streams, steps=steps),
        grid=(1,),
        in_specs=[
            _const_spec((rows, D_MODEL)),
            _const_spec((streams, CONV_WIDTH - 1, D_MODEL)),
            _const_spec((VEC_ROWS, D_MODEL)),
            _const_spec((CONV_WIDTH + 1, D_MODEL)),
            _const_spec((D_MODEL, 2 * D_MODEL)),
            _const_spec((D_MODEL, D_MODEL)),
        ],
        out_specs=[
            pl.BlockSpec((rows, D_MODEL), lambda t: (0, 0)),
            pl.BlockSpec((streams, CONV_WIDTH - 1, D_MODEL), lambda t: (0, 0, 0)),
        ],
        out_shape=[
            jax.ShapeDtypeStruct((rows, D_MODEL), F32),
            jax.ShapeDtypeStruct((streams, CONV_WIDTH - 1, D_MODEL), F32),
        ],
        scratch_shapes=[pltpu.VMEM((streams, HALO + steps, D_MODEL), F32)],
        compiler_params=pltpu.CompilerParams(
            dimension_semantics=("arbitrary",), vmem_limit_bytes=VMEM_LIMIT),
        name="conv_sample",
    )(x2d, left, lw["vec"], lw["dw"], lw["pw1"], lw["pw2"])


def _attn_prompt_kernel(x_ref, vec_ref, bias_ref, wq_ref, wkv_ref, wo_ref, o_ref, ks_ref, vs_ref,
                        q_ref, kb_ref, vb_ref, att_ref, *, tile_m):
    t = pl.program_id(1)

    @pl.when(t == 0)
    def _():
        kb_ref[0:WINDOW, :] = jnp.zeros((WINDOW, KV_DIM), BF16)
        vb_ref[0:WINDOW, :] = jnp.zeros((WINDOW, KV_DIM), BF16)

    x = x_ref[0]
    h = _rms(x, _row(vec_ref, V_NORM_MIX)).astype(BF16)
    q_ref[...] = _dot(h, wq_ref[...]).astype(BF16)
    kv = _dot(h, wkv_ref[...])
    k_new = kv[:, :KV_DIM]
    v_new = kv[:, KV_DIM:]
    kb_ref[WINDOW:WINDOW + tile_m, :] = k_new.astype(BF16)
    vb_ref[WINDOW:WINDOW + tile_m, :] = v_new.astype(BF16)
    ks_ref[0] = k_new[tile_m - WINDOW:, :]
    vs_ref[0] = v_new[tile_m - WINDOW:, :]

    rows = GROUP * CHUNK

    def chunk(c, carry):
        c0 = pl.multiple_of(c * CHUNK, CHUNK)
        q = jnp.concatenate(
            [q_ref[pl.ds(c0, CHUNK), g * KV_DIM:(g + 1) * KV_DIM] for g in range(GROUP)], axis=0)
        kb = kb_ref[pl.ds(c0, BAND), :]
        vb = vb_ref[pl.ds(c0, BAND), :]
        key_pos = t * tile_m + c0 - WINDOW + lax.broadcasted_iota(jnp.int32, (rows, BAND), 1)
        out = jnp.zeros((rows, KV_DIM), F32)
        for kh in range(N_KV_HEADS):
            k_kh = jnp.where(_kv_lane_mask(kb.shape, kh), kb, jnp.zeros_like(kb))
            logits = lax.dot_general(q, k_kh, (((1,), (1,)), ((), ())), preferred_element_type=F32)
            logits = logits * (HEAD_DIM ** -0.5) + bias_ref[kh, :, 0:BAND]
            logits = jnp.where(key_pos >= 0, logits, NEG_INF)
            probs = _sink_softmax(logits, bias_ref[kh, :, BAND:BAND + 1])
            o_kh = _dot(probs.astype(BF16), vb)
            out = jnp.where(_kv_lane_mask(out.shape, kh), o_kh, out)
        for g in range(GROUP):
            att_ref[pl.ds(c0, CHUNK), g * KV_DIM:(g + 1) * KV_DIM] = out[g * CHUNK:(g + 1) * CHUNK, :].astype(BF16)
        return carry

    lax.fori_loop(0, tile_m // CHUNK, chunk, 0)

    kb_ref[0:WINDOW, :] = kb_ref[tile_m:tile_m + WINDOW, :]
    vb_ref[0:WINDOW, :] = vb_ref[tile_m:tile_m + WINDOW, :]
    o_ref[0] = x + _dot(att_ref[...], wo_ref[...])


def _attn_prompt(x, lw, bias, tile_m):
    b, s, _ = x.shape
    return pl.pallas_call(
        functools.partial(_attn_prompt_kernel, tile_m=tile_m),
        grid=(b, s // tile_m),
        in_specs=[
            pl.BlockSpec((1, tile_m, D_MODEL), lambda i, t: (i, t, 0)),
            _const_spec((VEC_ROWS, D_MODEL)),
            _const_spec(bias.shape),
            _const_spec((D_MODEL, D_MODEL)),
            _const_spec((D_MODEL, 2 * KV_DIM)),
            _const_spec((D_MODEL, D_MODEL)),
        ],
        out_specs=[
            pl.BlockSpec((1, tile_m, D_MODEL), lambda i, t: (i, t, 0)),
            pl.BlockSpec((1, WINDOW, KV_DIM), lambda i, t: (i, 0, 0)),
            pl.BlockSpec((1, WINDOW, KV_DIM), lambda i, t: (i, 0, 0)),
        ],
        out_shape=[
            jax.ShapeDtypeStruct((b, s, D_MODEL), F32),
            jax.ShapeDtypeStruct((b, WINDOW, KV_DIM), F32),
            jax.ShapeDtypeStruct((b, WINDOW, KV_DIM), F32),
        ],
        scratch_shapes=[
            pltpu.VMEM((tile_m, D_MODEL), BF16),
            pltpu.VMEM((tile_m + WINDOW, KV_DIM), BF16),
            pltpu.VMEM((tile_m + WINDOW, KV_DIM), BF16),
            pltpu.VMEM((tile_m, D_MODEL), BF16),
        ],
        compiler_params=pltpu.CompilerParams(
            dimension_semantics=("arbitrary", "arbitrary"), vmem_limit_bytes=VMEM_LIMIT),
        name="attn_prompt",
    )(x, lw["vec"], bias, lw["wq"], lw["wkv"], lw["wo"])


def _attn_sample_kernel(x_ref, kc_ref, vc_ref, vec_ref, bias_ref, wq_ref, wkv_ref, wo_ref,
                        o_ref, ks_ref, vs_ref, kf_ref, vf_ref, *, streams, steps):
    cache = kc_ref.shape[1]
    keys = cache + steps
    x = x_ref[...]
    h = _rms(x, _row(vec_ref, V_NORM_MIX)).astype(BF16)
    q = _dot(h, wq_ref[...]).reshape(streams, steps, D_MODEL)
    kv = _dot(h, wkv_ref[...])
    kf_ref[:, 0:cache, :] = kc_ref[...]
    vf_ref[:, 0:cache, :] = vc_ref[...]
    kf_ref[:, cache:keys, :] = kv[:, :KV_DIM].reshape(streams, steps, KV_DIM)
    vf_ref[:, cache:keys, :] = kv[:, KV_DIM:].reshape(streams, steps, KV_DIM)
    ks_ref[...] = kf_ref[:, steps:keys, :]
    vs_ref[...] = vf_ref[:, steps:keys, :]

    qs = jnp.concatenate([q[:, :, g * KV_DIM:(g + 1) * KV_DIM] for g in range(GROUP)], axis=1).astype(BF16)
    kf = kf_ref[...].astype(BF16)
    vf = vf_ref[...].astype(BF16)
    out = jnp.zeros((streams, GROUP * steps, KV_DIM), F32)
    for kh in range(N_KV_HEADS):
        k_kh = jnp.where(_kv_lane_mask(kf.shape, kh), kf, jnp.zeros_like(kf))
        logits = jnp.einsum("bqd,bkd->bqk", qs, k_kh, preferred_element_type=F32)
        logits = logits * (HEAD_DIM ** -0.5) + bias_ref[kh, :, 0:keys][None]
        probs = _sink_softmax(logits, bias_ref[kh, :, keys:keys + 1][None])
        o_kh = jnp.einsum("bqk,bkd->bqd", probs.astype(BF16), vf, preferred_element_type=F32)
        out = jnp.where(_kv_lane_mask(out.shape, kh), o_kh, out)
    att = jnp.concatenate([out[:, g * steps:(g + 1) * steps, :] for g in range(GROUP)], axis=2)
    o_ref[...] = x + _dot(att.reshape(streams * steps, D_MODEL).astype(BF16), wo_ref[...])


def _attn_sample(x2d, k_left, v_left, lw, bias, streams, steps):
    rows = streams * steps
    cache = k_left.shape[1]
    return pl.pallas_call(
        functools.partial(_attn_sample_kernel, streams=streams, steps=steps),
        grid=(1,),
        in_specs=[
            _const_spec((rows, D_MODEL)),
            _const_spec(k_left.shape),
            _const_spec(v_left.shape),
            _const_spec((VEC_ROWS, D_MODEL)),
            _const_spec(bias.shape),
            _const_spec((D_MODEL, D_MODEL)),
            _const_spec((D_MODEL, 2 * KV_DIM)),
            _const_spec((D_MODEL, D_MODEL)),
        ],
        out_specs=[
            pl.BlockSpec((rows, D_MODEL), lambda t: (0, 0)),
            pl.BlockSpec((streams, cache, KV_DIM), lambda t: (0, 0, 0)),
            pl.BlockSpec((streams, cache, KV_DIM), lambda t: (0, 0, 0)),
        ],
        out_shape=[
            jax.ShapeDtypeStruct((rows, D_MODEL), F32),
            jax.ShapeDtypeStruct((streams, cache, KV_DIM), F32),
            jax.ShapeDtypeStruct((streams, cache, KV_DIM), F32),
        ],
        scratch_shapes=[
            pltpu.VMEM((streams, cache + steps, KV_DIM), F32),
            pltpu.VMEM((streams, cache + steps, KV_DIM), F32),
        ],
        compiler_params=pltpu.CompilerParams(
            dimension_semantics=("arbitrary",), vmem_limit_bytes=VMEM_LIMIT),
        name="attn_sample",
    )(x2d, k_left, v_left, lw["vec"], bias, lw["wq"], lw["wkv"], lw["wo"])


def _t5_bucket(rel):
    nb = NUM_BUCKETS // 2
    max_exact = nb // 2
    ret = (rel > 0).astype(jnp.int32) * nb
    n = jnp.abs(rel)
    nf = jnp.maximum(n, 1).astype(F32)
    large = max_exact + (jnp.log(nf / max_exact) / math.log(MAX_DISTANCE / max_exact)
                         * (nb - max_exact)).astype(jnp.int32)
    large = jnp.minimum(large, nb - 1)
    return ret + jnp.where(n < max_exact, n, large)


def _bias_table(rel_bias, sinks, q_len, k_len, offset):
    rel = jnp.arange(k_len, dtype=jnp.int32)[None, :] - offset - jnp.arange(q_len, dtype=jnp.int32)[:, None]
    b = rel_bias.astype(F32)[_t5_bucket(rel)]
    b = jnp.transpose(b, (2, 0, 1)).reshape(N_KV_HEADS, GROUP * q_len, k_len)
    s = jnp.repeat(sinks.astype(F32).reshape(N_KV_HEADS, GROUP), q_len, axis=1)[:, :, None]
    pad = jnp.zeros((N_KV_HEADS, GROUP * q_len, 256 - k_len - 1), F32)
    return jnp.concatenate([b, s, pad], axis=2)


def _pack_vec(rows):
    table = jnp.zeros((VEC_ROWS, D_MODEL), F32)
    for i, r in rows.items():
        table = table.at[i].set(r.astype(F32))
    return table


def kernel(x_prompt, x_sample, cache_conv, cache_k, cache_v, p_prompt, p_sample, rel_bias, norm_mix, norm_mlp, norm_ple, norm_final, conv_pw1, conv_pw1_b, conv_dw, conv_dw_b, conv_ln_g, conv_ln_b, conv_pw2, conv_pw2_b, attn_wq, attn_wk, attn_wv, attn_wo, attn_sinks, mlp_w1, mlp_w2, ple_proj, ple_gate):
    batch, seq, _ = x_prompt.shape
    streams, steps, _ = x_sample.shape
    cache = cache_k.shape[2]
    assert seq % TILE_M == 0 and TILE_M % CHUNK == 0 and TILE_M >= WINDOW and steps % SUBLANES == 0

    layers = []
    for i in range(DEPTH):
        j = i // 2
        rows = {V_NORM_MIX: norm_mix[i], V_NORM_MLP: norm_mlp[i], V_NORM_PLE: norm_ple[i], V_NORM_FINAL: norm_final}
        lw = {"w1": mlp_w1[i].astype(BF16), "w2": mlp_w2[i].astype(BF16),
              "gate": ple_gate[i].astype(BF16), "proj": ple_proj[i].astype(BF16)}
        if i % 2 == 0:
            rows.update({V_PW1_B_LO: conv_pw1_b[j, :D_MODEL], V_PW1_B_HI: conv_pw1_b[j, D_MODEL:],
                         V_DW_B: conv_dw_b[j], V_LN_G: conv_ln_g[j], V_LN_B: conv_ln_b[j], V_PW2_B: conv_pw2_b[j]})
            lw["pw1"] = conv_pw1[j].astype(BF16)
            lw["pw2"] = conv_pw2[j].astype(BF16)
            lw["dw"] = jnp.concatenate([conv_dw[j].astype(F32), jnp.zeros((1, D_MODEL), F32)], axis=0)
        else:
            wq = attn_wq[j].reshape(D_MODEL, N_KV_HEADS, GROUP, HEAD_DIM).transpose(0, 2, 1, 3)
            lw["wq"] = wq.reshape(D_MODEL, D_MODEL).astype(BF16)
            lw["wkv"] = jnp.concatenate([attn_wk[j], attn_wv[j]], axis=1).astype(BF16)
            wo = attn_wo[j].reshape(N_KV_HEADS, GROUP, HEAD_DIM, D_MODEL).transpose(1, 0, 2, 3)
            lw["wo"] = wo.reshape(D_MODEL, D_MODEL).astype(BF16)
            lw["bias_prompt"] = _bias_table(rel_bias, attn_sinks[j], CHUNK, BAND, WINDOW)
            lw["bias_sample"] = _bias_table(rel_bias, attn_sinks[j], steps, cache + steps, cache)
        lw["vec"] = _pack_vec(rows)
        layers.append(lw)

    p_prompt2 = p_prompt.reshape(DEPTH, batch * seq, PLE_DIM)
    p_sample2 = p_sample.reshape(DEPTH, streams * steps, PLE_DIM)

    xp = x_prompt
    xs = x_sample.reshape(streams * steps, D_MODEL)
    conv_p, k_p, v_p, conv_s, k_s, v_s = [], [], [], [], [], []
    for i, lw in enumerate(layers):
        j = i // 2
        final = i == DEPTH - 1
        if i % 2 == 0:
            xp, cs = _conv_prompt(xp, lw, TILE_M)
            conv_p.append(cs)
            xs, cs = _conv_sample(xs, cache_conv[j], lw, streams, steps)
            conv_s.append(cs)
        else:
            xp, ks, vs = _attn_prompt(xp, lw, lw["bias_prompt"], TILE_M)
            k_p.append(ks)
            v_p.append(vs)
            xs, ks, vs = _attn_sample(xs, cache_k[j].reshape(streams, cache, KV_DIM),
                                      cache_v[j].reshape(streams, cache, KV_DIM), lw, lw["bias_sample"], streams, steps)
            k_s.append(ks)
            v_s.append(vs)
        xp = _mlp_ple(xp.reshape(batch * seq, D_MODEL), p_prompt2, i, lw, TILE_M, final).reshape(batch, seq, D_MODEL)
        xs = _mlp_ple(xs, p_sample2, i, lw, streams * steps, final)

    def heads(ts, n):
        return jnp.stack(ts).reshape(len(ts), n, -1, N_KV_HEADS, HEAD_DIM)

    return (xp, xs.reshape(streams, steps, D_MODEL), jnp.stack(conv_p), heads(k_p, batch), heads(v_p, batch),
            jnp.stack(conv_s), heads(k_s, streams), heads(v_s, streams))
```

```python
import functools
import math

import jax
import jax.numpy as jnp
from jax import lax
from jax.experimental import pallas as pl
from jax.experimental.pallas import tpu as pltpu

D_MODEL = 1024
DEPTH = 4
CHUNK = 64
CONV_WIDTH = 31
WINDOW = 128
N_HEADS = 16
N_KV_HEADS = 4
GROUP = N_HEADS // N_KV_HEADS
HEAD_DIM = 64
KV_DIM = N_KV_HEADS * HEAD_DIM
D_FF = 4 * D_MODEL
PLE_DIM = 256
NUM_BUCKETS = 32
MAX_DISTANCE = 128
EPS = 1e-6
NEG_INF = -1e30

BAND = WINDOW + CHUNK
HALO = 32
SUBLANES = 8
CONV_ROWS = 32
FF_CHUNK = 1024
TILE_M = 512
VMEM_LIMIT = 60 * 1024 * 1024

V_NORM_MIX, V_PW1_B_LO, V_PW1_B_HI, V_DW_B, V_LN_G, V_LN_B, V_PW2_B, V_NORM_MLP, V_NORM_PLE, V_NORM_FINAL = range(10)
VEC_ROWS = 16

F32 = jnp.float32
BF16 = jnp.bfloat16


def _dot(a, b):
    return jnp.dot(a, b, preferred_element_type=F32)


def _sigmoid(x):
    return 1.0 / (1.0 + jnp.exp(-x))


def _rms(x, g):
    return x * lax.rsqrt(jnp.mean(x * x, axis=-1, keepdims=True) + EPS) * g


def _row(vec_ref, i):
    return vec_ref[i:i + 1, :]


def _layernorm_silu(c, g, b):
    mu = jnp.mean(c, axis=-1, keepdims=True)
    d = c - mu
    var = jnp.mean(d * d, axis=-1, keepdims=True)
    y = d * lax.rsqrt(var + EPS) * g + b
    return y * _sigmoid(y)


def _sink_softmax(logits, sink):
    m = jnp.maximum(jnp.max(logits, axis=-1, keepdims=True), sink)
    e = jnp.exp(logits - m)
    return e / (jnp.sum(e, axis=-1, keepdims=True) + jnp.exp(sink - m))


def _kv_lane_mask(shape, kh):
    lane = lax.broadcasted_iota(jnp.int32, shape, len(shape) - 1)
    return (lane >= kh * HEAD_DIM) & (lane < (kh + 1) * HEAD_DIM)


def _mlp_ple_kernel(x_ref, p_ref, vec_ref, w1_ref, w2_ref, gate_ref, proj_ref, o_ref, *, final):
    x = x_ref[...]
    h = _rms(x, _row(vec_ref, V_NORM_MLP)).astype(BF16)
    for c in range(D_FF // FF_CHUNK):
        hid = _dot(h, w1_ref[:, c * FF_CHUNK:(c + 1) * FF_CHUNK])
        hid = jnp.square(jnp.maximum(hid, 0.0)).astype(BF16)
        x = x + _dot(hid, w2_ref[c * FF_CHUNK:(c + 1) * FF_CHUNK, :])
    hg = _rms(x, _row(vec_ref, V_NORM_PLE)).astype(BF16)
    gate = _sigmoid(_dot(hg, gate_ref[...]))
    x = x + _dot(p_ref[...].astype(BF16), proj_ref[...]) * gate
    if final:
        x = _rms(x, _row(vec_ref, V_NORM_FINAL))
    o_ref[...] = x


def _const_spec(shape):
    nd = len(shape)
    return pl.BlockSpec(shape, lambda *_: (0,) * nd, pipeline_mode=pl.Buffered(1))


def _mlp_ple(x2d, p4d, layer, lw, tile_m, final):
    rows = x2d.shape[0]
    return pl.pallas_call(
        functools.partial(_mlp_ple_kernel, final=final),
        grid=(rows // tile_m,),
        in_specs=[
            pl.BlockSpec((tile_m, D_MODEL), lambda t: (t, 0)),
            pl.BlockSpec((None, tile_m, PLE_DIM), lambda t: (layer, t, 0)),
            _const_spec((VEC_ROWS, D_MODEL)),
            _const_spec((D_MODEL, D_FF)),
            _const_spec((D_FF, D_MODEL)),
            _const_spec((D_MODEL, D_MODEL)),
            _const_spec((PLE_DIM, D_MODEL)),
        ],
        out_specs=pl.BlockSpec((tile_m, D_MODEL), lambda t: (t, 0)),
        out_shape=jax.ShapeDtypeStruct((rows, D_MODEL), F32),
        compiler_params=pltpu.CompilerParams(
            dimension_semantics=("arbitrary",), vmem_limit_bytes=VMEM_LIMIT),
        name=f"mlp_ple_l{layer}_r{rows}",
    )(x2d, p4d, lw["vec"], lw["w1"], lw["w2"], lw["gate"], lw["proj"])


def _glu(x, vec_ref, pw1_ref):
    h = _rms(x, _row(vec_ref, V_NORM_MIX)).astype(BF16)
    a = _dot(h, pw1_ref[:, :D_MODEL]) + _row(vec_ref, V_PW1_B_LO)
    g = _dot(h, pw1_ref[:, D_MODEL:]) + _row(vec_ref, V_PW1_B_HI)
    return a * _sigmoid(g)


def _conv_prompt_kernel(x_ref, vec_ref, dwb_ref, pw1_ref, pw2_ref, o_ref, cs_ref, ext_ref, act_ref, *, tile_m):
    t = pl.program_id(1)
    ext_rows = tile_m + HALO

    @pl.when(t == 0)
    def _():
        ext_ref[0, 0:HALO, :] = jnp.zeros((HALO, D_MODEL), F32)
        ext_ref[0, ext_rows:ext_rows + SUBLANES, :] = jnp.zeros((SUBLANES, D_MODEL), F32)

    x = x_ref[0]
    ext_ref[0, HALO:ext_rows, :] = _glu(x, vec_ref, pw1_ref)
    for s in range(1, SUBLANES):
        ext_ref[s, 0:ext_rows, :] = ext_ref[0, s:s + ext_rows, :]

    dw_b = _row(vec_ref, V_DW_B)
    ln_g = _row(vec_ref, V_LN_G)
    ln_b = _row(vec_ref, V_LN_B)

    def block(r, carry):
        r0 = pl.multiple_of(r * CONV_ROWS, CONV_ROWS)
        groups = CONV_ROWS // SUBLANES
        acc = [jnp.broadcast_to(dw_b, (SUBLANES, D_MODEL))] * groups
        for k in range(CONV_WIDTH):
            j, s = divmod(k + HALO - (CONV_WIDTH - 1), SUBLANES)
            w = dwb_ref[k]
            for i in range(groups):
                acc[i] = acc[i] + w * ext_ref[s, pl.ds(r0 + SUBLANES * (i + j), SUBLANES), :]
        act = _layernorm_silu(jnp.concatenate(acc, axis=0), ln_g, ln_b)
        act_ref[pl.ds(r0, CONV_ROWS), :] = act.astype(BF16)
        return carry

    lax.fori_loop(0, tile_m // CONV_ROWS, block, 0)

    cs_ref[0] = ext_ref[0, ext_rows - (CONV_WIDTH - 1):ext_rows, :]
    ext_ref[0, 0:HALO, :] = ext_ref[0, tile_m:ext_rows, :]
    o_ref[0] = x + _dot(act_ref[...], pw2_ref[...]) + _row(vec_ref, V_PW2_B)


def _conv_prompt(x, lw, tile_m):
    b, s, _ = x.shape
    return pl.pallas_call(
        functools.partial(_conv_prompt_kernel, tile_m=tile_m),
        grid=(b, s // tile_m),
        in_specs=[
            pl.BlockSpec((1, tile_m, D_MODEL), lambda i, t: (i, t, 0)),
            _const_spec((VEC_ROWS, D_MODEL)),
            _const_spec((CONV_WIDTH, SUBLANES, D_MODEL)),
            _const_spec((D_MODEL, 2 * D_MODEL)),
            _const_spec((D_MODEL, D_MODEL)),
        ],
        out_specs=[
            pl.BlockSpec((1, tile_m, D_MODEL), lambda i, t: (i, t, 0)),
            pl.BlockSpec((1, CONV_WIDTH - 1, D_MODEL), lambda i, t: (i, 0, 0)),
        ],
        out_shape=[
            jax.ShapeDtypeStruct((b, s, D_MODEL), F32),
            jax.ShapeDtypeStruct((b, CONV_WIDTH - 1, D_MODEL), F32),
        ],
        scratch_shapes=[
            pltpu.VMEM((SUBLANES, tile_m + HALO + SUBLANES, D_MODEL), F32),
            pltpu.VMEM((tile_m, D_MODEL), BF16),
        ],
        compiler_params=pltpu.CompilerParams(
            dimension_semantics=("arbitrary", "arbitrary"), vmem_limit_bytes=VMEM_LIMIT),
        name="conv_prompt",
    )(x, lw["vec"], lw["dwb"], lw["pw1"], lw["pw2"])


def _conv_sample_kernel(x_ref, left_ref, vec_ref, dw_ref, pw1_ref, pw2_ref, o_ref, cs_ref, ext_ref, *, streams, steps):
    x = x_ref[...]
    ext_ref[:, 0:HALO - (CONV_WIDTH - 1), :] = jnp.zeros((streams, HALO - (CONV_WIDTH - 1), D_MODEL), F32)
    ext_ref[:, HALO - (CONV_WIDTH - 1):HALO, :] = left_ref[...]
    ext_ref[:, HALO:HALO + steps, :] = _glu(x, vec_ref, pw1_ref).reshape(streams, steps, D_MODEL)
    acc = jnp.broadcast_to(_row(vec_ref, V_DW_B).reshape(1, 1, D_MODEL), (streams, steps, D_MODEL))
    for k in range(CONV_WIDTH):
        lo = k + HALO - (CONV_WIDTH - 1)
        acc = acc + dw_ref[k:k + 1, :].reshape(1, 1, D_MODEL) * ext_ref[:, lo:lo + steps, :]
    act = _layernorm_silu(acc.reshape(streams * steps, D_MODEL), _row(vec_ref, V_LN_G), _row(vec_ref, V_LN_B))
    cs_ref[...] = ext_ref[:, HALO + steps - (CONV_WIDTH - 1):HALO + steps, :]
    o_ref[...] = x + _dot(act.astype(BF16), pw2_ref[...]) + _row(vec_ref, V_PW2_B)


def _conv_sample(x2d, left, lw, streams, steps):
    rows = streams * steps
    return pl.pallas_call(
        functools.partial(_conv_sample_kernel, streams=streams, steps=steps),
        grid=(1,),
        in_specs=[
            _const_spec((rows, D_MODEL)),
            _const_spec((streams, CONV_WIDTH - 1, D_MODEL)),
            _const_spec((VEC_ROWS, D_MODEL)),
            _const_spec((CONV_WIDTH + 1, D_MODEL)),
            _const_spec((D_MODEL, 2 * D_MODEL)),
            _const_spec((D_MODEL, D_MODEL)),
        ],
        out_specs=[
            pl.BlockSpec((rows, D_MODEL), lambda t: (0, 0)),
            pl.BlockSpec((streams, CONV_WIDTH - 1, D_MODEL), lambda t: (0, 0, 0)),
        ],
        out_shape=[
            jax.ShapeDtypeStruct((rows, D_MODEL), F32),
            jax.ShapeDtypeStruct((streams, CONV_WIDTH - 1, D_MODEL), F32),
        ],
        scratch_shapes=[pltpu.VMEM((streams, HALO + steps, D_MODEL), F32)],
        compiler_params=pltpu.CompilerParams(
            dimension_semantics=("arbitrary",), vmem_limit_bytes=VMEM_LIMIT),
        name="conv_sample",
    )(x2d, left, lw["vec"], lw["dw"], lw["pw1"], lw["pw2"])


def _attn_prompt_kernel(x_ref, vec_ref, bias_ref, wq_ref, wkv_ref, wo_ref, o_ref, ks_ref, vs_ref,
                        q_ref, kb_ref, vb_ref, att_ref, *, tile_m):
    t = pl.program_id(1)

    @pl.when(t == 0)
    def _():
        kb_ref[0:WINDOW, :] = jnp.zeros((WINDOW, KV_DIM), BF16)
        vb_ref[0:WINDOW, :] = jnp.zeros((WINDOW, KV_DIM), BF16)

    x = x_ref[0]
    h = _rms(x, _row(vec_ref, V_NORM_MIX)).astype(BF16)
    q_ref[...] = _dot(h, wq_ref[...]).astype(BF16)
    kv = _dot(h, wkv_ref[...])
    k_new = kv[:, :KV_DIM]
    v_new = kv[:, KV_DIM:]
    kb_ref[WINDOW:WINDOW + tile_m, :] = k_new.astype(BF16)
    vb_ref[WINDOW:WINDOW + tile_m, :] = v_new.astype(BF16)
    ks_ref[0] = k_new[tile_m - WINDOW:, :]
    vs_ref[0] = v_new[tile_m - WINDOW:, :]

    rows = GROUP * CHUNK

    def chunk(c):
        c0 = pl.multiple_of(c * CHUNK, CHUNK)
        q = jnp.concatenate(
            [q_ref[pl.ds(c0, CHUNK), g * KV_DIM:(g + 1) * KV_DIM] for g in range(GROUP)], axis=0)
        qm = jnp.concatenate(
            [jnp.where(_kv_lane_mask(q.shape, kh), q, jnp.zeros_like(q)) for kh in range(N_KV_HEADS)], axis=0)
        kb = kb_ref[pl.ds(c0, BAND), :]
        vb = vb_ref[pl.ds(c0, BAND), :]
        logits = lax.dot_general(qm, kb, (((1,), (1,)), ((), ())), preferred_element_type=F32)
        logits = logits + bias_ref[:, 0:BAND]
        key_pos = t * tile_m + c0 - WINDOW + lax.broadcasted_iota(jnp.int32, (1, BAND), 1)
        logits = jnp.where(key_pos >= 0, logits, NEG_INF)
        probs = _sink_softmax(logits, bias_ref[:, BAND:BAND + 1])
        pv = _dot(probs.astype(BF16), vb)
        out = pv[0:rows, :]
        for kh in range(1, N_KV_HEADS):
            out = jnp.where(_kv_lane_mask(out.shape, kh), pv[kh * rows:(kh + 1) * rows, :], out)
        for g in range(GROUP):
            att_ref[pl.ds(c0, CHUNK), g * KV_DIM:(g + 1) * KV_DIM] = out[g * CHUNK:(g + 1) * CHUNK, :].astype(BF16)

    def chunk_pair(i, carry):
        chunk(2 * i)
        chunk(2 * i + 1)
        return carry

    lax.fori_loop(0, tile_m // (2 * CHUNK), chunk_pair, 0)

    kb_ref[0:WINDOW, :] = kb_ref[tile_m:tile_m + WINDOW, :]
    vb_ref[0:WINDOW, :] = vb_ref[tile_m:tile_m + WINDOW, :]
    o_ref[0] = x + _dot(att_ref[...], wo_ref[...])


def _attn_prompt(x, lw, bias, tile_m):
    b, s, _ = x.shape
    return pl.pallas_call(
        functools.partial(_attn_prompt_kernel, tile_m=tile_m),
        grid=(b, s // tile_m),
        in_specs=[
            pl.BlockSpec((1, tile_m, D_MODEL), lambda i, t: (i, t, 0)),
            _const_spec((VEC_ROWS, D_MODEL)),
            _const_spec(bias.shape),
            _const_spec((D_MODEL, D_MODEL)),
            _const_spec((D_MODEL, 2 * KV_DIM)),
            _const_spec((D_MODEL, D_MODEL)),
        ],
        out_specs=[
            pl.BlockSpec((1, tile_m, D_MODEL), lambda i, t: (i, t, 0)),
            pl.BlockSpec((1, WINDOW, KV_DIM), lambda i, t: (i, 0, 0)),
            pl.BlockSpec((1, WINDOW, KV_DIM), lambda i, t: (i, 0, 0)),
        ],
        out_shape=[
            jax.ShapeDtypeStruct((b, s, D_MODEL), F32),
            jax.ShapeDtypeStruct((b, WINDOW, KV_DIM), F32),
            jax.ShapeDtypeStruct((b, WINDOW, KV_DIM), F32),
        ],
        scratch_shapes=[
            pltpu.VMEM((tile_m, D_MODEL), BF16),
            pltpu.VMEM((tile_m + WINDOW, KV_DIM), BF16),
            pltpu.VMEM((tile_m + WINDOW, KV_DIM), BF16),
            pltpu.VMEM((tile_m, D_MODEL), BF16),
        ],
        compiler_params=pltpu.CompilerParams(
            dimension_semantics=("arbitrary", "arbitrary"), vmem_limit_bytes=VMEM_LIMIT),
        name="attn_prompt",
    )(x, lw["vec"], bias, lw["wq"], lw["wkv"], lw["wo"])


def _attn_sample_kernel(x_ref, kc_ref, vc_ref, vec_ref, bias_ref, wq_ref, wkv_ref, wo_ref,
                        o_ref, ks_ref, vs_ref, kf_ref, vf_ref, *, streams, steps):
    cache = kc_ref.shape[1]
    keys = cache + steps
    x = x_ref[...]
    h = _rms(x, _row(vec_ref, V_NORM_MIX)).astype(BF16)
    q = _dot(h, wq_ref[...]).reshape(streams, steps, D_MODEL)
    kv = _dot(h, wkv_ref[...])
    kf_ref[:, 0:cache, :] = kc_ref[...]
    vf_ref[:, 0:cache, :] = vc_ref[...]
    kf_ref[:, cache:keys, :] = kv[:, :KV_DIM].reshape(streams, steps, KV_DIM)
    vf_ref[:, cache:keys, :] = kv[:, KV_DIM:].reshape(streams, steps, KV_DIM)
    ks_ref[...] = kf_ref[:, steps:keys, :]
    vs_ref[...] = vf_ref[:, steps:keys, :]

    qs = jnp.concatenate([q[:, :, g * KV_DIM:(g + 1) * KV_DIM] for g in range(GROUP)], axis=1).astype(BF16)
    kf = kf_ref[...].astype(BF16)
    vf = vf_ref[...].astype(BF16)
    out = jnp.zeros((streams, GROUP * steps, KV_DIM), F32)
    for kh in range(N_KV_HEADS):
        k_kh = jnp.where(_kv_lane_mask(kf.shape, kh), kf, jnp.zeros_like(kf))
        logits = jnp.einsum("bqd,bkd->bqk", qs, k_kh, preferred_element_type=F32)
        logits = logits + bias_ref[kh, :, 0:keys][None]
        probs = _sink_softmax(logits, bias_ref[kh, :, keys:keys + 1][None])
        o_kh = jnp.einsum("bqk,bkd->bqd", probs.astype(BF16), vf, preferred_element_type=F32)
        out = jnp.where(_kv_lane_mask(out.shape, kh), o_kh, out)
    att = jnp.concatenate([out[:, g * steps:(g + 1) * steps, :] for g in range(GROUP)], axis=2)
    o_ref[...] = x + _dot(att.reshape(streams * steps, D_MODEL).astype(BF16), wo_ref[...])


def _attn_sample(x2d, k_left, v_left, lw, bias, streams, steps):
    rows = streams * steps
    cache = k_left.shape[1]
    return pl.pallas_call(
        functools.partial(_attn_sample_kernel, streams=streams, steps=steps),
        grid=(1,),
        in_specs=[
            _const_spec((rows, D_MODEL)),
            _const_spec(k_left.shape),
            _const_spec(v_left.shape),
            _const_spec((VEC_ROWS, D_MODEL)),
            _const_spec(bias.shape),
            _const_spec((D_MODEL, D_MODEL)),
            _const_spec((D_MODEL, 2 * KV_DIM)),
            _const_spec((D_MODEL, D_MODEL)),
        ],
        out_specs=[
            pl.BlockSpec((rows, D_MODEL), lambda t: (0, 0)),
            pl.BlockSpec((streams, cache, KV_DIM), lambda t: (0, 0, 0)),
            pl.BlockSpec((streams, cache, KV_DIM), lambda t: (0, 0, 0)),
        ],
        out_shape=[
            jax.ShapeDtypeStruct((rows, D_MODEL), F32),
            jax.ShapeDtypeStruct((streams, cache, KV_DIM), F32),
            jax.ShapeDtypeStruct((streams, cache, KV_DIM), F32),
        ],
        scratch_shapes=[
            pltpu.VMEM((streams, cache + steps, KV_DIM), F32),
            pltpu.VMEM((streams, cache + steps, KV_DIM), F32),
        ],
        compiler_params=pltpu.CompilerParams(
            dimension_semantics=("arbitrary",), vmem_limit_bytes=VMEM_LIMIT),
        name="attn_sample",
    )(x2d, k_left, v_left, lw["vec"], bias, lw["wq"], lw["wkv"], lw["wo"])


def _t5_bucket(rel):
    nb = NUM_BUCKETS // 2
    max_exact = nb // 2
    ret = (rel > 0).astype(jnp.int32) * nb
    n = jnp.abs(rel)
    nf = jnp.maximum(n, 1).astype(F32)
    large = max_exact + (jnp.log(nf / max_exact) / math.log(MAX_DISTANCE / max_exact)
                         * (nb - max_exact)).astype(jnp.int32)
    large = jnp.minimum(large, nb - 1)
    return ret + jnp.where(n < max_exact, n, large)


def _bias_table(rel_bias, sinks, q_len, k_len, offset):
    rel = jnp.arange(k_len, dtype=jnp.int32)[None, :] - offset - jnp.arange(q_len, dtype=jnp.int32)[:, None]
    onehot = (_t5_bucket(rel)[:, :, None] == jnp.arange(NUM_BUCKETS, dtype=jnp.int32)).astype(F32)
    b = jnp.einsum("qkn,nh->hqk", onehot, rel_bias.astype(F32), precision=lax.Precision.HIGHEST)
    b = b.reshape(N_KV_HEADS, GROUP * q_len, k_len)
    s = jnp.repeat(sinks.astype(F32).reshape(N_KV_HEADS, GROUP), q_len, axis=1)[:, :, None]
    pad = jnp.zeros((N_KV_HEADS, GROUP * q_len, 256 - k_len - 1), F32)
    return jnp.concatenate([b, s, pad], axis=2)


def _pack_vec(rows):
    zero = jnp.zeros((D_MODEL,), F32)
    return jnp.stack([rows[i].astype(F32) if i in rows else zero for i in range(VEC_ROWS)])


def kernel(x_prompt, x_sample, cache_conv, cache_k, cache_v, p_prompt, p_sample, rel_bias, norm_mix, norm_mlp, norm_ple, norm_final, conv_pw1, conv_pw1_b, conv_dw, conv_dw_b, conv_ln_g, conv_ln_b, conv_pw2, conv_pw2_b, attn_wq, attn_wk, attn_wv, attn_wo, attn_sinks, mlp_w1, mlp_w2, ple_proj, ple_gate):
    batch, seq, _ = x_prompt.shape
    streams, steps, _ = x_sample.shape
    cache = cache_k.shape[2]
    assert seq % TILE_M == 0 and TILE_M % CHUNK == 0 and TILE_M >= WINDOW and steps % SUBLANES == 0

    layers = []
    for i in range(DEPTH):
        j = i // 2
        rows = {V_NORM_MIX: norm_mix[i], V_NORM_MLP: norm_mlp[i], V_NORM_PLE: norm_ple[i], V_NORM_FINAL: norm_final}
        lw = {"w1": mlp_w1[i].astype(BF16), "w2": mlp_w2[i].astype(BF16),
              "gate": ple_gate[i].astype(BF16), "proj": ple_proj[i].astype(BF16)}
        if i % 2 == 0:
            rows.update({V_PW1_B_LO: conv_pw1_b[j, :D_MODEL], V_PW1_B_HI: conv_pw1_b[j, D_MODEL:],
                         V_DW_B: conv_dw_b[j], V_LN_G: conv_ln_g[j], V_LN_B: conv_ln_b[j], V_PW2_B: conv_pw2_b[j]})
            lw["pw1"] = conv_pw1[j].astype(BF16)
            lw["pw2"] = conv_pw2[j].astype(BF16)
            lw["dw"] = jnp.concatenate([conv_dw[j].astype(F32), jnp.zeros((1, D_MODEL), F32)], axis=0)
            lw["dwb"] = jnp.broadcast_to(conv_dw[j].astype(F32)[:, None, :], (CONV_WIDTH, SUBLANES, D_MODEL))
        else:
            wq = (attn_wq[j] * (HEAD_DIM ** -0.5)).reshape(D_MODEL, N_KV_HEADS, GROUP, HEAD_DIM).transpose(0, 2, 1, 3)
            lw["wq"] = wq.reshape(D_MODEL, D_MODEL).astype(BF16)
            lw["wkv"] = jnp.concatenate([attn_wk[j], attn_wv[j]], axis=1).astype(BF16)
            wo = attn_wo[j].reshape(N_KV_HEADS, GROUP, HEAD_DIM, D_MODEL).transpose(1, 0, 2, 3)
            lw["wo"] = wo.reshape(D_MODEL, D_MODEL).astype(BF16)
            lw["bias_prompt"] = _bias_table(rel_bias, attn_sinks[j], CHUNK, BAND, WINDOW).reshape(N_HEADS * CHUNK, 256)
            lw["bias_sample"] = _bias_table(rel_bias, attn_sinks[j], steps, cache + steps, cache)
        lw["vec"] = _pack_vec(rows)
        layers.append(lw)

    p_prompt2 = p_prompt.reshape(DEPTH, batch * seq, PLE_DIM)
    p_sample2 = p_sample.reshape(DEPTH, streams * steps, PLE_DIM)

    xp = x_prompt
    xs = x_sample.reshape(streams * steps, D_MODEL)
    conv_p, k_p, v_p, conv_s, k_s, v_s = [], [], [], [], [], []
    for i, lw in enumerate(layers):
        j = i // 2
        final = i == DEPTH - 1
        if i % 2 == 0:
            xp, cs = _conv_prompt(xp, lw, TILE_M)
            conv_p.append(cs)
            xs, cs = _conv_sample(xs, cache_conv[j], lw, streams, steps)
            conv_s.append(cs)
        else:
            xp, ks, vs = _attn_prompt(xp, lw, lw["bias_prompt"], TILE_M)
            k_p.append(ks)
            v_p.append(vs)
            xs, ks, vs = _attn_sample(xs, cache_k[j].reshape(streams, cache, KV_DIM),
                                      cache_v[j].reshape(streams, cache, KV_DIM), lw, lw["bias_sample"], streams, steps)
            k_s.append(ks)
            v_s.append(vs)
        xp = _mlp_ple(xp.reshape(batch * seq, D_MODEL), p_prompt2, i, lw, TILE_M, final).reshape(batch, seq, D_MODEL)
        xs = _mlp_ple(xs, p_sample2, i, lw, streams * steps, final)

    def heads(ts, n):
        return jnp.stack(ts).reshape(len(ts), n, -1, N_KV_HEADS, HEAD_DIM)

    return (xp, xs.reshape(streams, steps, D_MODEL), jnp.stack(conv_p), heads(k_p, batch), heads(v_p, batch),
            jnp.stack(conv_s), heads(k_s, streams), heads(v_s, streams))
```

```python
import functools
import math

import jax
import jax.numpy as jnp
from jax import lax
from jax.experimental import pallas as pl
from jax.experimental.pallas import tpu as pltpu

D_MODEL = 1024
DEPTH = 4
CHUNK = 64
CONV_WIDTH = 31
WINDOW = 128
N_HEADS = 16
N_KV_HEADS = 4
GROUP = N_HEADS // N_KV_HEADS
HEAD_DIM = 64
KV_DIM = N_KV_HEADS * HEAD_DIM
D_FF = 4 * D_MODEL
PLE_DIM = 256
NUM_BUCKETS = 32
MAX_DISTANCE = 128
EPS = 1e-6
NEG_INF = -1e30

BAND = WINDOW + CHUNK
KEYS_EXT = 256
HALO = 32
SUBLANES = 8
LANES = 128
CONV_ROWS = 64
FF_CHUNK = 1024
TILE_M = 512
MLP_TILE_M = 1024
VMEM_LIMIT = 60 * 1024 * 1024

V_NORM_MIX, V_PW1_B_LO, V_PW1_B_HI, V_DW_B, V_LN_G, V_LN_B, V_PW2_B, V_NORM_MLP, V_NORM_PLE, V_NORM_FINAL = range(10)
VEC_ROWS = 16

F32 = jnp.float32
BF16 = jnp.bfloat16


def _dot(a, b):
    return jnp.dot(a, b, preferred_element_type=F32)


def _sigmoid(x):
    return 1.0 / (1.0 + jnp.exp(-x))


def _rms(x, g):
    return x * lax.rsqrt(jnp.mean(x * x, axis=-1, keepdims=True) + EPS) * g


def _row(vec_ref, i):
    return vec_ref[i:i + 1, :]


def _layernorm_silu(c, g, b):
    mu = jnp.mean(c, axis=-1, keepdims=True)
    d = c - mu
    var = jnp.mean(d * d, axis=-1, keepdims=True)
    y = d * lax.rsqrt(var + EPS) * g + b
    return y * _sigmoid(y)


def _sink_softmax(logits, sink):
    m = jnp.maximum(jnp.max(logits, axis=-1, keepdims=True), sink)
    e = jnp.exp(logits - m)
    return e / (jnp.sum(e, axis=-1, keepdims=True) + jnp.exp(sink - m))


def _kv_lane_mask(shape, kh):
    lane = lax.broadcasted_iota(jnp.int32, shape, len(shape) - 1)
    return (lane >= kh * HEAD_DIM) & (lane < (kh + 1) * HEAD_DIM)


def _mlp_ple_kernel(x_ref, p_ref, vec_ref, w1_ref, w2_ref, gate_ref, proj_ref, o_ref, *, final):
    x = x_ref[...]
    h = _rms(x, _row(vec_ref, V_NORM_MLP)).astype(BF16)
    for c in range(D_FF // FF_CHUNK):
        hid = _dot(h, w1_ref[:, c * FF_CHUNK:(c + 1) * FF_CHUNK])
        hid = jnp.square(jnp.maximum(hid, 0.0)).astype(BF16)
        x = x + _dot(hid, w2_ref[c * FF_CHUNK:(c + 1) * FF_CHUNK, :])
    hg = _rms(x, _row(vec_ref, V_NORM_PLE)).astype(BF16)
    gate = _sigmoid(_dot(hg, gate_ref[...]))
    x = x + _dot(p_ref[...].astype(BF16), proj_ref[...]) * gate
    if final:
        x = _rms(x, _row(vec_ref, V_NORM_FINAL))
    o_ref[...] = x


def _const_spec(shape):
    nd = len(shape)
    return pl.BlockSpec(shape, lambda *_: (0,) * nd, pipeline_mode=pl.Buffered(1))


def _mlp_ple(x2d, p4d, layer, lw, tile_m, final):
    rows = x2d.shape[0]
    return pl.pallas_call(
        functools.partial(_mlp_ple_kernel, final=final),
        grid=(rows // tile_m,),
        in_specs=[
            pl.BlockSpec((tile_m, D_MODEL), lambda t: (t, 0)),
            pl.BlockSpec((None, tile_m, PLE_DIM), lambda t: (layer, t, 0)),
            _const_spec((VEC_ROWS, D_MODEL)),
            _const_spec((D_MODEL, D_FF)),
            _const_spec((D_FF, D_MODEL)),
            _const_spec((D_MODEL, D_MODEL)),
            _const_spec((PLE_DIM, D_MODEL)),
        ],
        out_specs=pl.BlockSpec((tile_m, D_MODEL), lambda t: (t, 0)),
        out_shape=jax.ShapeDtypeStruct((rows, D_MODEL), F32),
        compiler_params=pltpu.CompilerParams(
            dimension_semantics=("arbitrary",), vmem_limit_bytes=VMEM_LIMIT),
        name=f"mlp_ple_l{layer}_r{rows}",
    )(x2d, p4d, lw["vec"], lw["w1"], lw["w2"], lw["gate"], lw["proj"])


def _glu(x, vec_ref, pw1_ref):
    h = _rms(x, _row(vec_ref, V_NORM_MIX)).astype(BF16)
    a = _dot(h, pw1_ref[:, :D_MODEL]) + _row(vec_ref, V_PW1_B_LO)
    g = _dot(h, pw1_ref[:, D_MODEL:]) + _row(vec_ref, V_PW1_B_HI)
    return a * _sigmoid(g)


def _conv_prompt_kernel(x_ref, vec_ref, dwb_ref, pw1_ref, pw2_ref, o_ref, cs_ref, ext_ref, c_ref, *, tile_m):
    t = pl.program_id(1)
    ext_rows = tile_m + HALO
    first = HALO - (CONV_WIDTH - 1)
    groups = CONV_ROWS // SUBLANES
    spans = -(-(first + CONV_WIDTH) // SUBLANES)

    @pl.when(t == 0)
    def _():
        ext_ref[0:HALO, :] = jnp.zeros((HALO, D_MODEL), F32)
        ext_ref[ext_rows:ext_rows + SUBLANES, :] = jnp.zeros((SUBLANES, D_MODEL), F32)

    x = x_ref[0]
    ext_ref[HALO:ext_rows, :] = _glu(x, vec_ref, pw1_ref)

    def block(r, carry):
        r0 = pl.multiple_of(r * CONV_ROWS, CONV_ROWS)
        for l in range(D_MODEL // LANES):
            lanes = slice(l * LANES, (l + 1) * LANES)
            win = [ext_ref[pl.ds(r0 + SUBLANES * g, SUBLANES), lanes] for g in range(groups + spans)]
            out = None
            for s in range(SUBLANES):
                taps = [(j, SUBLANES * j + s - first) for j in range(spans)
                        if 0 <= SUBLANES * j + s - first < CONV_WIDTH]
                phase = []
                for g in range(groups + (1 if s else 0)):
                    p = None
                    for j, k in taps:
                        term = dwb_ref[k, :, lanes] * win[g + j]
                        p = term if p is None else p + term
                    phase.append(p)
                shifted = jnp.concatenate(phase, axis=0)[s:s + CONV_ROWS, :]
                out = shifted if out is None else out + shifted
            c_ref[pl.ds(r0, CONV_ROWS), lanes] = out
        return carry

    lax.fori_loop(0, tile_m // CONV_ROWS, block, 0)

    cs_ref[0] = ext_ref[ext_rows - (CONV_WIDTH - 1):ext_rows, :]
    ext_ref[0:HALO, :] = ext_ref[tile_m:ext_rows, :]
    act = _layernorm_silu(c_ref[...] + _row(vec_ref, V_DW_B), _row(vec_ref, V_LN_G), _row(vec_ref, V_LN_B))
    o_ref[0] = x + _dot(act.astype(BF16), pw2_ref[...]) + _row(vec_ref, V_PW2_B)


def _conv_prompt(x, lw, tile_m):
    b, s, _ = x.shape
    return pl.pallas_call(
        functools.partial(_conv_prompt_kernel, tile_m=tile_m),
        grid=(b, s // tile_m),
        in_specs=[
            pl.BlockSpec((1, tile_m, D_MODEL), lambda i, t: (i, t, 0)),
            _const_spec((VEC_ROWS, D_MODEL)),
            _const_spec((CONV_WIDTH, SUBLANES, D_MODEL)),
            _const_spec((D_MODEL, 2 * D_MODEL)),
            _const_spec((D_MODEL, D_MODEL)),
        ],
        out_specs=[
            pl.BlockSpec((1, tile_m, D_MODEL), lambda i, t: (i, t, 0)),
            pl.BlockSpec((1, CONV_WIDTH - 1, D_MODEL), lambda i, t: (i, 0, 0)),
        ],
        out_shape=[
            jax.ShapeDtypeStruct((b, s, D_MODEL), F32),
            jax.ShapeDtypeStruct((b, CONV_WIDTH - 1, D_MODEL), F32),
        ],
        scratch_shapes=[
            pltpu.VMEM((tile_m + HALO + SUBLANES, D_MODEL), F32),
            pltpu.VMEM((tile_m, D_MODEL), F32),
        ],
        compiler_params=pltpu.CompilerParams(
            dimension_semantics=("arbitrary", "arbitrary"), vmem_limit_bytes=VMEM_LIMIT),
        name="conv_prompt",
    )(x, lw["vec"], lw["dwb"], lw["pw1"], lw["pw2"])


def _conv_sample_kernel(x_ref, left_ref, vec_ref, dw_ref, pw1_ref, pw2_ref, o_ref, cs_ref, ext_ref, *, streams, steps):
    x = x_ref[...]
    ext_ref[:, 0:HALO - (CONV_WIDTH - 1), :] = jnp.zeros((streams, HALO - (CONV_WIDTH - 1), D_MODEL), F32)
    ext_ref[:, HALO - (CONV_WIDTH - 1):HALO, :] = left_ref[...]
    ext_ref[:, HALO:HALO + steps, :] = _glu(x, vec_ref, pw1_ref).reshape(streams, steps, D_MODEL)
    acc = jnp.broadcast_to(_row(vec_ref, V_DW_B).reshape(1, 1, D_MODEL), (streams, steps, D_MODEL))
    for k in range(CONV_WIDTH):
        lo = k + HALO - (CONV_WIDTH - 1)
        acc = acc + dw_ref[k:k + 1, :].reshape(1, 1, D_MODEL) * ext_ref[:, lo:lo + steps, :]
    act = _layernorm_silu(acc.reshape(streams * steps, D_MODEL), _row(vec_ref, V_LN_G), _row(vec_ref, V_LN_B))
    cs_ref[...] = ext_ref[:, HALO + steps - (CONV_WIDTH - 1):HALO + steps, :]
    o_ref[...] = x + _dot(act.astype(BF16), pw2_ref[...]) + _row(vec_ref, V_PW2_B)


def _conv_sample(x2d, left, lw, streams, steps):
    rows = streams * steps
    return pl.pallas_call(
        functools.partial(_conv_sample_kernel, streams=streams, steps=steps),
        grid=(1,),
        in_specs=[
            _const_spec((rows, D_MODEL)),
            _const_spec((streams, CONV_WIDTH - 1, D_MODEL)),
            _const_spec((VEC_ROWS, D_MODEL)),
            _const_spec((CONV_WIDTH + 1, D_MODEL)),
            _const_spec((D_MODEL, 2 * D_MODEL)),
            _const_spec((D_MODEL, D_MODEL)),
        ],
        out_specs=[
            pl.BlockSpec((rows, D_MODEL), lambda t: (0, 0)),
            pl.BlockSpec((streams, CONV_WIDTH - 1, D_MODEL), lambda t: (0, 0, 0)),
        ],
        out_shape=[
            jax.ShapeDtypeStruct((rows, D_MODEL), F32),
            jax.ShapeDtypeStruct((streams, CONV_WIDTH - 1, D_MODEL), F32),
        ],
        scratch_shapes=[pltpu.VMEM((streams, HALO + steps, D_MODEL), F32)],
        compiler_params=pltpu.CompilerParams(
            dimension_semantics=("arbitrary",), vmem_limit_bytes=VMEM_LIMIT),
        name="conv_sample",
    )(x2d, left, lw["vec"], lw["dw"], lw["pw1"], lw["pw2"])


def _attn_prompt_kernel(x_ref, vec_ref, bias_ref, wq_ref, wkv_ref, wo_ref, o_ref, ks_ref, vs_ref,
                        q_ref, kb_ref, vb_ref, att_ref, *, tile_m):
    t = pl.program_id(1)

    @pl.when(t == 0)
    def _():
        kb_ref[0:WINDOW, :] = jnp.zeros((WINDOW, KV_DIM), BF16)
        vb_ref[0:WINDOW, :] = jnp.zeros((WINDOW, KV_DIM), BF16)

    x = x_ref[0]
    h = _rms(x, _row(vec_ref, V_NORM_MIX)).astype(BF16)
    q_ref[...] = _dot(h, wq_ref[...]).astype(BF16)
    kv = _dot(h, wkv_ref[...])
    k_new = kv[:, :KV_DIM]
    v_new = kv[:, KV_DIM:]
    kb_ref[WINDOW:WINDOW + tile_m, :] = k_new.astype(BF16)
    vb_ref[WINDOW:WINDOW + tile_m, :] = v_new.astype(BF16)
    ks_ref[0] = k_new[tile_m - WINDOW:, :]
    vs_ref[0] = v_new[tile_m - WINDOW:, :]

    rows = GROUP * CHUNK

    def chunk(c):
        c0 = pl.multiple_of(c * CHUNK, CHUNK)
        q = jnp.concatenate(
            [q_ref[pl.ds(c0, CHUNK), g * KV_DIM:(g + 1) * KV_DIM] for g in range(GROUP)], axis=0)
        qm = jnp.concatenate(
            [jnp.where(_kv_lane_mask(q.shape, kh), q, jnp.zeros_like(q)) for kh in range(N_KV_HEADS)], axis=0)
        pad = jnp.zeros((KEYS_EXT - BAND, KV_DIM), BF16)
        kb = jnp.concatenate([kb_ref[pl.ds(c0, BAND), :], pad], axis=0)
        vb = vb_ref[pl.ds(c0, BAND), :]
        logits = lax.dot_general(qm, kb, (((1,), (1,)), ((), ())), preferred_element_type=F32)
        logits = logits + bias_ref[...]
        key_pos = t * tile_m + c0 - WINDOW + lax.broadcasted_iota(jnp.int32, (1, KEYS_EXT), 1)
        logits = jnp.where(key_pos >= 0, logits, NEG_INF)
        e = jnp.exp(logits - jnp.max(logits, axis=-1, keepdims=True)).astype(BF16)
        half = LANES // HEAD_DIM
        krow = lax.broadcasted_iota(jnp.int32, (KEYS_EXT, LANES), 0)
        klane = lax.broadcasted_iota(jnp.int32, (KEYS_EXT, LANES), 1)
        outs = []
        for pair in range(N_KV_HEADS // half):
            num = den = None
            for sub in range(half):
                kh = pair * half + sub
                in_head = (klane >= sub * HEAD_DIM) & (klane < (sub + 1) * HEAD_DIM)
                v_kh = jnp.concatenate([vb[:, pair * LANES:(pair + 1) * LANES], pad[:, 0:LANES]], axis=0)
                v_kh = jnp.where(in_head, v_kh, jnp.zeros_like(v_kh))
                ones = jnp.where(in_head & (krow <= BAND), 1.0, 0.0).astype(BF16)
                pv = _dot(e[kh * rows:(kh + 1) * rows, :], jnp.concatenate([v_kh, ones], axis=1))
                num = pv[:, 0:LANES] if num is None else num + pv[:, 0:LANES]
                den = pv[:, LANES:] if den is None else den + pv[:, LANES:]
            outs.append(num / den)
        out = jnp.concatenate(outs, axis=1)
        for g in range(GROUP):
            att_ref[pl.ds(c0, CHUNK), g * KV_DIM:(g + 1) * KV_DIM] = out[g * CHUNK:(g + 1) * CHUNK, :].astype(BF16)

    def chunk_pair(i, carry):
        chunk(2 * i)
        chunk(2 * i + 1)
        return carry

    lax.fori_loop(0, tile_m // (2 * CHUNK), chunk_pair, 0)

    kb_ref[0:WINDOW, :] = kb_ref[tile_m:tile_m + WINDOW, :]
    vb_ref[0:WINDOW, :] = vb_ref[tile_m:tile_m + WINDOW, :]
    o_ref[0] = x + _dot(att_ref[...], wo_ref[...])


def _attn_prompt(x, lw, bias, tile_m):
    b, s, _ = x.shape
    return pl.pallas_call(
        functools.partial(_attn_prompt_kernel, tile_m=tile_m),
        grid=(b, s // tile_m),
        in_specs=[
            pl.BlockSpec((1, tile_m, D_MODEL), lambda i, t: (i, t, 0)),
            _const_spec((VEC_ROWS, D_MODEL)),
            _const_spec(bias.shape),
            _const_spec((D_MODEL, D_MODEL)),
            _const_spec((D_MODEL, 2 * KV_DIM)),
            _const_spec((D_MODEL, D_MODEL)),
        ],
        out_specs=[
            pl.BlockSpec((1, tile_m, D_MODEL), lambda i, t: (i, t, 0)),
            pl.BlockSpec((1, WINDOW, KV_DIM), lambda i, t: (i, 0, 0)),
            pl.BlockSpec((1, WINDOW, KV_DIM), lambda i, t: (i, 0, 0)),
        ],
        out_shape=[
            jax.ShapeDtypeStruct((b, s, D_MODEL), F32),
            jax.ShapeDtypeStruct((b, WINDOW, KV_DIM), F32),
            jax.ShapeDtypeStruct((b, WINDOW, KV_DIM), F32),
        ],
        scratch_shapes=[
            pltpu.VMEM((tile_m, D_MODEL), BF16),
            pltpu.VMEM((tile_m + WINDOW, KV_DIM), BF16),
            pltpu.VMEM((tile_m + WINDOW, KV_DIM), BF16),
            pltpu.VMEM((tile_m, D_MODEL), BF16),
        ],
        compiler_params=pltpu.CompilerParams(
            dimension_semantics=("arbitrary", "arbitrary"), vmem_limit_bytes=VMEM_LIMIT),
        name="attn_prompt",
    )(x, lw["vec"], bias, lw["wq"], lw["wkv"], lw["wo"])


def _attn_sample_kernel(x_ref, kc_ref, vc_ref, vec_ref, bias_ref, wq_ref, wkv_ref, wo_ref,
                        o_ref, ks_ref, vs_ref, kf_ref, vf_ref, *, streams, steps):
    cache = kc_ref.shape[1]
    keys = cache + steps
    x = x_ref[...]
    h = _rms(x, _row(vec_ref, V_NORM_MIX)).astype(BF16)
    q = _dot(h, wq_ref[...]).reshape(streams, steps, D_MODEL)
    kv = _dot(h, wkv_ref[...])
    kf_ref[:, 0:cache, :] = kc_ref[...]
    vf_ref[:, 0:cache, :] = vc_ref[...]
    kf_ref[:, cache:keys, :] = kv[:, :KV_DIM].reshape(streams, steps, KV_DIM)
    vf_ref[:, cache:keys, :] = kv[:, KV_DIM:].reshape(streams, steps, KV_DIM)
    ks_ref[...] = kf_ref[:, steps:keys, :]
    vs_ref[...] = vf_ref[:, steps:keys, :]

    qs = jnp.concatenate([q[:, :, g * KV_DIM:(g + 1) * KV_DIM] for g in range(GROUP)], axis=1).astype(BF16)
    kf = kf_ref[...].astype(BF16)
    vf = vf_ref[...].astype(BF16)
    out = jnp.zeros((streams, GROUP * steps, KV_DIM), F32)
    for kh in range(N_KV_HEADS):
        k_kh = jnp.where(_kv_lane_mask(kf.shape, kh), kf, jnp.zeros_like(kf))
        logits = jnp.einsum("bqd,bkd->bqk", qs, k_kh, preferred_element_type=F32)
        logits = logits + bias_ref[kh, :, 0:keys][None]
        probs = _sink_softmax(logits, bias_ref[kh, :, keys:keys + 1][None])
        o_kh = jnp.einsum("bqk,bkd->bqd", probs.astype(BF16), vf, preferred_element_type=F32)
        out = jnp.where(_kv_lane_mask(out.shape, kh), o_kh, out)
    att = jnp.concatenate([out[:, g * steps:(g + 1) * steps, :] for g in range(GROUP)], axis=2)
    o_ref[...] = x + _dot(att.reshape(streams * steps, D_MODEL).astype(BF16), wo_ref[...])


def _attn_sample(x2d, k_left, v_left, lw, bias, streams, steps):
    rows = streams * steps
    cache = k_left.shape[1]
    return pl.pallas_call(
        functools.partial(_attn_sample_kernel, streams=streams, steps=steps),
        grid=(1,),
        in_specs=[
            _const_spec((rows, D_MODEL)),
            _const_spec(k_left.shape),
            _const_spec(v_left.shape),
            _const_spec((VEC_ROWS, D_MODEL)),
            _const_spec(bias.shape),
            _const_spec((D_MODEL, D_MODEL)),
            _const_spec((D_MODEL, 2 * KV_DIM)),
            _const_spec((D_MODEL, D_MODEL)),
        ],
        out_specs=[
            pl.BlockSpec((rows, D_MODEL), lambda t: (0, 0)),
            pl.BlockSpec((streams, cache, KV_DIM), lambda t: (0, 0, 0)),
            pl.BlockSpec((streams, cache, KV_DIM), lambda t: (0, 0, 0)),
        ],
        out_shape=[
            jax.ShapeDtypeStruct((rows, D_MODEL), F32),
            jax.ShapeDtypeStruct((streams, cache, KV_DIM), F32),
            jax.ShapeDtypeStruct((streams, cache, KV_DIM), F32),
        ],
        scratch_shapes=[
            pltpu.VMEM((streams, cache + steps, KV_DIM), F32),
            pltpu.VMEM((streams, cache + steps, KV_DIM), F32),
        ],
        compiler_params=pltpu.CompilerParams(
            dimension_semantics=("arbitrary",), vmem_limit_bytes=VMEM_LIMIT),
        name="attn_sample",
    )(x2d, k_left, v_left, lw["vec"], bias, lw["wq"], lw["wkv"], lw["wo"])


def _t5_bucket(rel):
    nb = NUM_BUCKETS // 2
    max_exact = nb // 2
    ret = (rel > 0).astype(jnp.int32) * nb
    n = jnp.abs(rel)
    nf = jnp.maximum(n, 1).astype(F32)
    large = max_exact + (jnp.log(nf / max_exact) / math.log(MAX_DISTANCE / max_exact)
                         * (nb - max_exact)).astype(jnp.int32)
    large = jnp.minimum(large, nb - 1)
    return ret + jnp.where(n < max_exact, n, large)


def _bias_table(rel_bias, sinks, q_len, k_len, offset):
    rel = jnp.arange(k_len, dtype=jnp.int32)[None, :] - offset - jnp.arange(q_len, dtype=jnp.int32)[:, None]
    onehot = (_t5_bucket(rel)[:, :, None] == jnp.arange(NUM_BUCKETS, dtype=jnp.int32)).astype(F32)
    b = jnp.einsum("qkn,nh->hqk", onehot, rel_bias.astype(F32), precision=lax.Precision.HIGHEST)
    b = b.reshape(N_KV_HEADS, GROUP * q_len, k_len)
    s = jnp.repeat(sinks.astype(F32).reshape(N_KV_HEADS, GROUP), q_len, axis=1)[:, :, None]
    pad = jnp.full((N_KV_HEADS, GROUP * q_len, KEYS_EXT - k_len - 1), NEG_INF, F32)
    return jnp.concatenate([b, s, pad], axis=2)


def _pack_vec(rows):
    zero = jnp.zeros((D_MODEL,), F32)
    return jnp.stack([rows[i].astype(F32) if i in rows else zero for i in range(VEC_ROWS)])


def kernel(x_prompt, x_sample, cache_conv, cache_k, cache_v, p_prompt, p_sample, rel_bias, norm_mix, norm_mlp, norm_ple, norm_final, conv_pw1, conv_pw1_b, conv_dw, conv_dw_b, conv_ln_g, conv_ln_b, conv_pw2, conv_pw2_b, attn_wq, attn_wk, attn_wv, attn_wo, attn_sinks, mlp_w1, mlp_w2, ple_proj, ple_gate):
    batch, seq, _ = x_prompt.shape
    streams, steps, _ = x_sample.shape
    cache = cache_k.shape[2]
    assert seq % TILE_M == 0 and TILE_M % CHUNK == 0 and TILE_M >= WINDOW and steps % SUBLANES == 0

    layers = []
    for i in range(DEPTH):
        j = i // 2
        rows = {V_NORM_MIX: norm_mix[i], V_NORM_MLP: norm_mlp[i], V_NORM_PLE: norm_ple[i], V_NORM_FINAL: norm_final}
        lw = {"w1": mlp_w1[i].astype(BF16), "w2": mlp_w2[i].astype(BF16),
              "gate": ple_gate[i].astype(BF16), "proj": ple_proj[i].astype(BF16)}
        if i % 2 == 0:
            rows.update({V_PW1_B_LO: conv_pw1_b[j, :D_MODEL], V_PW1_B_HI: conv_pw1_b[j, D_MODEL:],
                         V_DW_B: conv_dw_b[j], V_LN_G: conv_ln_g[j], V_LN_B: conv_ln_b[j], V_PW2_B: conv_pw2_b[j]})
            lw["pw1"] = conv_pw1[j].astype(BF16)
            lw["pw2"] = conv_pw2[j].astype(BF16)
            lw["dw"] = jnp.concatenate([conv_dw[j].astype(F32), jnp.zeros((1, D_MODEL), F32)], axis=0)
            lw["dwb"] = jnp.broadcast_to(conv_dw[j].astype(F32)[:, None, :], (CONV_WIDTH, SUBLANES, D_MODEL))
        else:
            wq = (attn_wq[j] * (HEAD_DIM ** -0.5)).reshape(D_MODEL, N_KV_HEADS, GROUP, HEAD_DIM).transpose(0, 2, 1, 3)
            lw["wq"] = wq.reshape(D_MODEL, D_MODEL).astype(BF16)
            lw["wkv"] = jnp.concatenate([attn_wk[j], attn_wv[j]], axis=1).astype(BF16)
            wo = attn_wo[j].reshape(N_KV_HEADS, GROUP, HEAD_DIM, D_MODEL).transpose(1, 0, 2, 3)
            lw["wo"] = wo.reshape(D_MODEL, D_MODEL).astype(BF16)
            lw["bias_prompt"] = _bias_table(rel_bias, attn_sinks[j], CHUNK, BAND, WINDOW).reshape(N_HEADS * CHUNK, 256)
            lw["bias_sample"] = _bias_table(rel_bias, attn_sinks[j], steps, cache + steps, cache)
        lw["vec"] = _pack_vec(rows)
        layers.append(lw)

    p_prompt2 = p_prompt.reshape(DEPTH, batch * seq, PLE_DIM)
    p_sample2 = p_sample.reshape(DEPTH, streams * steps, PLE_DIM)

    xp = x_prompt
    xs = x_sample.reshape(streams * steps, D_MODEL)
    conv_p, k_p, v_p, conv_s, k_s, v_s = [], [], [], [], [], []
    for i, lw in enumerate(layers):
        j = i // 2
        final = i == DEPTH - 1
        if i % 2 == 0:
            xp, cs = _conv_prompt(xp, lw, TILE_M)
            conv_p.append(cs)
            xs, cs = _conv_sample(xs, cache_conv[j], lw, streams, steps)
            conv_s.append(cs)
        else:
            xp, ks, vs = _attn_prompt(xp, lw, lw["bias_prompt"], TILE_M)
            k_p.append(ks)
            v_p.append(vs)
            xs, ks, vs = _attn_sample(xs, cache_k[j].reshape(streams, cache, KV_DIM),
                                      cache_v[j].reshape(streams, cache, KV_DIM), lw, lw["bias_sample"], streams, steps)
            k_s.append(ks)
            v_s.append(vs)
        xp = _mlp_ple(xp.reshape(batch * seq, D_MODEL), p_prompt2, i, lw, MLP_TILE_M, final).reshape(batch, seq, D_MODEL)
        xs = _mlp_ple(xs, p_sample2, i, lw, streams * steps, final)

    def heads(ts, n):
        return jnp.stack(ts).reshape(len(ts), n, -1, N_KV_HEADS, HEAD_DIM)

    return (xp, xs.reshape(streams, steps, D_MODEL), jnp.stack(conv_p), heads(k_p, batch), heads(v_p, batch),
            jnp.stack(conv_s), heads(k_s, streams), heads(v_s, streams))
```

```python
import functools
import math

import jax
import jax.numpy as jnp
from jax import lax
from jax.experimental import pallas as pl
from jax.experimental.pallas import tpu as pltpu

D_MODEL = 1024
DEPTH = 4
CHUNK = 64
CONV_WIDTH = 31
WINDOW = 128
N_HEADS = 16
N_KV_HEADS = 4
GROUP = N_HEADS // N_KV_HEADS
HEAD_DIM = 64
KV_DIM = N_KV_HEADS * HEAD_DIM
D_FF = 4 * D_MODEL
PLE_DIM = 256
NUM_BUCKETS = 32
MAX_DISTANCE = 128
EPS = 1e-6
NEG_INF = -1e30

BAND = WINDOW + CHUNK
KEYS_EXT = 256
HALO = 32
SUBLANES = 8
LANES = 128
CONV_ROWS = 64
FF_CHUNK = 1024
TILE_M = 512
MLP_TILE_M = 1024
VMEM_LIMIT = 60 * 1024 * 1024

V_NORM_MIX, V_PW1_B_LO, V_PW1_B_HI, V_DW_B, V_LN_G, V_LN_B, V_PW2_B, V_NORM_MLP, V_NORM_PLE, V_NORM_FINAL = range(10)
VEC_ROWS = 16

F32 = jnp.float32
BF16 = jnp.bfloat16


def _dot(a, b):
    return jnp.dot(a, b, preferred_element_type=F32)


def _sigmoid(x):
    return 1.0 / (1.0 + jnp.exp(-x))


def _rms(x, g):
    return x * lax.rsqrt(jnp.mean(x * x, axis=-1, keepdims=True) + EPS) * g


def _row(vec_ref, i):
    return vec_ref[i:i + 1, :]


def _layernorm_silu(c, g, b):
    mu = jnp.mean(c, axis=-1, keepdims=True)
    d = c - mu
    var = jnp.mean(d * d, axis=-1, keepdims=True)
    y = d * lax.rsqrt(var + EPS) * g + b
    return y * _sigmoid(y)


def _sink_softmax(logits, sink):
    m = jnp.maximum(jnp.max(logits, axis=-1, keepdims=True), sink)
    e = jnp.exp(logits - m)
    return e / (jnp.sum(e, axis=-1, keepdims=True) + jnp.exp(sink - m))


def _kv_lane_mask(shape, kh):
    lane = lax.broadcasted_iota(jnp.int32, shape, len(shape) - 1)
    return (lane >= kh * HEAD_DIM) & (lane < (kh + 1) * HEAD_DIM)


def _mlp_ple_rows(x, p, vec_ref, w1_ref, w2_ref, gate_ref, proj_ref, final):
    h = _rms(x, _row(vec_ref, V_NORM_MLP)).astype(BF16)
    for c in range(D_FF // FF_CHUNK):
        hid = _dot(h, w1_ref[:, c * FF_CHUNK:(c + 1) * FF_CHUNK])
        hid = jnp.square(jnp.maximum(hid, 0.0)).astype(BF16)
        x = x + _dot(hid, w2_ref[c * FF_CHUNK:(c + 1) * FF_CHUNK, :])
    hg = _rms(x, _row(vec_ref, V_NORM_PLE)).astype(BF16)
    gate = _sigmoid(_dot(hg, gate_ref[...]))
    x = x + _dot(p.astype(BF16), proj_ref[...]) * gate
    if final:
        x = _rms(x, _row(vec_ref, V_NORM_FINAL))
    return x


def _mlp_ple_kernel(x_ref, p_ref, vec_ref, w1_ref, w2_ref, gate_ref, proj_ref, o_ref, *, final):
    o_ref[...] = _mlp_ple_rows(x_ref[...], p_ref[...], vec_ref, w1_ref, w2_ref, gate_ref, proj_ref, final)


def _const_spec(shape):
    nd = len(shape)
    return pl.BlockSpec(shape, lambda *_: (0,) * nd, pipeline_mode=pl.Buffered(1))


def _mlp_ple(x2d, p4d, layer, lw, tile_m, final):
    rows = x2d.shape[0]
    return pl.pallas_call(
        functools.partial(_mlp_ple_kernel, final=final),
        grid=(rows // tile_m,),
        in_specs=[
            pl.BlockSpec((tile_m, D_MODEL), lambda t: (t, 0)),
            pl.BlockSpec((None, tile_m, PLE_DIM), lambda t: (layer, t, 0)),
            _const_spec((VEC_ROWS, D_MODEL)),
            _const_spec((D_MODEL, D_FF)),
            _const_spec((D_FF, D_MODEL)),
            _const_spec((D_MODEL, D_MODEL)),
            _const_spec((PLE_DIM, D_MODEL)),
        ],
        out_specs=pl.BlockSpec((tile_m, D_MODEL), lambda t: (t, 0)),
        out_shape=jax.ShapeDtypeStruct((rows, D_MODEL), F32),
        compiler_params=pltpu.CompilerParams(
            dimension_semantics=("arbitrary",), vmem_limit_bytes=VMEM_LIMIT),
        name=f"mlp_ple_l{layer}_r{rows}",
    )(x2d, p4d, lw["vec"], lw["w1"], lw["w2"], lw["gate"], lw["proj"])


def _glu(x, vec_ref, pw1_ref):
    h = _rms(x, _row(vec_ref, V_NORM_MIX)).astype(BF16)
    a = _dot(h, pw1_ref[:, :D_MODEL]) + _row(vec_ref, V_PW1_B_LO)
    g = _dot(h, pw1_ref[:, D_MODEL:]) + _row(vec_ref, V_PW1_B_HI)
    return a * _sigmoid(g)


def _depthwise_block(ext_ref, dwb_ref, c_ref, r0, row_slice):
    first = HALO - (CONV_WIDTH - 1)
    groups = CONV_ROWS // SUBLANES
    spans = -(-(first + CONV_WIDTH) // SUBLANES)
    for l in range(D_MODEL // LANES):
        lanes = slice(l * LANES, (l + 1) * LANES)
        win = [ext_ref[row_slice(r0 + SUBLANES * g, SUBLANES), lanes] for g in range(groups + spans)]
        out = None
        for s in range(SUBLANES):
            taps = [(j, SUBLANES * j + s - first) for j in range(spans)
                    if 0 <= SUBLANES * j + s - first < CONV_WIDTH]
            phase = []
            for g in range(groups + (1 if s else 0)):
                p = None
                for j, k in taps:
                    term = dwb_ref[k, :, lanes] * win[g + j]
                    p = term if p is None else p + term
                phase.append(p)
            shifted = jnp.concatenate(phase, axis=0)[s:s + CONV_ROWS, :]
            out = shifted if out is None else out + shifted
        c_ref[row_slice(r0, CONV_ROWS), lanes] = out


def _conv_mlp_prompt_kernel(x_ref, p_ref, vec_ref, dwb_ref, pw1_ref, pw2_ref, w1_ref, w2_ref, gate_ref, proj_ref,
                            o_ref, cs_ref, ext_ref, c_ref, mid_ref, *, tile_m, tiles_per_stream):
    t = pl.program_id(0)
    ext_rows = tile_m + HALO
    slot = t % 2

    @pl.when(t % tiles_per_stream == 0)
    def _():
        ext_ref[0:HALO, :] = jnp.zeros((HALO, D_MODEL), F32)
        ext_ref[ext_rows:ext_rows + SUBLANES, :] = jnp.zeros((SUBLANES, D_MODEL), F32)

    @pl.when(t == 0)
    def _():
        mid_ref[1] = jnp.zeros((tile_m, D_MODEL), F32)

    o_ref[...] = _mlp_ple_rows(mid_ref[1 - slot], p_ref[...], vec_ref, w1_ref, w2_ref, gate_ref, proj_ref, False)

    x = x_ref[...]
    ext_ref[HALO:ext_rows, :] = _glu(x, vec_ref, pw1_ref)
    for r in range(tile_m // CONV_ROWS):
        _depthwise_block(ext_ref, dwb_ref, c_ref, r * CONV_ROWS, lambda start, size: slice(start, start + size))
    cs_ref[0] = ext_ref[ext_rows - (CONV_WIDTH - 1):ext_rows, :]
    ext_ref[0:HALO, :] = ext_ref[tile_m:ext_rows, :]
    act = _layernorm_silu(c_ref[...] + _row(vec_ref, V_DW_B), _row(vec_ref, V_LN_G), _row(vec_ref, V_LN_B))
    mid_ref[slot] = x + _dot(act.astype(BF16), pw2_ref[...]) + _row(vec_ref, V_PW2_B)


def _conv_mlp_prompt(x, p3d, layer, lw, tile_m):
    b, s, _ = x.shape
    tiles_per_stream = s // tile_m
    n_tiles = b * tiles_per_stream
    mixer_tile = lambda t: jnp.minimum(t, n_tiles - 1)
    mlp_tile = lambda t: jnp.maximum(t - 1, 0)
    y, cs = pl.pallas_call(
        functools.partial(_conv_mlp_prompt_kernel, tile_m=tile_m, tiles_per_stream=tiles_per_stream),
        grid=(n_tiles + 1,),
        in_specs=[
            pl.BlockSpec((tile_m, D_MODEL), lambda t: (mixer_tile(t), 0)),
            pl.BlockSpec((None, tile_m, PLE_DIM), lambda t: (layer, mlp_tile(t), 0)),
            _const_spec((VEC_ROWS, D_MODEL)),
            _const_spec((CONV_WIDTH, SUBLANES, D_MODEL)),
            _const_spec((D_MODEL, 2 * D_MODEL)),
            _const_spec((D_MODEL, D_MODEL)),
            _const_spec((D_MODEL, D_FF)),
            _const_spec((D_FF, D_MODEL)),
            _const_spec((D_MODEL, D_MODEL)),
            _const_spec((PLE_DIM, D_MODEL)),
        ],
        out_specs=[
            pl.BlockSpec((tile_m, D_MODEL), lambda t: (mlp_tile(t), 0)),
            pl.BlockSpec((1, CONV_WIDTH - 1, D_MODEL), lambda t: (mixer_tile(t) // tiles_per_stream, 0, 0)),
        ],
        out_shape=[
            jax.ShapeDtypeStruct((b * s, D_MODEL), F32),
            jax.ShapeDtypeStruct((b, CONV_WIDTH - 1, D_MODEL), F32),
        ],
        scratch_shapes=[
            pltpu.VMEM((tile_m + HALO + SUBLANES, D_MODEL), F32),
            pltpu.VMEM((tile_m, D_MODEL), F32),
            pltpu.VMEM((2, tile_m, D_MODEL), F32),
        ],
        compiler_params=pltpu.CompilerParams(
            dimension_semantics=("arbitrary",), vmem_limit_bytes=VMEM_LIMIT),
        name="conv_mlp_prompt",
    )(x.reshape(b * s, D_MODEL), p3d, lw["vec"], lw["dwb"], lw["pw1"], lw["pw2"],
      lw["w1"], lw["w2"], lw["gate"], lw["proj"])
    return y.reshape(b, s, D_MODEL), cs


def _conv_sample_kernel(x_ref, left_ref, vec_ref, dw_ref, pw1_ref, pw2_ref, o_ref, cs_ref, ext_ref, *, streams, steps):
    x = x_ref[...]
    ext_ref[:, 0:HALO - (CONV_WIDTH - 1), :] = jnp.zeros((streams, HALO - (CONV_WIDTH - 1), D_MODEL), F32)
    ext_ref[:, HALO - (CONV_WIDTH - 1):HALO, :] = left_ref[...]
    ext_ref[:, HALO:HALO + steps, :] = _glu(x, vec_ref, pw1_ref).reshape(streams, steps, D_MODEL)
    acc = jnp.broadcast_to(_row(vec_ref, V_DW_B).reshape(1, 1, D_MODEL), (streams, steps, D_MODEL))
    for k in range(CONV_WIDTH):
        lo = k + HALO - (CONV_WIDTH - 1)
        acc = acc + dw_ref[k:k + 1, :].reshape(1, 1, D_MODEL) * ext_ref[:, lo:lo + steps, :]
    act = _layernorm_silu(acc.reshape(streams * steps, D_MODEL), _row(vec_ref, V_LN_G), _row(vec_ref, V_LN_B))
    cs_ref[...] = ext_ref[:, HALO + steps - (CONV_WIDTH - 1):HALO + steps, :]
    o_ref[...] = x + _dot(act.astype(BF16), pw2_ref[...]) + _row(vec_ref, V_PW2_B)


def _conv_sample(x2d, left, lw, streams, steps):
    rows = streams * steps
    return pl.pallas_call(
        functools.partial(_conv_sample_kernel, streams=streams, steps=steps),
        grid=(1,),
        in_specs=[
            _const_spec((rows, D_MODEL)),
            _const_spec((streams, CONV_WIDTH - 1, D_MODEL)),
            _const_spec((VEC_ROWS, D_MODEL)),
            _const_spec((CONV_WIDTH + 1, D_MODEL)),
            _const_spec((D_MODEL, 2 * D_MODEL)),
            _const_spec((D_MODEL, D_MODEL)),
        ],
        out_specs=[
            pl.BlockSpec((rows, D_MODEL), lambda t: (0, 0)),
            pl.BlockSpec((streams, CONV_WIDTH - 1, D_MODEL), lambda t: (0, 0, 0)),
        ],
        out_shape=[
            jax.ShapeDtypeStruct((rows, D_MODEL), F32),
            jax.ShapeDtypeStruct((streams, CONV_WIDTH - 1, D_MODEL), F32),
        ],
        scratch_shapes=[pltpu.VMEM((streams, HALO + steps, D_MODEL), F32)],
        compiler_params=pltpu.CompilerParams(
            dimension_semantics=("arbitrary",), vmem_limit_bytes=VMEM_LIMIT),
        name="conv_sample",
    )(x2d, left, lw["vec"], lw["dw"], lw["pw1"], lw["pw2"])


def _attn_prompt_kernel(x_ref, vec_ref, bias_ref, wq_ref, wkv_ref, wo_ref, o_ref, ks_ref, vs_ref,
                        q_ref, kb_ref, vb_ref, att_ref, *, tile_m):
    t = pl.program_id(1)

    @pl.when(t == 0)
    def _():
        kb_ref[0:WINDOW, :] = jnp.zeros((WINDOW, KV_DIM), BF16)
        vb_ref[0:WINDOW, :] = jnp.zeros((WINDOW, KV_DIM), BF16)

    x = x_ref[0]
    h = _rms(x, _row(vec_ref, V_NORM_MIX)).astype(BF16)
    q_ref[...] = _dot(h, wq_ref[...]).astype(BF16)
    kv = _dot(h, wkv_ref[...])
    k_new = kv[:, :KV_DIM]
    v_new = kv[:, KV_DIM:]
    kb_ref[WINDOW:WINDOW + tile_m, :] = k_new.astype(BF16)
    vb_ref[WINDOW:WINDOW + tile_m, :] = v_new.astype(BF16)
    ks_ref[0] = k_new[tile_m - WINDOW:, :]
    vs_ref[0] = v_new[tile_m - WINDOW:, :]

    rows = GROUP * CHUNK

    def chunk(c):
        c0 = pl.multiple_of(c * CHUNK, CHUNK)
        q = jnp.concatenate(
            [q_ref[pl.ds(c0, CHUNK), g * KV_DIM:(g + 1) * KV_DIM] for g in range(GROUP)], axis=0)
        qm = jnp.concatenate(
            [jnp.where(_kv_lane_mask(q.shape, kh), q, jnp.zeros_like(q)) for kh in range(N_KV_HEADS)], axis=0)
        pad = jnp.zeros((KEYS_EXT - BAND, KV_DIM), BF16)
        kb = jnp.concatenate([kb_ref[pl.ds(c0, BAND), :], pad], axis=0)
        vb = vb_ref[pl.ds(c0, BAND), :]
        logits = lax.dot_general(qm, kb, (((1,), (1,)), ((), ())), preferred_element_type=F32)
        logits = logits + bias_ref[...]
        key_pos = t * tile_m + c0 - WINDOW + lax.broadcasted_iota(jnp.int32, (1, KEYS_EXT), 1)
        logits = jnp.where(key_pos >= 0, logits, NEG_INF)
        e = jnp.exp(logits - jnp.max(logits, axis=-1, keepdims=True)).astype(BF16)
        half = LANES // HEAD_DIM
        krow = lax.broadcasted_iota(jnp.int32, (KEYS_EXT, LANES), 0)
        klane = lax.broadcasted_iota(jnp.int32, (KEYS_EXT, LANES), 1)
        outs = []
        for pair in range(N_KV_HEADS // half):
            num = den = None
            for sub in range(half):
                kh = pair * half + sub
                in_head = (klane >= sub * HEAD_DIM) & (klane < (sub + 1) * HEAD_DIM)
                v_kh = jnp.concatenate([vb[:, pair * LANES:(pair + 1) * LANES], pad[:, 0:LANES]], axis=0)
                v_kh = jnp.where(in_head, v_kh, jnp.zeros_like(v_kh))
                ones = jnp.where(in_head & (krow <= BAND), 1.0, 0.0).astype(BF16)
                pv = _dot(e[kh * rows:(kh + 1) * rows, :], jnp.concatenate([v_kh, ones], axis=1))
                num = pv[:, 0:LANES] if num is None else num + pv[:, 0:LANES]
                den = pv[:, LANES:] if den is None else den + pv[:, LANES:]
            outs.append(num / den)
        out = jnp.concatenate(outs, axis=1)
        for g in range(GROUP):
            att_ref[pl.ds(c0, CHUNK), g * KV_DIM:(g + 1) * KV_DIM] = out[g * CHUNK:(g + 1) * CHUNK, :].astype(BF16)

    def chunk_pair(i, carry):
        chunk(2 * i)
        chunk(2 * i + 1)
        return carry

    lax.fori_loop(0, tile_m // (2 * CHUNK), chunk_pair, 0)

    kb_ref[0:WINDOW, :] = kb_ref[tile_m:tile_m + WINDOW, :]
    vb_ref[0:WINDOW, :] = vb_ref[tile_m:tile_m + WINDOW, :]
    o_ref[0] = x + _dot(att_ref[...], wo_ref[...])


def _attn_prompt(x, lw, bias, tile_m):
    b, s, _ = x.shape
    return pl.pallas_call(
        functools.partial(_attn_prompt_kernel, tile_m=tile_m),
        grid=(b, s // tile_m),
        in_specs=[
            pl.BlockSpec((1, tile_m, D_MODEL), lambda i, t: (i, t, 0)),
            _const_spec((VEC_ROWS, D_MODEL)),
            _const_spec(bias.shape),
            _const_spec((D_MODEL, D_MODEL)),
            _const_spec((D_MODEL, 2 * KV_DIM)),
            _const_spec((D_MODEL, D_MODEL)),
        ],
        out_specs=[
            pl.BlockSpec((1, tile_m, D_MODEL), lambda i, t: (i, t, 0)),
            pl.BlockSpec((1, WINDOW, KV_DIM), lambda i, t: (i, 0, 0)),
            pl.BlockSpec((1, WINDOW, KV_DIM), lambda i, t: (i, 0, 0)),
        ],
        out_shape=[
            jax.ShapeDtypeStruct((b, s, D_MODEL), F32),
            jax.ShapeDtypeStruct((b, WINDOW, KV_DIM), F32),
            jax.ShapeDtypeStruct((b, WINDOW, KV_DIM), F32),
        ],
        scratch_shapes=[
            pltpu.VMEM((tile_m, D_MODEL), BF16),
            pltpu.VMEM((tile_m + WINDOW, KV_DIM), BF16),
            pltpu.VMEM((tile_m + WINDOW, KV_DIM), BF16),
            pltpu.VMEM((tile_m, D_MODEL), BF16),
        ],
        compiler_params=pltpu.CompilerParams(
            dimension_semantics=("arbitrary", "arbitrary"), vmem_limit_bytes=VMEM_LIMIT),
        name="attn_prompt",
    )(x, lw["vec"], bias, lw["wq"], lw["wkv"], lw["wo"])


def _attn_sample_kernel(x_ref, kc_ref, vc_ref, vec_ref, bias_ref, wq_ref, wkv_ref, wo_ref,
                        o_ref, ks_ref, vs_ref, kf_ref, vf_ref, *, streams, steps):
    cache = kc_ref.shape[1]
    keys = cache + steps
    x = x_ref[...]
    h = _rms(x, _row(vec_ref, V_NORM_MIX)).astype(BF16)
    q = _dot(h, wq_ref[...]).reshape(streams, steps, D_MODEL)
    kv = _dot(h, wkv_ref[...])
    kf_ref[:, 0:cache, :] = kc_ref[...]
    vf_ref[:, 0:cache, :] = vc_ref[...]
    kf_ref[:, cache:keys, :] = kv[:, :KV_DIM].reshape(streams, steps, KV_DIM)
    vf_ref[:, cache:keys, :] = kv[:, KV_DIM:].reshape(streams, steps, KV_DIM)
    ks_ref[...] = kf_ref[:, steps:keys, :]
    vs_ref[...] = vf_ref[:, steps:keys, :]

    qs = jnp.concatenate([q[:, :, g * KV_DIM:(g + 1) * KV_DIM] for g in range(GROUP)], axis=1).astype(BF16)
    kf = kf_ref[...].astype(BF16)
    vf = vf_ref[...].astype(BF16)
    out = jnp.zeros((streams, GROUP * steps, KV_DIM), F32)
    for kh in range(N_KV_HEADS):
        k_kh = jnp.where(_kv_lane_mask(kf.shape, kh), kf, jnp.zeros_like(kf))
        logits = jnp.einsum("bqd,bkd->bqk", qs, k_kh, preferred_element_type=F32)
        logits = logits + bias_ref[kh, :, 0:keys][None]
        probs = _sink_softmax(logits, bias_ref[kh, :, keys:keys + 1][None])
        o_kh = jnp.einsum("bqk,bkd->bqd", probs.astype(BF16), vf, preferred_element_type=F32)
        out = jnp.where(_kv_lane_mask(out.shape, kh), o_kh, out)
    att = jnp.concatenate([out[:, g * steps:(g + 1) * steps, :] for g in range(GROUP)], axis=2)
    o_ref[...] = x + _dot(att.reshape(streams * steps, D_MODEL).astype(BF16), wo_ref[...])


def _attn_sample(x2d, k_left, v_left, lw, bias, streams, steps):
    rows = streams * steps
    cache = k_left.shape[1]
    return pl.pallas_call(
        functools.partial(_attn_sample_kernel, streams=streams, steps=steps),
        grid=(1,),
        in_specs=[
            _const_spec((rows, D_MODEL)),
            _const_spec(k_left.shape),
            _const_spec(v_left.shape),
            _const_spec((VEC_ROWS, D_MODEL)),
            _const_spec(bias.shape),
            _const_spec((D_MODEL, D_MODEL)),
            _const_spec((D_MODEL, 2 * KV_DIM)),
            _const_spec((D_MODEL, D_MODEL)),
        ],
        out_specs=[
            pl.BlockSpec((rows, D_MODEL), lambda t: (0, 0)),
            pl.BlockSpec((streams, cache, KV_DIM), lambda t: (0, 0, 0)),
            pl.BlockSpec((streams, cache, KV_DIM), lambda t: (0, 0, 0)),
        ],
        out_shape=[
            jax.ShapeDtypeStruct((rows, D_MODEL), F32),
            jax.ShapeDtypeStruct((streams, cache, KV_DIM), F32),
            jax.ShapeDtypeStruct((streams, cache, KV_DIM), F32),
        ],
        scratch_shapes=[
            pltpu.VMEM((streams, cache + steps, KV_DIM), F32),
            pltpu.VMEM((streams, cache + steps, KV_DIM), F32),
        ],
        compiler_params=pltpu.CompilerParams(
            dimension_semantics=("arbitrary",), vmem_limit_bytes=VMEM_LIMIT),
        name="attn_sample",
    )(x2d, k_left, v_left, lw["vec"], bias, lw["wq"], lw["wkv"], lw["wo"])


def _t5_bucket(rel):
    nb = NUM_BUCKETS // 2
    max_exact = nb // 2
    ret = (rel > 0).astype(jnp.int32) * nb
    n = jnp.abs(rel)
    nf = jnp.maximum(n, 1).astype(F32)
    large = max_exact + (jnp.log(nf / max_exact) / math.log(MAX_DISTANCE / max_exact)
                         * (nb - max_exact)).astype(jnp.int32)
    large = jnp.minimum(large, nb - 1)
    return ret + jnp.where(n < max_exact, n, large)


def _bias_table(rel_bias, sinks, q_len, k_len, offset):
    rel = jnp.arange(k_len, dtype=jnp.int32)[None, :] - offset - jnp.arange(q_len, dtype=jnp.int32)[:, None]
    onehot = (_t5_bucket(rel)[:, :, None] == jnp.arange(NUM_BUCKETS, dtype=jnp.int32)).astype(F32)
    b = jnp.einsum("qkn,nh->hqk", onehot, rel_bias.astype(F32), precision=lax.Precision.HIGHEST)
    b = b.reshape(N_KV_HEADS, GROUP * q_len, k_len)
    s = jnp.repeat(sinks.astype(F32).reshape(N_KV_HEADS, GROUP), q_len, axis=1)[:, :, None]
    pad = jnp.full((N_KV_HEADS, GROUP * q_len, KEYS_EXT - k_len - 1), NEG_INF, F32)
    return jnp.concatenate([b, s, pad], axis=2)


def _pack_vec(rows):
    zero = jnp.zeros((D_MODEL,), F32)
    return jnp.stack([rows[i].astype(F32) if i in rows else zero for i in range(VEC_ROWS)])


def kernel(x_prompt, x_sample, cache_conv, cache_k, cache_v, p_prompt, p_sample, rel_bias, norm_mix, norm_mlp, norm_ple, norm_final, conv_pw1, conv_pw1_b, conv_dw, conv_dw_b, conv_ln_g, conv_ln_b, conv_pw2, conv_pw2_b, attn_wq, attn_wk, attn_wv, attn_wo, attn_sinks, mlp_w1, mlp_w2, ple_proj, ple_gate):
    batch, seq, _ = x_prompt.shape
    streams, steps, _ = x_sample.shape
    cache = cache_k.shape[2]
    assert seq % TILE_M == 0 and TILE_M % CHUNK == 0 and TILE_M >= WINDOW and steps % SUBLANES == 0

    layers = []
    for i in range(DEPTH):
        j = i // 2
        rows = {V_NORM_MIX: norm_mix[i], V_NORM_MLP: norm_mlp[i], V_NORM_PLE: norm_ple[i], V_NORM_FINAL: norm_final}
        lw = {"w1": mlp_w1[i].astype(BF16), "w2": mlp_w2[i].astype(BF16),
              "gate": ple_gate[i].astype(BF16), "proj": ple_proj[i].astype(BF16)}
        if i % 2 == 0:
            rows.update({V_PW1_B_LO: conv_pw1_b[j, :D_MODEL], V_PW1_B_HI: conv_pw1_b[j, D_MODEL:],
                         V_DW_B: conv_dw_b[j], V_LN_G: conv_ln_g[j], V_LN_B: conv_ln_b[j], V_PW2_B: conv_pw2_b[j]})
            lw["pw1"] = conv_pw1[j].astype(BF16)
            lw["pw2"] = conv_pw2[j].astype(BF16)
            lw["dw"] = jnp.concatenate([conv_dw[j].astype(F32), jnp.zeros((1, D_MODEL), F32)], axis=0)
            lw["dwb"] = jnp.broadcast_to(conv_dw[j].astype(F32)[:, None, :], (CONV_WIDTH, SUBLANES, D_MODEL))
        else:
            wq = (attn_wq[j] * (HEAD_DIM ** -0.5)).reshape(D_MODEL, N_KV_HEADS, GROUP, HEAD_DIM).transpose(0, 2, 1, 3)
            lw["wq"] = wq.reshape(D_MODEL, D_MODEL).astype(BF16)
            lw["wkv"] = jnp.concatenate([attn_wk[j], attn_wv[j]], axis=1).astype(BF16)
            wo = attn_wo[j].reshape(N_KV_HEADS, GROUP, HEAD_DIM, D_MODEL).transpose(1, 0, 2, 3)
            lw["wo"] = wo.reshape(D_MODEL, D_MODEL).astype(BF16)
            lw["bias_prompt"] = _bias_table(rel_bias, attn_sinks[j], CHUNK, BAND, WINDOW).reshape(N_HEADS * CHUNK, 256)
            lw["bias_sample"] = _bias_table(rel_bias, attn_sinks[j], steps, cache + steps, cache)
        lw["vec"] = _pack_vec(rows)
        layers.append(lw)

    p_prompt2 = p_prompt.reshape(DEPTH, batch * seq, PLE_DIM)
    p_sample2 = p_sample.reshape(DEPTH, streams * steps, PLE_DIM)

    xp = x_prompt
    xs = x_sample.reshape(streams * steps, D_MODEL)
    conv_p, k_p, v_p, conv_s, k_s, v_s = [], [], [], [], [], []
    for i, lw in enumerate(layers):
        j = i // 2
        final = i == DEPTH - 1
        if i % 2 == 0:
            xp, cs = _conv_mlp_prompt(xp, p_prompt2, i, lw, TILE_M)
            conv_p.append(cs)
            xs, cs = _conv_sample(xs, cache_conv[j], lw, streams, steps)
            conv_s.append(cs)
        else:
            xp, ks, vs = _attn_prompt(xp, lw, lw["bias_prompt"], TILE_M)
            k_p.append(ks)
            v_p.append(vs)
            xs, ks, vs = _attn_sample(xs, cache_k[j].reshape(streams, cache, KV_DIM),
                                      cache_v[j].reshape(streams, cache, KV_DIM), lw, lw["bias_sample"], streams, steps)
            k_s.append(ks)
            v_s.append(vs)
            xp = _mlp_ple(xp.reshape(batch * seq, D_MODEL), p_prompt2, i, lw, MLP_TILE_M, final).reshape(batch, seq, D_MODEL)
        xs = _mlp_ple(xs, p_sample2, i, lw, streams * steps, final)

    def heads(ts, n):
        return jnp.stack(ts).reshape(len(ts), n, -1, N_KV_HEADS, HEAD_DIM)

    return (xp, xs.reshape(streams, steps, D_MODEL), jnp.stack(conv_p), heads(k_p, batch), heads(v_p, batch),
            jnp.stack(conv_s), heads(k_s, streams), heads(v_s, streams))
```

```python
import functools
import math

import jax
import jax.numpy as jnp
from jax import lax
from jax.experimental import pallas as pl
from jax.experimental.pallas import tpu as pltpu

D_MODEL = 1024
DEPTH = 4
CHUNK = 64
CONV_WIDTH = 31
WINDOW = 128
N_HEADS = 16
N_KV_HEADS = 4
GROUP = N_HEADS // N_KV_HEADS
HEAD_DIM = 64
KV_DIM = N_KV_HEADS * HEAD_DIM
D_FF = 4 * D_MODEL
PLE_DIM = 256
NUM_BUCKETS = 32
MAX_DISTANCE = 128
EPS = 1e-6
NEG_INF = -1e30

BAND = WINDOW + CHUNK
KEYS_EXT = 256
HALO = 32
SUBLANES = 8
LANES = 128
CONV_ROWS = 64
FF_CHUNK = 1024
MXU_COLS = 256
FILL_PER_PIECE = 2
TILE_M = 512
MLP_TILE_M = 1024
VMEM_LIMIT = 60 * 1024 * 1024

V_NORM_MIX, V_PW1_B_LO, V_PW1_B_HI, V_DW_B, V_LN_G, V_LN_B, V_PW2_B, V_NORM_MLP, V_NORM_PLE, V_NORM_FINAL = range(10)
VEC_ROWS = 16

F32 = jnp.float32
BF16 = jnp.bfloat16


def _dot(a, b):
    return jnp.dot(a, b, preferred_element_type=F32)


def _sigmoid(x):
    return 1.0 / (1.0 + jnp.exp(-x))


def _rms(x, g):
    return x * lax.rsqrt(jnp.mean(x * x, axis=-1, keepdims=True) + EPS) * g


def _row(vec_ref, i):
    return vec_ref[i:i + 1, :]


def _layernorm_silu(c, g, b):
    mu = jnp.mean(c, axis=-1, keepdims=True)
    d = c - mu
    var = jnp.mean(d * d, axis=-1, keepdims=True)
    y = d * lax.rsqrt(var + EPS) * g + b
    return y * _sigmoid(y)


def _sink_softmax(logits, sink):
    m = jnp.maximum(jnp.max(logits, axis=-1, keepdims=True), sink)
    e = jnp.exp(logits - m)
    return e / (jnp.sum(e, axis=-1, keepdims=True) + jnp.exp(sink - m))


def _kv_lane_mask(shape, kh):
    lane = lax.broadcasted_iota(jnp.int32, shape, len(shape) - 1)
    return (lane >= kh * HEAD_DIM) & (lane < (kh + 1) * HEAD_DIM)


def _mlp_ple_rows(x, p, vec_ref, w1_ref, w2_ref, gate_ref, proj_ref, final, filler=()):
    filler = iter(filler)

    def fill():
        for _ in range(FILL_PER_PIECE):
            thunk = next(filler, None)
            if thunk is not None:
                thunk()

    pieces = D_MODEL // MXU_COLS
    h = _rms(x, _row(vec_ref, V_NORM_MLP)).astype(BF16)
    xs = [x[:, n * MXU_COLS:(n + 1) * MXU_COLS] for n in range(pieces)]
    for c in range(D_FF // FF_CHUNK):
        hid = []
        for n in range(FF_CHUNK // MXU_COLS):
            lo = c * FF_CHUNK + n * MXU_COLS
            a = _dot(h, w1_ref[:, lo:lo + MXU_COLS])
            hid.append(jnp.square(jnp.maximum(a, 0.0)).astype(BF16))
            fill()
        hid = jnp.concatenate(hid, axis=1)
        for n in range(pieces):
            xs[n] = xs[n] + _dot(hid, w2_ref[c * FF_CHUNK:(c + 1) * FF_CHUNK, n * MXU_COLS:(n + 1) * MXU_COLS])
            fill()
    for thunk in filler:
        thunk()
    x = jnp.concatenate(xs, axis=1)
    hg = _rms(x, _row(vec_ref, V_NORM_PLE)).astype(BF16)
    gate = _sigmoid(_dot(hg, gate_ref[...]))
    x = x + _dot(p.astype(BF16), proj_ref[...]) * gate
    if final:
        x = _rms(x, _row(vec_ref, V_NORM_FINAL))
    return x


def _mlp_ple_kernel(x_ref, p_ref, vec_ref, w1_ref, w2_ref, gate_ref, proj_ref, o_ref, *, final):
    o_ref[...] = _mlp_ple_rows(x_ref[...], p_ref[...], vec_ref, w1_ref, w2_ref, gate_ref, proj_ref, final)


def _const_spec(shape):
    nd = len(shape)
    return pl.BlockSpec(shape, lambda *_: (0,) * nd, pipeline_mode=pl.Buffered(1))


def _mlp_ple(x2d, p4d, layer, lw, tile_m, final):
    rows = x2d.shape[0]
    return pl.pallas_call(
        functools.partial(_mlp_ple_kernel, final=final),
        grid=(rows // tile_m,),
        in_specs=[
            pl.BlockSpec((tile_m, D_MODEL), lambda t: (t, 0)),
            pl.BlockSpec((None, tile_m, PLE_DIM), lambda t: (layer, t, 0)),
            _const_spec((VEC_ROWS, D_MODEL)),
            _const_spec((D_MODEL, D_FF)),
            _const_spec((D_FF, D_MODEL)),
            _const_spec((D_MODEL, D_MODEL)),
            _const_spec((PLE_DIM, D_MODEL)),
        ],
        out_specs=pl.BlockSpec((tile_m, D_MODEL), lambda t: (t, 0)),
        out_shape=jax.ShapeDtypeStruct((rows, D_MODEL), F32),
        compiler_params=pltpu.CompilerParams(
            dimension_semantics=("arbitrary",), vmem_limit_bytes=VMEM_LIMIT),
        name=f"mlp_ple_l{layer}_r{rows}",
    )(x2d, p4d, lw["vec"], lw["w1"], lw["w2"], lw["gate"], lw["proj"])


def _glu(x, vec_ref, pw1_ref):
    h = _rms(x, _row(vec_ref, V_NORM_MIX)).astype(BF16)
    a = _dot(h, pw1_ref[:, :D_MODEL]) + _row(vec_ref, V_PW1_B_LO)
    g = _dot(h, pw1_ref[:, D_MODEL:]) + _row(vec_ref, V_PW1_B_HI)
    return a * _sigmoid(g)


def _depthwise_unit(ext_ref, dwb_ref, c_ref, r0, l):
    first = HALO - (CONV_WIDTH - 1)
    groups = CONV_ROWS // SUBLANES
    spans = -(-(first + CONV_WIDTH) // SUBLANES)
    lanes = slice(l * LANES, (l + 1) * LANES)
    win = [ext_ref[r0 + SUBLANES * g:r0 + SUBLANES * (g + 1), lanes] for g in range(groups + spans)]
    out = None
    for s in range(SUBLANES):
        taps = [(j, SUBLANES * j + s - first) for j in range(spans)
                if 0 <= SUBLANES * j + s - first < CONV_WIDTH]
        phase = []
        for g in range(groups + (1 if s else 0)):
            p = None
            for j, k in taps:
                term = dwb_ref[k, :, lanes] * win[g + j]
                p = term if p is None else p + term
            phase.append(p)
        shifted = jnp.concatenate(phase, axis=0)[s:s + CONV_ROWS, :]
        out = shifted if out is None else out + shifted
    c_ref[r0:r0 + CONV_ROWS, lanes] = out


def _conv_mlp_prompt_kernel(x_ref, p_ref, vec_ref, dwb_ref, pw1_ref, pw2_ref, w1_ref, w2_ref, gate_ref, proj_ref,
                            o_ref, cs_ref, ext_ref, c_ref, mid_ref, *, tile_m, tiles_per_stream):
    t = pl.program_id(0)
    ext_rows = tile_m + HALO
    slot = t % 2

    @pl.when(t % tiles_per_stream == 0)
    def _():
        ext_ref[0:HALO, :] = jnp.zeros((HALO, D_MODEL), F32)
        ext_ref[ext_rows:ext_rows + SUBLANES, :] = jnp.zeros((SUBLANES, D_MODEL), F32)

    @pl.when(t == 0)
    def _():
        mid_ref[1] = jnp.zeros((tile_m, D_MODEL), F32)

    x = x_ref[...]
    ext_ref[HALO:ext_rows, :] = _glu(x, vec_ref, pw1_ref)

    conv_units = [functools.partial(_depthwise_unit, ext_ref, dwb_ref, c_ref, r * CONV_ROWS, l)
                  for r in range(tile_m // CONV_ROWS) for l in range(D_MODEL // LANES)]
    o_ref[...] = _mlp_ple_rows(mid_ref[1 - slot], p_ref[...], vec_ref, w1_ref, w2_ref, gate_ref, proj_ref, False,
                               filler=conv_units)
    cs_ref[0] = ext_ref[ext_rows - (CONV_WIDTH - 1):ext_rows, :]
    ext_ref[0:HALO, :] = ext_ref[tile_m:ext_rows, :]
    act = _layernorm_silu(c_ref[...] + _row(vec_ref, V_DW_B), _row(vec_ref, V_LN_G), _row(vec_ref, V_LN_B))
    mid_ref[slot] = x + _dot(act.astype(BF16), pw2_ref[...]) + _row(vec_ref, V_PW2_B)


def _conv_mlp_prompt(x, p3d, layer, lw, tile_m):
    b, s, _ = x.shape
    tiles_per_stream = s // tile_m
    n_tiles = b * tiles_per_stream
    mixer_tile = lambda t: jnp.minimum(t, n_tiles - 1)
    mlp_tile = lambda t: jnp.maximum(t - 1, 0)
    y, cs = pl.pallas_call(
        functools.partial(_conv_mlp_prompt_kernel, tile_m=tile_m, tiles_per_stream=tiles_per_stream),
        grid=(n_tiles + 1,),
        in_specs=[
            pl.BlockSpec((tile_m, D_MODEL), lambda t: (mixer_tile(t), 0)),
            pl.BlockSpec((None, tile_m, PLE_DIM), lambda t: (layer, mlp_tile(t), 0)),
            _const_spec((VEC_ROWS, D_MODEL)),
            _const_spec((CONV_WIDTH, SUBLANES, D_MODEL)),
            _const_spec((D_MODEL, 2 * D_MODEL)),
            _const_spec((D_MODEL, D_MODEL)),
            _const_spec((D_MODEL, D_FF)),
            _const_spec((D_FF, D_MODEL)),
            _const_spec((D_MODEL, D_MODEL)),
            _const_spec((PLE_DIM, D_MODEL)),
        ],
        out_specs=[
            pl.BlockSpec((tile_m, D_MODEL), lambda t: (mlp_tile(t), 0)),
            pl.BlockSpec((1, CONV_WIDTH - 1, D_MODEL), lambda t: (mixer_tile(t) // tiles_per_stream, 0, 0)),
        ],
        out_shape=[
            jax.ShapeDtypeStruct((b * s, D_MODEL), F32),
            jax.ShapeDtypeStruct((b, CONV_WIDTH - 1, D_MODEL), F32),
        ],
        scratch_shapes=[
            pltpu.VMEM((tile_m + HALO + SUBLANES, D_MODEL), F32),
            pltpu.VMEM((tile_m, D_MODEL), F32),
            pltpu.VMEM((2, tile_m, D_MODEL), F32),
        ],
        compiler_params=pltpu.CompilerParams(
            dimension_semantics=("arbitrary",), vmem_limit_bytes=VMEM_LIMIT),
        name="conv_mlp_prompt",
    )(x.reshape(b * s, D_MODEL), p3d, lw["vec"], lw["dwb"], lw["pw1"], lw["pw2"],
      lw["w1"], lw["w2"], lw["gate"], lw["proj"])
    return y.reshape(b, s, D_MODEL), cs


def _conv_sample_kernel(x_ref, left_ref, vec_ref, dw_ref, pw1_ref, pw2_ref, o_ref, cs_ref, ext_ref, *, streams, steps):
    x = x_ref[...]
    ext_ref[:, 0:HALO - (CONV_WIDTH - 1), :] = jnp.zeros((streams, HALO - (CONV_WIDTH - 1), D_MODEL), F32)
    ext_ref[:, HALO - (CONV_WIDTH - 1):HALO, :] = left_ref[...]
    ext_ref[:, HALO:HALO + steps, :] = _glu(x, vec_ref, pw1_ref).reshape(streams, steps, D_MODEL)
    acc = jnp.broadcast_to(_row(vec_ref, V_DW_B).reshape(1, 1, D_MODEL), (streams, steps, D_MODEL))
    for k in range(CONV_WIDTH):
        lo = k + HALO - (CONV_WIDTH - 1)
        acc = acc + dw_ref[k:k + 1, :].reshape(1, 1, D_MODEL) * ext_ref[:, lo:lo + steps, :]
    act = _layernorm_silu(acc.reshape(streams * steps, D_MODEL), _row(vec_ref, V_LN_G), _row(vec_ref, V_LN_B))
    cs_ref[...] = ext_ref[:, HALO + steps - (CONV_WIDTH - 1):HALO + steps, :]
    o_ref[...] = x + _dot(act.astype(BF16), pw2_ref[...]) + _row(vec_ref, V_PW2_B)


def _conv_sample(x2d, left, lw, streams, steps):
    rows = streams * steps
    return pl.pallas_call(
        functools.partial(_conv_sample_kernel, streams=streams, steps=steps),
        grid=(1,),
        in_specs=[
            _const_spec((rows, D_MODEL)),
            _const_spec((streams, CONV_WIDTH - 1, D_MODEL)),
            _const_spec((VEC_ROWS, D_MODEL)),
            _const_spec((CONV_WIDTH + 1, D_MODEL)),
            _const_spec((D_MODEL, 2 * D_MODEL)),
            _const_spec((D_MODEL, D_MODEL)),
        ],
        out_specs=[
            pl.BlockSpec((rows, D_MODEL), lambda t: (0, 0)),
            pl.BlockSpec((streams, CONV_WIDTH - 1, D_MODEL), lambda t: (0, 0, 0)),
        ],
        out_shape=[
            jax.ShapeDtypeStruct((rows, D_MODEL), F32),
            jax.ShapeDtypeStruct((streams, CONV_WIDTH - 1, D_MODEL), F32),
        ],
        scratch_shapes=[pltpu.VMEM((streams, HALO + steps, D_MODEL), F32)],
        compiler_params=pltpu.CompilerParams(
            dimension_semantics=("arbitrary",), vmem_limit_bytes=VMEM_LIMIT),
        name="conv_sample",
    )(x2d, left, lw["vec"], lw["dw"], lw["pw1"], lw["pw2"])


def _attn_prompt_kernel(x_ref, vec_ref, bias_ref, wq_ref, wkv_ref, wo_ref, o_ref, ks_ref, vs_ref,
                        q_ref, kb_ref, vb_ref, att_ref, *, tile_m):
    t = pl.program_id(1)

    @pl.when(t == 0)
    def _():
        kb_ref[0:WINDOW, :] = jnp.zeros((WINDOW, KV_DIM), BF16)
        vb_ref[0:WINDOW, :] = jnp.zeros((WINDOW, KV_DIM), BF16)

    x = x_ref[0]
    h = _rms(x, _row(vec_ref, V_NORM_MIX)).astype(BF16)
    q_ref[...] = _dot(h, wq_ref[...]).astype(BF16)
    kv = _dot(h, wkv_ref[...])
    k_new = kv[:, :KV_DIM]
    v_new = kv[:, KV_DIM:]
    kb_ref[WINDOW:WINDOW + tile_m, :] = k_new.astype(BF16)
    vb_ref[WINDOW:WINDOW + tile_m, :] = v_new.astype(BF16)
    ks_ref[0] = k_new[tile_m - WINDOW:, :]
    vs_ref[0] = v_new[tile_m - WINDOW:, :]

    rows = GROUP * CHUNK

    def chunk(c):
        c0 = pl.multiple_of(c * CHUNK, CHUNK)
        q = jnp.concatenate(
            [q_ref[pl.ds(c0, CHUNK), g * KV_DIM:(g + 1) * KV_DIM] for g in range(GROUP)], axis=0)
        qm = jnp.concatenate(
            [jnp.where(_kv_lane_mask(q.shape, kh), q, jnp.zeros_like(q)) for kh in range(N_KV_HEADS)], axis=0)
        pad = jnp.zeros((KEYS_EXT - BAND, KV_DIM), BF16)
        kb = jnp.concatenate([kb_ref[pl.ds(c0, BAND), :], pad], axis=0)
        vb = vb_ref[pl.ds(c0, BAND), :]
        logits = lax.dot_general(qm, kb, (((1,), (1,)), ((), ())), preferred_element_type=F32)
        logits = logits + bias_ref[...]
        key_pos = t * tile_m + c0 - WINDOW + lax.broadcasted_iota(jnp.int32, (1, KEYS_EXT), 1)
        logits = jnp.where(key_pos >= 0, logits, NEG_INF)
        e = jnp.exp(logits - jnp.max(logits, axis=-1, keepdims=True)).astype(BF16)
        half = LANES // HEAD_DIM
        krow = lax.broadcasted_iota(jnp.int32, (KEYS_EXT, LANES), 0)
        klane = lax.broadcasted_iota(jnp.int32, (KEYS_EXT, LANES), 1)
        outs = []
        for pair in range(N_KV_HEADS // half):
            num = den = None
            for sub in range(half):
                kh = pair * half + sub
                in_head = (klane >= sub * HEAD_DIM) & (klane < (sub + 1) * HEAD_DIM)
                v_kh = jnp.concatenate([vb[:, pair * LANES:(pair + 1) * LANES], pad[:, 0:LANES]], axis=0)
                v_kh = jnp.where(in_head, v_kh, jnp.zeros_like(v_kh))
                ones = jnp.where(in_head & (krow <= BAND), 1.0, 0.0).astype(BF16)
                pv = _dot(e[kh * rows:(kh + 1) * rows, :], jnp.concatenate([v_kh, ones], axis=1))
                num = pv[:, 0:LANES] if num is None else num + pv[:, 0:LANES]
                den = pv[:, LANES:] if den is None else den + pv[:, LANES:]
            outs.append(num / den)
        out = jnp.concatenate(outs, axis=1)
        for g in range(GROUP):
            att_ref[pl.ds(c0, CHUNK), g * KV_DIM:(g + 1) * KV_DIM] = out[g * CHUNK:(g + 1) * CHUNK, :].astype(BF16)

    def chunk_pair(i, carry):
        chunk(2 * i)
        chunk(2 * i + 1)
        return carry

    lax.fori_loop(0, tile_m // (2 * CHUNK), chunk_pair, 0)

    kb_ref[0:WINDOW, :] = kb_ref[tile_m:tile_m + WINDOW, :]
    vb_ref[0:WINDOW, :] = vb_ref[tile_m:tile_m + WINDOW, :]
    o_ref[0] = x + _dot(att_ref[...], wo_ref[...])


def _attn_prompt(x, lw, bias, tile_m):
    b, s, _ = x.shape
    return pl.pallas_call(
        functools.partial(_attn_prompt_kernel, tile_m=tile_m),
        grid=(b, s // tile_m),
        in_specs=[
            pl.BlockSpec((1, tile_m, D_MODEL), lambda i, t: (i, t, 0)),
            _const_spec((VEC_ROWS, D_MODEL)),
            _const_spec(bias.shape),
            _const_spec((D_MODEL, D_MODEL)),
            _const_spec((D_MODEL, 2 * KV_DIM)),
            _const_spec((D_MODEL, D_MODEL)),
        ],
        out_specs=[
            pl.BlockSpec((1, tile_m, D_MODEL), lambda i, t: (i, t, 0)),
            pl.BlockSpec((1, WINDOW, KV_DIM), lambda i, t: (i, 0, 0)),
            pl.BlockSpec((1, WINDOW, KV_DIM), lambda i, t: (i, 0, 0)),
        ],
        out_shape=[
            jax.ShapeDtypeStruct((b, s, D_MODEL), F32),
            jax.ShapeDtypeStruct((b, WINDOW, KV_DIM), F32),
            jax.ShapeDtypeStruct((b, WINDOW, KV_DIM), F32),
        ],
        scratch_shapes=[
            pltpu.VMEM((tile_m, D_MODEL), BF16),
            pltpu.VMEM((tile_m + WINDOW, KV_DIM), BF16),
            pltpu.VMEM((tile_m + WINDOW, KV_DIM), BF16),
            pltpu.VMEM((tile_m, D_MODEL), BF16),
        ],
        compiler_params=pltpu.CompilerParams(
            dimension_semantics=("arbitrary", "arbitrary"), vmem_limit_bytes=VMEM_LIMIT),
        name="attn_prompt",
    )(x, lw["vec"], bias, lw["wq"], lw["wkv"], lw["wo"])


def _attn_sample_kernel(x_ref, kc_ref, vc_ref, vec_ref, bias_ref, wq_ref, wkv_ref, wo_ref,
                        o_ref, ks_ref, vs_ref, kf_ref, vf_ref, *, streams, steps):
    cache = kc_ref.shape[1]
    keys = cache + steps
    x = x_ref[...]
    h = _rms(x, _row(vec_ref, V_NORM_MIX)).astype(BF16)
    q = _dot(h, wq_ref[...]).reshape(streams, steps, D_MODEL)
    kv = _dot(h, wkv_ref[...])
    kf_ref[:, 0:cache, :] = kc_ref[...]
    vf_ref[:, 0:cache, :] = vc_ref[...]
    kf_ref[:, cache:keys, :] = kv[:, :KV_DIM].reshape(streams, steps, KV_DIM)
    vf_ref[:, cache:keys, :] = kv[:, KV_DIM:].reshape(streams, steps, KV_DIM)
    ks_ref[...] = kf_ref[:, steps:keys, :]
    vs_ref[...] = vf_ref[:, steps:keys, :]

    qs = jnp.concatenate([q[:, :, g * KV_DIM:(g + 1) * KV_DIM] for g in range(GROUP)], axis=1).astype(BF16)
    kf = kf_ref[...].astype(BF16)
    vf = vf_ref[...].astype(BF16)
    out = jnp.zeros((streams, GROUP * steps, KV_DIM), F32)
    for kh in range(N_KV_HEADS):
        k_kh = jnp.where(_kv_lane_mask(kf.shape, kh), kf, jnp.zeros_like(kf))
        logits = jnp.einsum("bqd,bkd->bqk", qs, k_kh, preferred_element_type=F32)
        logits = logits + bias_ref[kh, :, 0:keys][None]
        probs = _sink_softmax(logits, bias_ref[kh, :, keys:keys + 1][None])
        o_kh = jnp.einsum("bqk,bkd->bqd", probs.astype(BF16), vf, preferred_element_type=F32)
        out = jnp.where(_kv_lane_mask(out.shape, kh), o_kh, out)
    att = jnp.concatenate([out[:, g * steps:(g + 1) * steps, :] for g in range(GROUP)], axis=2)
    o_ref[...] = x + _dot(att.reshape(streams * steps, D_MODEL).astype(BF16), wo_ref[...])


def _attn_sample(x2d, k_left, v_left, lw, bias, streams, steps):
    rows = streams * steps
    cache = k_left.shape[1]
    return pl.pallas_call(
        functools.partial(_attn_sample_kernel, streams=streams, steps=steps),
        grid=(1,),
        in_specs=[
            _const_spec((rows, D_MODEL)),
            _const_spec(k_left.shape),
            _const_spec(v_left.shape),
            _const_spec((VEC_ROWS, D_MODEL)),
            _const_spec(bias.shape),
            _const_spec((D_MODEL, D_MODEL)),
            _const_spec((D_MODEL, 2 * KV_DIM)),
            _const_spec((D_MODEL, D_MODEL)),
        ],
        out_specs=[
            pl.BlockSpec((rows, D_MODEL), lambda t: (0, 0)),
            pl.BlockSpec((streams, cache, KV_DIM), lambda t: (0, 0, 0)),
            pl.BlockSpec((streams, cache, KV_DIM), lambda t: (0, 0, 0)),
        ],
        out_shape=[
            jax.ShapeDtypeStruct((rows, D_MODEL), F32),
            jax.ShapeDtypeStruct((streams, cache, KV_DIM), F32),
            jax.ShapeDtypeStruct((streams, cache, KV_DIM), F32),
        ],
        scratch_shapes=[
            pltpu.VMEM((streams, cache + steps, KV_DIM), F32),
            pltpu.VMEM((streams, cache + steps, KV_DIM), F32),
        ],
        compiler_params=pltpu.CompilerParams(
            dimension_semantics=("arbitrary",), vmem_limit_bytes=VMEM_LIMIT),
        name="attn_sample",
    )(x2d, k_left, v_left, lw["vec"], bias, lw["wq"], lw["wkv"], lw["wo"])


def _t5_bucket(rel):
    nb = NUM_BUCKETS // 2
    max_exact = nb // 2
    ret = (rel > 0).astype(jnp.int32) * nb
    n = jnp.abs(rel)
    nf = jnp.maximum(n, 1).astype(F32)
    large = max_exact + (jnp.log(nf / max_exact) / math.log(MAX_DISTANCE / max_exact)
                         * (nb - max_exact)).astype(jnp.int32)
    large = jnp.minimum(large, nb - 1)
    return ret + jnp.where(n < max_exact, n, large)


def _bias_table(rel_bias, sinks, q_len, k_len, offset):
    rel = jnp.arange(k_len, dtype=jnp.int32)[None, :] - offset - jnp.arange(q_len, dtype=jnp.int32)[:, None]
    onehot = (_t5_bucket(rel)[:, :, None] == jnp.arange(NUM_BUCKETS, dtype=jnp.int32)).astype(F32)
    b = jnp.einsum("qkn,nh->hqk", onehot, rel_bias.astype(F32), precision=lax.Precision.HIGHEST)
    b = b.reshape(N_KV_HEADS, GROUP * q_len, k_len)
    s = jnp.repeat(sinks.astype(F32).reshape(N_KV_HEADS, GROUP), q_len, axis=1)[:, :, None]
    pad = jnp.full((N_KV_HEADS, GROUP * q_len, KEYS_EXT - k_len - 1), NEG_INF, F32)
    return jnp.concatenate([b, s, pad], axis=2)


def _pack_vec(rows):
    zero = jnp.zeros((D_MODEL,), F32)
    return jnp.stack([rows[i].astype(F32) if i in rows else zero for i in range(VEC_ROWS)])


def kernel(x_prompt, x_sample, cache_conv, cache_k, cache_v, p_prompt, p_sample, rel_bias, norm_mix, norm_mlp, norm_ple, norm_final, conv_pw1, conv_pw1_b, conv_dw, conv_dw_b, conv_ln_g, conv_ln_b, conv_pw2, conv_pw2_b, attn_wq, attn_wk, attn_wv, attn_wo, attn_sinks, mlp_w1, mlp_w2, ple_proj, ple_gate):
    batch, seq, _ = x_prompt.shape
    streams, steps, _ = x_sample.shape
    cache = cache_k.shape[2]
    assert seq % TILE_M == 0 and TILE_M % CHUNK == 0 and TILE_M >= WINDOW and steps % SUBLANES == 0

    layers = []
    for i in range(DEPTH):
        j = i // 2
        rows = {V_NORM_MIX: norm_mix[i], V_NORM_MLP: norm_mlp[i], V_NORM_PLE: norm_ple[i], V_NORM_FINAL: norm_final}
        lw = {"w1": mlp_w1[i].astype(BF16), "w2": mlp_w2[i].astype(BF16),
              "gate": ple_gate[i].astype(BF16), "proj": ple_proj[i].astype(BF16)}
        if i % 2 == 0:
            rows.update({V_PW1_B_LO: conv_pw1_b[j, :D_MODEL], V_PW1_B_HI: conv_pw1_b[j, D_MODEL:],
                         V_DW_B: conv_dw_b[j], V_LN_G: conv_ln_g[j], V_LN_B: conv_ln_b[j], V_PW2_B: conv_pw2_b[j]})
            lw["pw1"] = conv_pw1[j].astype(BF16)
            lw["pw2"] = conv_pw2[j].astype(BF16)
            lw["dw"] = jnp.concatenate([conv_dw[j].astype(F32), jnp.zeros((1, D_MODEL), F32)], axis=0)
            lw["dwb"] = jnp.broadcast_to(conv_dw[j].astype(F32)[:, None, :], (CONV_WIDTH, SUBLANES, D_MODEL))
        else:
            wq = (attn_wq[j] * (HEAD_DIM ** -0.5)).reshape(D_MODEL, N_KV_HEADS, GROUP, HEAD_DIM).transpose(0, 2, 1, 3)
            lw["wq"] = wq.reshape(D_MODEL, D_MODEL).astype(BF16)
            lw["wkv"] = jnp.concatenate([attn_wk[j], attn_wv[j]], axis=1).astype(BF16)
            wo = attn_wo[j].reshape(N_KV_HEADS, GROUP, HEAD_DIM, D_MODEL).transpose(1, 0, 2, 3)
            lw["wo"] = wo.reshape(D_MODEL, D_MODEL).astype(BF16)
            lw["bias_prompt"] = _bias_table(rel_bias, attn_sinks[j], CHUNK, BAND, WINDOW).reshape(N_HEADS * CHUNK, 256)
            lw["bias_sample"] = _bias_table(rel_bias, attn_sinks[j], steps, cache + steps, cache)
        lw["vec"] = _pack_vec(rows)
        layers.append(lw)

    p_prompt2 = p_prompt.reshape(DEPTH, batch * seq, PLE_DIM)
    p_sample2 = p_sample.reshape(DEPTH, streams * steps, PLE_DIM)

    xp = x_prompt
    xs = x_sample.reshape(streams * steps, D_MODEL)
    conv_p, k_p, v_p, conv_s, k_s, v_s = [], [], [], [], [], []
    for i, lw in enumerate(layers):
        j = i // 2
        final = i == DEPTH - 1
        if i % 2 == 0:
            xp, cs = _conv_mlp_prompt(xp, p_prompt2, i, lw, TILE_M)
            conv_p.append(cs)
            xs, cs = _conv_sample(xs, cache_conv[j], lw, streams, steps)
            conv_s.append(cs)
        else:
            xp, ks, vs = _attn_prompt(xp, lw, lw["bias_prompt"], TILE_M)
            k_p.append(ks)
            v_p.append(vs)
            xs, ks, vs = _attn_sample(xs, cache_k[j].reshape(streams, cache, KV_DIM),
                                      cache_v[j].reshape(streams, cache, KV_DIM), lw, lw["bias_sample"], streams, steps)
            k_s.append(ks)
            v_s.append(vs)
            xp = _mlp_ple(xp.reshape(batch * seq, D_MODEL), p_prompt2, i, lw, MLP_TILE_M, final).reshape(batch, seq, D_MODEL)
        xs = _mlp_ple(xs, p_sample2, i, lw, streams * steps, final)

    def heads(ts, n):
        return jnp.stack(ts).reshape(len(ts), n, -1, N_KV_HEADS, HEAD_DIM)

    return (xp, xs.reshape(streams, steps, D_MODEL), jnp.stack(conv_p), heads(k_p, batch), heads(v_p, batch),
            jnp.stack(conv_s), heads(k_s, streams), heads(v_s, streams))
```

```python
import functools
import math

import jax
import jax.numpy as jnp
from jax import lax
from jax.experimental import pallas as pl
from jax.experimental.pallas import tpu as pltpu

D_MODEL = 1024
DEPTH = 4
CHUNK = 64
CONV_WIDTH = 31
WINDOW = 128
N_HEADS = 16
N_KV_HEADS = 4
GROUP = N_HEADS // N_KV_HEADS
HEAD_DIM = 64
KV_DIM = N_KV_HEADS * HEAD_DIM
D_FF = 4 * D_MODEL
PLE_DIM = 256
NUM_BUCKETS = 32
MAX_DISTANCE = 128
EPS = 1e-6
NEG_INF = -1e30

BAND = WINDOW + CHUNK
KEYS_EXT = 256
HALO = 32
SUBLANES = 8
LANES = 128
CONV_ROWS = 64
FF_CHUNK = 1024
MXU_COLS = 256
TILE_M = 512
VMEM_LIMIT = 60 * 1024 * 1024

V_NORM_MIX, V_PW1_B_LO, V_PW1_B_HI, V_DW_B, V_LN_G, V_LN_B, V_PW2_B, V_NORM_MLP, V_NORM_PLE, V_NORM_FINAL = range(10)
VEC_ROWS = 16

F32 = jnp.float32
BF16 = jnp.bfloat16


def _dot(a, b):
    return jnp.dot(a, b, preferred_element_type=F32)


def _sigmoid(x):
    return 1.0 / (1.0 + jnp.exp(-x))


def _rms(x, g):
    return x * lax.rsqrt(jnp.mean(x * x, axis=-1, keepdims=True) + EPS) * g


def _row(vec_ref, i):
    return vec_ref[i:i + 1, :]


def _layernorm_silu(c, g, b):
    mu = jnp.mean(c, axis=-1, keepdims=True)
    d = c - mu
    var = jnp.mean(d * d, axis=-1, keepdims=True)
    y = d * lax.rsqrt(var + EPS) * g + b
    return y * _sigmoid(y)


def _sink_softmax(logits, sink):
    m = jnp.maximum(jnp.max(logits, axis=-1, keepdims=True), sink)
    e = jnp.exp(logits - m)
    return e / (jnp.sum(e, axis=-1, keepdims=True) + jnp.exp(sink - m))


def _kv_lane_mask(shape, kh):
    lane = lax.broadcasted_iota(jnp.int32, shape, len(shape) - 1)
    return (lane >= kh * HEAD_DIM) & (lane < (kh + 1) * HEAD_DIM)


def _mlp_ple_rows(x, p, vec_ref, w1_ref, w2_ref, gate_ref, proj_ref, final, filler=()):
    filler = list(filler)
    pieces = D_MODEL // MXU_COLS
    n_pieces = (D_FF // FF_CHUNK) * (FF_CHUNK // MXU_COLS + pieces)
    done = [0, 0]

    def fill():
        done[0] += 1
        while done[1] < len(filler) * done[0] // n_pieces:
            filler[done[1]]()
            done[1] += 1

    h = _rms(x, _row(vec_ref, V_NORM_MLP)).astype(BF16)
    xs = [x[:, n * MXU_COLS:(n + 1) * MXU_COLS] for n in range(pieces)]
    for c in range(D_FF // FF_CHUNK):
        hid = []
        for n in range(FF_CHUNK // MXU_COLS):
            lo = c * FF_CHUNK + n * MXU_COLS
            a = _dot(h, w1_ref[:, lo:lo + MXU_COLS])
            hid.append(jnp.square(jnp.maximum(a, 0.0)).astype(BF16))
            fill()
        hid = jnp.concatenate(hid, axis=1)
        for n in range(pieces):
            xs[n] = xs[n] + _dot(hid, w2_ref[c * FF_CHUNK:(c + 1) * FF_CHUNK, n * MXU_COLS:(n + 1) * MXU_COLS])
            fill()
    assert done == [n_pieces, len(filler)]
    x = jnp.concatenate(xs, axis=1)
    hg = _rms(x, _row(vec_ref, V_NORM_PLE)).astype(BF16)
    gate = _sigmoid(_dot(hg, gate_ref[...]))
    x = x + _dot(p.astype(BF16), proj_ref[...]) * gate
    if final:
        x = _rms(x, _row(vec_ref, V_NORM_FINAL))
    return x


def _mlp_ple_kernel(x_ref, p_ref, vec_ref, w1_ref, w2_ref, gate_ref, proj_ref, o_ref, *, final):
    o_ref[...] = _mlp_ple_rows(x_ref[...], p_ref[...], vec_ref, w1_ref, w2_ref, gate_ref, proj_ref, final)


def _const_spec(shape):
    nd = len(shape)
    return pl.BlockSpec(shape, lambda *_: (0,) * nd, pipeline_mode=pl.Buffered(1))


def _mlp_ple(x2d, p4d, layer, lw, tile_m, final):
    rows = x2d.shape[0]
    return pl.pallas_call(
        functools.partial(_mlp_ple_kernel, final=final),
        grid=(rows // tile_m,),
        in_specs=[
            pl.BlockSpec((tile_m, D_MODEL), lambda t: (t, 0)),
            pl.BlockSpec((None, tile_m, PLE_DIM), lambda t: (layer, t, 0)),
            _const_spec((VEC_ROWS, D_MODEL)),
            _const_spec((D_MODEL, D_FF)),
            _const_spec((D_FF, D_MODEL)),
            _const_spec((D_MODEL, D_MODEL)),
            _const_spec((PLE_DIM, D_MODEL)),
        ],
        out_specs=pl.BlockSpec((tile_m, D_MODEL), lambda t: (t, 0)),
        out_shape=jax.ShapeDtypeStruct((rows, D_MODEL), F32),
        compiler_params=pltpu.CompilerParams(
            dimension_semantics=("arbitrary",), vmem_limit_bytes=VMEM_LIMIT),
        name=f"mlp_ple_l{layer}_r{rows}",
    )(x2d, p4d, lw["vec"], lw["w1"], lw["w2"], lw["gate"], lw["proj"])


def _glu(x, vec_ref, pw1_ref):
    h = _rms(x, _row(vec_ref, V_NORM_MIX)).astype(BF16)
    a = _dot(h, pw1_ref[:, :D_MODEL]) + _row(vec_ref, V_PW1_B_LO)
    g = _dot(h, pw1_ref[:, D_MODEL:]) + _row(vec_ref, V_PW1_B_HI)
    return a * _sigmoid(g)


def _depthwise_unit(ext_ref, dwb_ref, c_ref, r0, l):
    first = HALO - (CONV_WIDTH - 1)
    groups = CONV_ROWS // SUBLANES
    spans = -(-(first + CONV_WIDTH) // SUBLANES)
    lanes = slice(l * LANES, (l + 1) * LANES)
    win = [ext_ref[r0 + SUBLANES * g:r0 + SUBLANES * (g + 1), lanes] for g in range(groups + spans)]
    out = None
    for s in range(SUBLANES):
        taps = [(j, SUBLANES * j + s - first) for j in range(spans)
                if 0 <= SUBLANES * j + s - first < CONV_WIDTH]
        phase = []
        for g in range(groups + (1 if s else 0)):
            p = None
            for j, k in taps:
                term = dwb_ref[k, :, lanes] * win[g + j]
                p = term if p is None else p + term
            phase.append(p)
        shifted = jnp.concatenate(phase, axis=0)[s:s + CONV_ROWS, :]
        out = shifted if out is None else out + shifted
    c_ref[r0:r0 + CONV_ROWS, lanes] = out


def _conv_mlp_prompt_kernel(x_ref, p_ref, vec_ref, dwb_ref, pw1_ref, pw2_ref, w1_ref, w2_ref, gate_ref, proj_ref,
                            o_ref, cs_ref, ext_ref, c_ref, mid_ref, *, tile_m, tiles_per_stream):
    t = pl.program_id(0)
    ext_rows = tile_m + HALO
    slot = t % 2

    @pl.when(t % tiles_per_stream == 0)
    def _():
        ext_ref[0:HALO, :] = jnp.zeros((HALO, D_MODEL), F32)
        ext_ref[ext_rows:ext_rows + SUBLANES, :] = jnp.zeros((SUBLANES, D_MODEL), F32)

    @pl.when(t == 0)
    def _():
        mid_ref[1] = jnp.zeros((tile_m, D_MODEL), F32)

    x = x_ref[...]
    ext_ref[HALO:ext_rows, :] = _glu(x, vec_ref, pw1_ref)

    conv_units = [functools.partial(_depthwise_unit, ext_ref, dwb_ref, c_ref, r * CONV_ROWS, l)
                  for r in range(tile_m // CONV_ROWS) for l in range(D_MODEL // LANES)]
    o_ref[...] = _mlp_ple_rows(mid_ref[1 - slot], p_ref[...], vec_ref, w1_ref, w2_ref, gate_ref, proj_ref, False,
                               filler=conv_units)
    cs_ref[0] = ext_ref[ext_rows - (CONV_WIDTH - 1):ext_rows, :]
    ext_ref[0:HALO, :] = ext_ref[tile_m:ext_rows, :]
    act = _layernorm_silu(c_ref[...] + _row(vec_ref, V_DW_B), _row(vec_ref, V_LN_G), _row(vec_ref, V_LN_B))
    mid_ref[slot] = x + _dot(act.astype(BF16), pw2_ref[...]) + _row(vec_ref, V_PW2_B)


def _conv_mlp_prompt(x, p3d, layer, lw, tile_m):
    b, s, _ = x.shape
    tiles_per_stream = s // tile_m
    n_tiles = b * tiles_per_stream
    mixer_tile = lambda t: jnp.minimum(t, n_tiles - 1)
    mlp_tile = lambda t: jnp.maximum(t - 1, 0)
    y, cs = pl.pallas_call(
        functools.partial(_conv_mlp_prompt_kernel, tile_m=tile_m, tiles_per_stream=tiles_per_stream),
        grid=(n_tiles + 1,),
        in_specs=[
            pl.BlockSpec((tile_m, D_MODEL), lambda t: (mixer_tile(t), 0)),
            pl.BlockSpec((None, tile_m, PLE_DIM), lambda t: (layer, mlp_tile(t), 0)),
            _const_spec((VEC_ROWS, D_MODEL)),
            _const_spec((CONV_WIDTH, SUBLANES, D_MODEL)),
            _const_spec((D_MODEL, 2 * D_MODEL)),
            _const_spec((D_MODEL, D_MODEL)),
            _const_spec((D_MODEL, D_FF)),
            _const_spec((D_FF, D_MODEL)),
            _const_spec((D_MODEL, D_MODEL)),
            _const_spec((PLE_DIM, D_MODEL)),
        ],
        out_specs=[
            pl.BlockSpec((tile_m, D_MODEL), lambda t: (mlp_tile(t), 0)),
            pl.BlockSpec((1, CONV_WIDTH - 1, D_MODEL), lambda t: (mixer_tile(t) // tiles_per_stream, 0, 0)),
        ],
        out_shape=[
            jax.ShapeDtypeStruct((b * s, D_MODEL), F32),
            jax.ShapeDtypeStruct((b, CONV_WIDTH - 1, D_MODEL), F32),
        ],
        scratch_shapes=[
            pltpu.VMEM((tile_m + HALO + SUBLANES, D_MODEL), F32),
            pltpu.VMEM((tile_m, D_MODEL), F32),
            pltpu.VMEM((2, tile_m, D_MODEL), F32),
        ],
        compiler_params=pltpu.CompilerParams(
            dimension_semantics=("arbitrary",), vmem_limit_bytes=VMEM_LIMIT),
        name="conv_mlp_prompt",
    )(x.reshape(b * s, D_MODEL), p3d, lw["vec"], lw["dwb"], lw["pw1"], lw["pw2"],
      lw["w1"], lw["w2"], lw["gate"], lw["proj"])
    return y.reshape(b, s, D_MODEL), cs


def _conv_sample_kernel(x_ref, left_ref, vec_ref, dw_ref, pw1_ref, pw2_ref, o_ref, cs_ref, ext_ref, *, streams, steps):
    x = x_ref[...]
    ext_ref[:, 0:HALO - (CONV_WIDTH - 1), :] = jnp.zeros((streams, HALO - (CONV_WIDTH - 1), D_MODEL), F32)
    ext_ref[:, HALO - (CONV_WIDTH - 1):HALO, :] = left_ref[...]
    ext_ref[:, HALO:HALO + steps, :] = _glu(x, vec_ref, pw1_ref).reshape(streams, steps, D_MODEL)
    acc = jnp.broadcast_to(_row(vec_ref, V_DW_B).reshape(1, 1, D_MODEL), (streams, steps, D_MODEL))
    for k in range(CONV_WIDTH):
        lo = k + HALO - (CONV_WIDTH - 1)
        acc = acc + dw_ref[k:k + 1, :].reshape(1, 1, D_MODEL) * ext_ref[:, lo:lo + steps, :]
    act = _layernorm_silu(acc.reshape(streams * steps, D_MODEL), _row(vec_ref, V_LN_G), _row(vec_ref, V_LN_B))
    cs_ref[...] = ext_ref[:, HALO + steps - (CONV_WIDTH - 1):HALO + steps, :]
    o_ref[...] = x + _dot(act.astype(BF16), pw2_ref[...]) + _row(vec_ref, V_PW2_B)


def _conv_sample(x2d, left, lw, streams, steps):
    rows = streams * steps
    return pl.pallas_call(
        functools.partial(_conv_sample_kernel, streams=streams, steps=steps),
        grid=(1,),
        in_specs=[
            _const_spec((rows, D_MODEL)),
            _const_spec((streams, CONV_WIDTH - 1, D_MODEL)),
            _const_spec((VEC_ROWS, D_MODEL)),
            _const_spec((CONV_WIDTH + 1, D_MODEL)),
            _const_spec((D_MODEL, 2 * D_MODEL)),
            _const_spec((D_MODEL, D_MODEL)),
        ],
        out_specs=[
            pl.BlockSpec((rows, D_MODEL), lambda t: (0, 0)),
            pl.BlockSpec((streams, CONV_WIDTH - 1, D_MODEL), lambda t: (0, 0, 0)),
        ],
        out_shape=[
            jax.ShapeDtypeStruct((rows, D_MODEL), F32),
            jax.ShapeDtypeStruct((streams, CONV_WIDTH - 1, D_MODEL), F32),
        ],
        scratch_shapes=[pltpu.VMEM((streams, HALO + steps, D_MODEL), F32)],
        compiler_params=pltpu.CompilerParams(
            dimension_semantics=("arbitrary",), vmem_limit_bytes=VMEM_LIMIT),
        name="conv_sample",
    )(x2d, left, lw["vec"], lw["dw"], lw["pw1"], lw["pw2"])


def _attn_mlp_prompt_kernel(x_ref, p_ref, vec_ref, bias_ref, wq_ref, wkv_ref, wo_ref, w1_ref, w2_ref, gate_ref,
                            proj_ref, o_ref, ks_ref, vs_ref, q_ref, kb_ref, vb_ref, att_ref, mid_ref,
                            *, tile_m, tiles_per_stream, final):
    t = pl.program_id(0)
    slot = t % 2
    tile_in_stream = t % tiles_per_stream

    @pl.when(tile_in_stream == 0)
    def _():
        kb_ref[0:WINDOW, :] = jnp.zeros((WINDOW, KV_DIM), BF16)
        vb_ref[0:WINDOW, :] = jnp.zeros((WINDOW, KV_DIM), BF16)

    @pl.when(t == 0)
    def _():
        mid_ref[1] = jnp.zeros((tile_m, D_MODEL), F32)

    x = x_ref[...]
    h = _rms(x, _row(vec_ref, V_NORM_MIX)).astype(BF16)
    q_ref[...] = _dot(h, wq_ref[...]).astype(BF16)
    kv = _dot(h, wkv_ref[...])
    k_new = kv[:, :KV_DIM]
    v_new = kv[:, KV_DIM:]
    kb_ref[WINDOW:WINDOW + tile_m, :] = k_new.astype(BF16)
    vb_ref[WINDOW:WINDOW + tile_m, :] = v_new.astype(BF16)
    ks_ref[0] = k_new[tile_m - WINDOW:, :]
    vs_ref[0] = v_new[tile_m - WINDOW:, :]

    rows = GROUP * CHUNK
    pad = jnp.zeros((KEYS_EXT - BAND, KV_DIM), BF16)
    weights = {}

    def chunk_logits(c0):
        q = jnp.concatenate(
            [q_ref[c0:c0 + CHUNK, g * KV_DIM:(g + 1) * KV_DIM] for g in range(GROUP)], axis=0)
        qm = jnp.concatenate(
            [jnp.where(_kv_lane_mask(q.shape, kh), q, jnp.zeros_like(q)) for kh in range(N_KV_HEADS)], axis=0)
        kb = jnp.concatenate([kb_ref[c0:c0 + BAND, :], pad], axis=0)
        logits = lax.dot_general(qm, kb, (((1,), (1,)), ((), ())), preferred_element_type=F32)
        logits = logits + bias_ref[...]
        key_pos = tile_in_stream * tile_m + c0 - WINDOW + lax.broadcasted_iota(jnp.int32, (1, KEYS_EXT), 1)
        logits = jnp.where(key_pos >= 0, logits, NEG_INF)
        weights[c0] = jnp.exp(logits - jnp.max(logits, axis=-1, keepdims=True)).astype(BF16)

    def chunk_values(c0):
        e = weights.pop(c0)
        vb = vb_ref[c0:c0 + BAND, :]
        half = LANES // HEAD_DIM
        krow = lax.broadcasted_iota(jnp.int32, (KEYS_EXT, LANES), 0)
        klane = lax.broadcasted_iota(jnp.int32, (KEYS_EXT, LANES), 1)
        outs = []
        for pair in range(N_KV_HEADS // half):
            num = den = None
            for sub in range(half):
                kh = pair * half + sub
                in_head = (klane >= sub * HEAD_DIM) & (klane < (sub + 1) * HEAD_DIM)
                v_kh = jnp.concatenate([vb[:, pair * LANES:(pair + 1) * LANES], pad[:, 0:LANES]], axis=0)
                v_kh = jnp.where(in_head, v_kh, jnp.zeros_like(v_kh))
                ones = jnp.where(in_head & (krow <= BAND), 1.0, 0.0).astype(BF16)
                pv = _dot(e[kh * rows:(kh + 1) * rows, :], jnp.concatenate([v_kh, ones], axis=1))
                num = pv[:, 0:LANES] if num is None else num + pv[:, 0:LANES]
                den = pv[:, LANES:] if den is None else den + pv[:, LANES:]
            outs.append(num / den)
        out = jnp.concatenate(outs, axis=1)
        for g in range(GROUP):
            att_ref[c0:c0 + CHUNK, g * KV_DIM:(g + 1) * KV_DIM] = out[g * CHUNK:(g + 1) * CHUNK, :].astype(BF16)

    chunk_units = []
    for c0 in range(0, tile_m, CHUNK):
        chunk_units += [functools.partial(chunk_logits, c0), functools.partial(chunk_values, c0)]
    o_ref[...] = _mlp_ple_rows(mid_ref[1 - slot], p_ref[...], vec_ref, w1_ref, w2_ref, gate_ref, proj_ref, final,
                               filler=chunk_units)

    mid_ref[slot] = x + _dot(att_ref[...], wo_ref[...])

    @pl.when(t >= 0)
    def _():
        kb_ref[0:WINDOW, :] = kb_ref[tile_m:tile_m + WINDOW, :]
        vb_ref[0:WINDOW, :] = vb_ref[tile_m:tile_m + WINDOW, :]


def _attn_mlp_prompt(x, p3d, layer, lw, bias, tile_m, final):
    b, s, _ = x.shape
    tiles_per_stream = s // tile_m
    n_tiles = b * tiles_per_stream
    mixer_tile = lambda t: jnp.minimum(t, n_tiles - 1)
    mlp_tile = lambda t: jnp.maximum(t - 1, 0)
    state_spec = pl.BlockSpec((1, WINDOW, KV_DIM), lambda t: (mixer_tile(t) // tiles_per_stream, 0, 0))
    y, ks, vs = pl.pallas_call(
        functools.partial(_attn_mlp_prompt_kernel, tile_m=tile_m, tiles_per_stream=tiles_per_stream, final=final),
        grid=(n_tiles + 1,),
        in_specs=[
            pl.BlockSpec((tile_m, D_MODEL), lambda t: (mixer_tile(t), 0)),
            pl.BlockSpec((None, tile_m, PLE_DIM), lambda t: (layer, mlp_tile(t), 0)),
            _const_spec((VEC_ROWS, D_MODEL)),
            _const_spec(bias.shape),
            _const_spec((D_MODEL, D_MODEL)),
            _const_spec((D_MODEL, 2 * KV_DIM)),
            _const_spec((D_MODEL, D_MODEL)),
            _const_spec((D_MODEL, D_FF)),
            _const_spec((D_FF, D_MODEL)),
            _const_spec((D_MODEL, D_MODEL)),
            _const_spec((PLE_DIM, D_MODEL)),
        ],
        out_specs=[
            pl.BlockSpec((tile_m, D_MODEL), lambda t: (mlp_tile(t), 0)),
            state_spec,
            state_spec,
        ],
        out_shape=[
            jax.ShapeDtypeStruct((b * s, D_MODEL), F32),
            jax.ShapeDtypeStruct((b, WINDOW, KV_DIM), F32),
            jax.ShapeDtypeStruct((b, WINDOW, KV_DIM), F32),
        ],
        scratch_shapes=[
            pltpu.VMEM((tile_m, D_MODEL), BF16),
            pltpu.VMEM((tile_m + WINDOW, KV_DIM), BF16),
            pltpu.VMEM((tile_m + WINDOW, KV_DIM), BF16),
            pltpu.VMEM((tile_m, D_MODEL), BF16),
            pltpu.VMEM((2, tile_m, D_MODEL), F32),
        ],
        compiler_params=pltpu.CompilerParams(
            dimension_semantics=("arbitrary",), vmem_limit_bytes=VMEM_LIMIT),
        name="attn_mlp_prompt",
    )(x.reshape(b * s, D_MODEL), p3d, lw["vec"], bias, lw["wq"], lw["wkv"], lw["wo"],
      lw["w1"], lw["w2"], lw["gate"], lw["proj"])
    return y.reshape(b, s, D_MODEL), ks, vs


def _attn_sample_kernel(x_ref, kc_ref, vc_ref, vec_ref, bias_ref, wq_ref, wkv_ref, wo_ref,
                        o_ref, ks_ref, vs_ref, kf_ref, vf_ref, *, streams, steps):
    cache = kc_ref.shape[1]
    keys = cache + steps
    x = x_ref[...]
    h = _rms(x, _row(vec_ref, V_NORM_MIX)).astype(BF16)
    q = _dot(h, wq_ref[...]).reshape(streams, steps, D_MODEL)
    kv = _dot(h, wkv_ref[...])
    kf_ref[:, 0:cache, :] = kc_ref[...]
    vf_ref[:, 0:cache, :] = vc_ref[...]
    kf_ref[:, cache:keys, :] = kv[:, :KV_DIM].reshape(streams, steps, KV_DIM)
    vf_ref[:, cache:keys, :] = kv[:, KV_DIM:].reshape(streams, steps, KV_DIM)
    ks_ref[...] = kf_ref[:, steps:keys, :]
    vs_ref[...] = vf_ref[:, steps:keys, :]

    qs = jnp.concatenate([q[:, :, g * KV_DIM:(g + 1) * KV_DIM] for g in range(GROUP)], axis=1).astype(BF16)
    kf = kf_ref[...].astype(BF16)
    vf = vf_ref[...].astype(BF16)
    out = jnp.zeros((streams, GROUP * steps, KV_DIM), F32)
    for kh in range(N_KV_HEADS):
        k_kh = jnp.where(_kv_lane_mask(kf.shape, kh), kf, jnp.zeros_like(kf))
        logits = jnp.einsum("bqd,bkd->bqk", qs, k_kh, preferred_element_type=F32)
        logits = logits + bias_ref[kh, :, 0:keys][None]
        probs = _sink_softmax(logits, bias_ref[kh, :, keys:keys + 1][None])
        o_kh = jnp.einsum("bqk,bkd->bqd", probs.astype(BF16), vf, preferred_element_type=F32)
        out = jnp.where(_kv_lane_mask(out.shape, kh), o_kh, out)
    att = jnp.concatenate([out[:, g * steps:(g + 1) * steps, :] for g in range(GROUP)], axis=2)
    o_ref[...] = x + _dot(att.reshape(streams * steps, D_MODEL).astype(BF16), wo_ref[...])


def _attn_sample(x2d, k_left, v_left, lw, bias, streams, steps):
    rows = streams * steps
    cache = k_left.shape[1]
    return pl.pallas_call(
        functools.partial(_attn_sample_kernel, streams=streams, steps=steps),
        grid=(1,),
        in_specs=[
            _const_spec((rows, D_MODEL)),
            _const_spec(k_left.shape),
            _const_spec(v_left.shape),
            _const_spec((VEC_ROWS, D_MODEL)),
            _const_spec(bias.shape),
            _const_spec((D_MODEL, D_MODEL)),
            _const_spec((D_MODEL, 2 * KV_DIM)),
            _const_spec((D_MODEL, D_MODEL)),
        ],
        out_specs=[
            pl.BlockSpec((rows, D_MODEL), lambda t: (0, 0)),
            pl.BlockSpec((streams, cache, KV_DIM), lambda t: (0, 0, 0)),
            pl.BlockSpec((streams, cache, KV_DIM), lambda t: (0, 0, 0)),
        ],
        out_shape=[
            jax.ShapeDtypeStruct((rows, D_MODEL), F32),
            jax.ShapeDtypeStruct((streams, cache, KV_DIM), F32),
            jax.ShapeDtypeStruct((streams, cache, KV_DIM), F32),
        ],
        scratch_shapes=[
            pltpu.VMEM((streams, cache + steps, KV_DIM), F32),
            pltpu.VMEM((streams, cache + steps, KV_DIM), F32),
        ],
        compiler_params=pltpu.CompilerParams(
            dimension_semantics=("arbitrary",), vmem_limit_bytes=VMEM_LIMIT),
        name="attn_sample",
    )(x2d, k_left, v_left, lw["vec"], bias, lw["wq"], lw["wkv"], lw["wo"])


def _t5_bucket(rel):
    nb = NUM_BUCKETS // 2
    max_exact = nb // 2
    ret = (rel > 0).astype(jnp.int32) * nb
    n = jnp.abs(rel)
    nf = jnp.maximum(n, 1).astype(F32)
    large = max_exact + (jnp.log(nf / max_exact) / math.log(MAX_DISTANCE / max_exact)
                         * (nb - max_exact)).astype(jnp.int32)
    large = jnp.minimum(large, nb - 1)
    return ret + jnp.where(n < max_exact, n, large)


def _bias_table(rel_bias, sinks, q_len, k_len, offset):
    rel = jnp.arange(k_len, dtype=jnp.int32)[None, :] - offset - jnp.arange(q_len, dtype=jnp.int32)[:, None]
    onehot = (_t5_bucket(rel)[:, :, None] == jnp.arange(NUM_BUCKETS, dtype=jnp.int32)).astype(F32)
    b = jnp.einsum("qkn,nh->hqk", onehot, rel_bias.astype(F32), precision=lax.Precision.HIGHEST)
    b = b.reshape(N_KV_HEADS, GROUP * q_len, k_len)
    s = jnp.repeat(sinks.astype(F32).reshape(N_KV_HEADS, GROUP), q_len, axis=1)[:, :, None]
    pad = jnp.full((N_KV_HEADS, GROUP * q_len, KEYS_EXT - k_len - 1), NEG_INF, F32)
    return jnp.concatenate([b, s, pad], axis=2)


def _pack_vec(rows):
    zero = jnp.zeros((D_MODEL,), F32)
    return jnp.stack([rows[i].astype(F32) if i in rows else zero for i in range(VEC_ROWS)])


def kernel(x_prompt, x_sample, cache_conv, cache_k, cache_v, p_prompt, p_sample, rel_bias, norm_mix, norm_mlp, norm_ple, norm_final, conv_pw1, conv_pw1_b, conv_dw, conv_dw_b, conv_ln_g, conv_ln_b, conv_pw2, conv_pw2_b, attn_wq, attn_wk, attn_wv, attn_wo, attn_sinks, mlp_w1, mlp_w2, ple_proj, ple_gate):
    batch, seq, _ = x_prompt.shape
    streams, steps, _ = x_sample.shape
    cache = cache_k.shape[2]
    assert seq % TILE_M == 0 and TILE_M % CHUNK == 0 and TILE_M >= WINDOW and steps % SUBLANES == 0

    layers = []
    for i in range(DEPTH):
        j = i // 2
        rows = {V_NORM_MIX: norm_mix[i], V_NORM_MLP: norm_mlp[i], V_NORM_PLE: norm_ple[i], V_NORM_FINAL: norm_final}
        lw = {"w1": mlp_w1[i].astype(BF16), "w2": mlp_w2[i].astype(BF16),
              "gate": ple_gate[i].astype(BF16), "proj": ple_proj[i].astype(BF16)}
        if i % 2 == 0:
            rows.update({V_PW1_B_LO: conv_pw1_b[j, :D_MODEL], V_PW1_B_HI: conv_pw1_b[j, D_MODEL:],
                         V_DW_B: conv_dw_b[j], V_LN_G: conv_ln_g[j], V_LN_B: conv_ln_b[j], V_PW2_B: conv_pw2_b[j]})
            lw["pw1"] = conv_pw1[j].astype(BF16)
            lw["pw2"] = conv_pw2[j].astype(BF16)
            lw["dw"] = jnp.concatenate([conv_dw[j].astype(F32), jnp.zeros((1, D_MODEL), F32)], axis=0)
            lw["dwb"] = jnp.broadcast_to(conv_dw[j].astype(F32)[:, None, :], (CONV_WIDTH, SUBLANES, D_MODEL))
        else:
            wq = (attn_wq[j] * (HEAD_DIM ** -0.5)).reshape(D_MODEL, N_KV_HEADS, GROUP, HEAD_DIM).transpose(0, 2, 1, 3)
            lw["wq"] = wq.reshape(D_MODEL, D_MODEL).astype(BF16)
            lw["wkv"] = jnp.concatenate([attn_wk[j], attn_wv[j]], axis=1).astype(BF16)
            wo = attn_wo[j].reshape(N_KV_HEADS, GROUP, HEAD_DIM, D_MODEL).transpose(1, 0, 2, 3)
            lw["wo"] = wo.reshape(D_MODEL, D_MODEL).astype(BF16)
            lw["bias_prompt"] = _bias_table(rel_bias, attn_sinks[j], CHUNK, BAND, WINDOW).reshape(N_HEADS * CHUNK, 256)
            lw["bias_sample"] = _bias_table(rel_bias, attn_sinks[j], steps, cache + steps, cache)
        lw["vec"] = _pack_vec(rows)
        layers.append(lw)

    p_prompt2 = p_prompt.reshape(DEPTH, batch * seq, PLE_DIM)
    p_sample2 = p_sample.reshape(DEPTH, streams * steps, PLE_DIM)

    xp = x_prompt
    xs = x_sample.reshape(streams * steps, D_MODEL)
    conv_p, k_p, v_p, conv_s, k_s, v_s = [], [], [], [], [], []
    for i, lw in enumerate(layers):
        j = i // 2
        final = i == DEPTH - 1
        if i % 2 == 0:
            xp, cs = _conv_mlp_prompt(xp, p_prompt2, i, lw, TILE_M)
            conv_p.append(cs)
            xs, cs = _conv_sample(xs, cache_conv[j], lw, streams, steps)
            conv_s.append(cs)
        else:
            xp, ks, vs = _attn_mlp_prompt(xp, p_prompt2, i, lw, lw["bias_prompt"], TILE_M, final)
            k_p.append(ks)
            v_p.append(vs)
            xs, ks, vs = _attn_sample(xs, cache_k[j].reshape(streams, cache, KV_DIM),
                                      cache_v[j].reshape(streams, cache, KV_DIM), lw, lw["bias_sample"], streams, steps)
            k_s.append(ks)
            v_s.append(vs)
        xs = _mlp_ple(xs, p_sample2, i, lw, streams * steps, final)

    def heads(ts, n):
        return jnp.stack(ts).reshape(len(ts), n, -1, N_KV_HEADS, HEAD_DIM)

    return (xp, xs.reshape(streams, steps, D_MODEL), jnp.stack(conv_p), heads(k_p, batch), heads(v_p, batch),
            jnp.stack(conv_s), heads(k_s, streams), heads(v_s, streams))
```

```python
import functools
import math

import jax
import jax.numpy as jnp
from jax import lax
from jax.experimental import pallas as pl
from jax.experimental.pallas import tpu as pltpu

D_MODEL = 1024
DEPTH = 4
CHUNK = 64
CONV_WIDTH = 31
WINDOW = 128
N_HEADS = 16
N_KV_HEADS = 4
GROUP = N_HEADS // N_KV_HEADS
HEAD_DIM = 64
KV_DIM = N_KV_HEADS * HEAD_DIM
D_FF = 4 * D_MODEL
PLE_DIM = 256
NUM_BUCKETS = 32
MAX_DISTANCE = 128
EPS = 1e-6
NEG_INF = -1e30

BAND = WINDOW + CHUNK
KEYS_EXT = 256
HALO = 32
SUBLANES = 8
LANES = 128
CONV_ROWS = 64
FF_CHUNK = 1024
MXU_COLS = 256
LEAD_WEIGHT = 2
TILE_M = 512
VMEM_LIMIT = 60 * 1024 * 1024

V_NORM_MIX, V_PW1_B_LO, V_PW1_B_HI, V_DW_B, V_LN_G, V_LN_B, V_PW2_B, V_NORM_MLP, V_NORM_PLE, V_NORM_FINAL = range(10)
VEC_ROWS = 16

F32 = jnp.float32
BF16 = jnp.bfloat16


def _dot(a, b):
    return jnp.dot(a, b, preferred_element_type=F32)


def _sigmoid(x):
    return 1.0 / (1.0 + jnp.exp(-x))


def _rms(x, g):
    return x * lax.rsqrt(jnp.mean(x * x, axis=-1, keepdims=True) + EPS) * g


def _row(vec_ref, i):
    return vec_ref[i:i + 1, :]


def _layernorm_silu(c, g, b):
    mu = jnp.mean(c, axis=-1, keepdims=True)
    d = c - mu
    var = jnp.mean(d * d, axis=-1, keepdims=True)
    y = d * lax.rsqrt(var + EPS) * g + b
    return y * _sigmoid(y)


def _sink_softmax(logits, sink):
    m = jnp.maximum(jnp.max(logits, axis=-1, keepdims=True), sink)
    e = jnp.exp(logits - m)
    return e / (jnp.sum(e, axis=-1, keepdims=True) + jnp.exp(sink - m))


def _kv_lane_mask(shape, kh):
    lane = lax.broadcasted_iota(jnp.int32, shape, len(shape) - 1)
    return (lane >= kh * HEAD_DIM) & (lane < (kh + 1) * HEAD_DIM)


def _mlp_ple_rows(x, p, vec_ref, w1_ref, w2_ref, gate_ref, proj_ref, final, filler=(), lead=()):
    filler = list(filler)
    pieces = D_MODEL // MXU_COLS
    n_pieces = (D_FF // FF_CHUNK) * (FF_CHUNK // MXU_COLS + pieces) + LEAD_WEIGHT * len(lead)
    done = [0, 0]

    def fill(weight=1):
        done[0] += weight
        while done[1] < len(filler) * done[0] // n_pieces:
            filler[done[1]]()
            done[1] += 1

    for piece in lead:
        piece()
        fill(LEAD_WEIGHT)
    h = _rms(x, _row(vec_ref, V_NORM_MLP)).astype(BF16)
    xs = [x[:, n * MXU_COLS:(n + 1) * MXU_COLS] for n in range(pieces)]
    for c in range(D_FF // FF_CHUNK):
        hid = []
        for n in range(FF_CHUNK // MXU_COLS):
            lo = c * FF_CHUNK + n * MXU_COLS
            a = _dot(h, w1_ref[:, lo:lo + MXU_COLS])
            hid.append(jnp.square(jnp.maximum(a, 0.0)).astype(BF16))
            fill()
        hid = jnp.concatenate(hid, axis=1)
        for n in range(pieces):
            xs[n] = xs[n] + _dot(hid, w2_ref[c * FF_CHUNK:(c + 1) * FF_CHUNK, n * MXU_COLS:(n + 1) * MXU_COLS])
            fill()
    assert done == [n_pieces, len(filler)]
    x = jnp.concatenate(xs, axis=1)
    hg = _rms(x, _row(vec_ref, V_NORM_PLE)).astype(BF16)
    gate = _sigmoid(_dot(hg, gate_ref[...]))
    x = x + _dot(p.astype(BF16), proj_ref[...]) * gate
    if final:
        x = _rms(x, _row(vec_ref, V_NORM_FINAL))
    return x


def _mlp_ple_kernel(x_ref, p_ref, vec_ref, w1_ref, w2_ref, gate_ref, proj_ref, o_ref, *, final):
    o_ref[...] = _mlp_ple_rows(x_ref[...], p_ref[...], vec_ref, w1_ref, w2_ref, gate_ref, proj_ref, final)


def _const_spec(shape):
    nd = len(shape)
    return pl.BlockSpec(shape, lambda *_: (0,) * nd, pipeline_mode=pl.Buffered(1))


def _mlp_ple(x2d, p4d, layer, lw, tile_m, final):
    rows = x2d.shape[0]
    return pl.pallas_call(
        functools.partial(_mlp_ple_kernel, final=final),
        grid=(rows // tile_m,),
        in_specs=[
            pl.BlockSpec((tile_m, D_MODEL), lambda t: (t, 0)),
            pl.BlockSpec((None, tile_m, PLE_DIM), lambda t: (layer, t, 0)),
            _const_spec((VEC_ROWS, D_MODEL)),
            _const_spec((D_MODEL, D_FF)),
            _const_spec((D_FF, D_MODEL)),
            _const_spec((D_MODEL, D_MODEL)),
            _const_spec((PLE_DIM, D_MODEL)),
        ],
        out_specs=pl.BlockSpec((tile_m, D_MODEL), lambda t: (t, 0)),
        out_shape=jax.ShapeDtypeStruct((rows, D_MODEL), F32),
        compiler_params=pltpu.CompilerParams(
            dimension_semantics=("arbitrary",), vmem_limit_bytes=VMEM_LIMIT),
        name=f"mlp_ple_l{layer}_r{rows}",
    )(x2d, p4d, lw["vec"], lw["w1"], lw["w2"], lw["gate"], lw["proj"])


def _glu(x, vec_ref, pw1_ref):
    h = _rms(x, _row(vec_ref, V_NORM_MIX)).astype(BF16)
    a = _dot(h, pw1_ref[:, :D_MODEL]) + _row(vec_ref, V_PW1_B_LO)
    g = _dot(h, pw1_ref[:, D_MODEL:]) + _row(vec_ref, V_PW1_B_HI)
    return a * _sigmoid(g)


def _depthwise_unit(ext_ref, dwb_ref, c_ref, r0, l):
    first = HALO - (CONV_WIDTH - 1)
    groups = CONV_ROWS // SUBLANES
    spans = -(-(first + CONV_WIDTH) // SUBLANES)
    lanes = slice(l * LANES, (l + 1) * LANES)
    win = [ext_ref[r0 + SUBLANES * g:r0 + SUBLANES * (g + 1), lanes] for g in range(groups + spans)]
    out = None
    for s in range(SUBLANES):
        taps = [(j, SUBLANES * j + s - first) for j in range(spans)
                if 0 <= SUBLANES * j + s - first < CONV_WIDTH]
        phase = []
        for g in range(groups + (1 if s else 0)):
            p = None
            for j, k in taps:
                term = dwb_ref[k, :, lanes] * win[g + j]
                p = term if p is None else p + term
            phase.append(p)
        shifted = jnp.concatenate(phase, axis=0)[s:s + CONV_ROWS, :]
        out = shifted if out is None else out + shifted
    c_ref[r0:r0 + CONV_ROWS, lanes] = out


def _conv_mlp_prompt_kernel(x_ref, p_ref, vec_ref, dwb_ref, pw1_ref, pw2_ref, w1_ref, w2_ref, gate_ref, proj_ref,
                            o_ref, cs_ref, ext_ref, c_ref, mid_ref, *, tile_m, tiles_per_stream):
    t = pl.program_id(0)
    ext_rows = tile_m + HALO
    slot = t % 2

    @pl.when(t % tiles_per_stream == 0)
    def _():
        ext_ref[0:HALO, :] = jnp.zeros((HALO, D_MODEL), F32)
        ext_ref[ext_rows:ext_rows + SUBLANES, :] = jnp.zeros((SUBLANES, D_MODEL), F32)

    @pl.when(t == 0)
    def _():
        mid_ref[1] = jnp.zeros((tile_m, D_MODEL), F32)

    x = x_ref[...]
    h = _rms(x, _row(vec_ref, V_NORM_MIX)).astype(BF16)

    def glu_piece(n):
        cols = slice(n * MXU_COLS, (n + 1) * MXU_COLS)
        gcols = slice(D_MODEL + n * MXU_COLS, D_MODEL + (n + 1) * MXU_COLS)
        a = _dot(h, pw1_ref[:, cols]) + vec_ref[V_PW1_B_LO:V_PW1_B_LO + 1, cols]
        g = _dot(h, pw1_ref[:, gcols]) + vec_ref[V_PW1_B_HI:V_PW1_B_HI + 1, cols]
        ext_ref[HALO:ext_rows, cols] = a * _sigmoid(g)

    glu_piece(0)
    conv_units = [functools.partial(_depthwise_unit, ext_ref, dwb_ref, c_ref, r * CONV_ROWS, l)
                  for l in range(D_MODEL // LANES) for r in range(tile_m // CONV_ROWS)]
    o_ref[...] = _mlp_ple_rows(mid_ref[1 - slot], p_ref[...], vec_ref, w1_ref, w2_ref, gate_ref, proj_ref, False,
                               filler=conv_units,
                               lead=[functools.partial(glu_piece, n) for n in range(1, D_MODEL // MXU_COLS)])
    cs_ref[0] = ext_ref[ext_rows - (CONV_WIDTH - 1):ext_rows, :]
    act = _layernorm_silu(c_ref[...] + _row(vec_ref, V_DW_B), _row(vec_ref, V_LN_G), _row(vec_ref, V_LN_B))
    mid_ref[slot] = x + _dot(act.astype(BF16), pw2_ref[...]) + _row(vec_ref, V_PW2_B)

    @pl.when(t >= 0)
    def _():
        ext_ref[0:HALO, :] = ext_ref[tile_m:ext_rows, :]


def _conv_mlp_prompt(x, p3d, layer, lw, tile_m):
    b, s, _ = x.shape
    tiles_per_stream = s // tile_m
    n_tiles = b * tiles_per_stream
    mixer_tile = lambda t: jnp.minimum(t, n_tiles - 1)
    mlp_tile = lambda t: jnp.maximum(t - 1, 0)
    y, cs = pl.pallas_call(
        functools.partial(_conv_mlp_prompt_kernel, tile_m=tile_m, tiles_per_stream=tiles_per_stream),
        grid=(n_tiles + 1,),
        in_specs=[
            pl.BlockSpec((tile_m, D_MODEL), lambda t: (mixer_tile(t), 0)),
            pl.BlockSpec((None, tile_m, PLE_DIM), lambda t: (layer, mlp_tile(t), 0)),
            _const_spec((VEC_ROWS, D_MODEL)),
            _const_spec((CONV_WIDTH, SUBLANES, D_MODEL)),
            _const_spec((D_MODEL, 2 * D_MODEL)),
            _const_spec((D_MODEL, D_MODEL)),
            _const_spec((D_MODEL, D_FF)),
            _const_spec((D_FF, D_MODEL)),
            _const_spec((D_MODEL, D_MODEL)),
            _const_spec((PLE_DIM, D_MODEL)),
        ],
        out_specs=[
            pl.BlockSpec((tile_m, D_MODEL), lambda t: (mlp_tile(t), 0)),
            pl.BlockSpec((1, CONV_WIDTH - 1, D_MODEL), lambda t: (mixer_tile(t) // tiles_per_stream, 0, 0)),
        ],
        out_shape=[
            jax.ShapeDtypeStruct((b * s, D_MODEL), F32),
            jax.ShapeDtypeStruct((b, CONV_WIDTH - 1, D_MODEL), F32),
        ],
        scratch_shapes=[
            pltpu.VMEM((tile_m + HALO + SUBLANES, D_MODEL), F32),
            pltpu.VMEM((tile_m, D_MODEL), F32),
            pltpu.VMEM((2, tile_m, D_MODEL), F32),
        ],
        compiler_params=pltpu.CompilerParams(
            dimension_semantics=("arbitrary",), vmem_limit_bytes=VMEM_LIMIT),
        name="conv_mlp_prompt",
    )(x.reshape(b * s, D_MODEL), p3d, lw["vec"], lw["dwb"], lw["pw1"], lw["pw2"],
      lw["w1"], lw["w2"], lw["gate"], lw["proj"])
    return y.reshape(b, s, D_MODEL), cs


def _conv_sample_kernel(x_ref, left_ref, vec_ref, dw_ref, pw1_ref, pw2_ref, o_ref, cs_ref, ext_ref, *, streams, steps):
    x = x_ref[...]
    ext_ref[:, 0:HALO - (CONV_WIDTH - 1), :] = jnp.zeros((streams, HALO - (CONV_WIDTH - 1), D_MODEL), F32)
    ext_ref[:, HALO - (CONV_WIDTH - 1):HALO, :] = left_ref[...]
    ext_ref[:, HALO:HALO + steps, :] = _glu(x, vec_ref, pw1_ref).reshape(streams, steps, D_MODEL)
    acc = jnp.broadcast_to(_row(vec_ref, V_DW_B).reshape(1, 1, D_MODEL), (streams, steps, D_MODEL))
    for k in range(CONV_WIDTH):
        lo = k + HALO - (CONV_WIDTH - 1)
        acc = acc + dw_ref[k:k + 1, :].reshape(1, 1, D_MODEL) * ext_ref[:, lo:lo + steps, :]
    act = _layernorm_silu(acc.reshape(streams * steps, D_MODEL), _row(vec_ref, V_LN_G), _row(vec_ref, V_LN_B))
    cs_ref[...] = ext_ref[:, HALO + steps - (CONV_WIDTH - 1):HALO + steps, :]
    o_ref[...] = x + _dot(act.astype(BF16), pw2_ref[...]) + _row(vec_ref, V_PW2_B)


def _conv_sample(x2d, left, lw, streams, steps):
    rows = streams * steps
    return pl.pallas_call(
        functools.partial(_conv_sample_kernel, streams=streams, steps=steps),
        grid=(1,),
        in_specs=[
            _const_spec((rows, D_MODEL)),
            _const_spec((streams, CONV_WIDTH - 1, D_MODEL)),
            _const_spec((VEC_ROWS, D_MODEL)),
            _const_spec((CONV_WIDTH + 1, D_MODEL)),
            _const_spec((D_MODEL, 2 * D_MODEL)),
            _const_spec((D_MODEL, D_MODEL)),
        ],
        out_specs=[
            pl.BlockSpec((rows, D_MODEL), lambda t: (0, 0)),
            pl.BlockSpec((streams, CONV_WIDTH - 1, D_MODEL), lambda t: (0, 0, 0)),
        ],
        out_shape=[
            jax.ShapeDtypeStruct((rows, D_MODEL), F32),
            jax.ShapeDtypeStruct((streams, CONV_WIDTH - 1, D_MODEL), F32),
        ],
        scratch_shapes=[pltpu.VMEM((streams, HALO + steps, D_MODEL), F32)],
        compiler_params=pltpu.CompilerParams(
            dimension_semantics=("arbitrary",), vmem_limit_bytes=VMEM_LIMIT),
        name="conv_sample",
    )(x2d, left, lw["vec"], lw["dw"], lw["pw1"], lw["pw2"])


def _attn_mlp_prompt_kernel(x_ref, p_ref, vec_ref, bias_ref, wq_ref, wkv_ref, wo_ref, w1_ref, w2_ref, gate_ref,
                            proj_ref, o_ref, ks_ref, vs_ref, q_ref, kb_ref, vb_ref, att_ref, mid_ref,
                            *, tile_m, tiles_per_stream, final):
    t = pl.program_id(0)
    slot = t % 2
    tile_in_stream = t % tiles_per_stream

    @pl.when(tile_in_stream == 0)
    def _():
        kb_ref[0:WINDOW, :] = jnp.zeros((WINDOW, KV_DIM), BF16)
        vb_ref[0:WINDOW, :] = jnp.zeros((WINDOW, KV_DIM), BF16)

    @pl.when(t == 0)
    def _():
        mid_ref[1] = jnp.zeros((tile_m, D_MODEL), F32)

    x = x_ref[...]
    h = _rms(x, _row(vec_ref, V_NORM_MIX)).astype(BF16)
    q_ref[...] = _dot(h, wq_ref[...]).astype(BF16)
    kv = _dot(h, wkv_ref[...])
    k_new = kv[:, :KV_DIM]
    v_new = kv[:, KV_DIM:]
    kb_ref[WINDOW:WINDOW + tile_m, :] = k_new.astype(BF16)
    vb_ref[WINDOW:WINDOW + tile_m, :] = v_new.astype(BF16)
    ks_ref[0] = k_new[tile_m - WINDOW:, :]
    vs_ref[0] = v_new[tile_m - WINDOW:, :]

    rows = GROUP * CHUNK
    pad = jnp.zeros((KEYS_EXT - BAND, KV_DIM), BF16)
    weights = {}

    def chunk_logits(c0):
        q = jnp.concatenate(
            [q_ref[c0:c0 + CHUNK, g * KV_DIM:(g + 1) * KV_DIM] for g in range(GROUP)], axis=0)
        qm = jnp.concatenate(
            [jnp.where(_kv_lane_mask(q.shape, kh), q, jnp.zeros_like(q)) for kh in range(N_KV_HEADS)], axis=0)
        kb = jnp.concatenate([kb_ref[c0:c0 + BAND, :], pad], axis=0)
        logits = lax.dot_general(qm, kb, (((1,), (1,)), ((), ())), preferred_element_type=F32)
        logits = logits + bias_ref[...]
        key_pos = tile_in_stream * tile_m + c0 - WINDOW + lax.broadcasted_iota(jnp.int32, (1, KEYS_EXT), 1)
        logits = jnp.where(key_pos >= 0, logits, NEG_INF)
        weights[c0] = jnp.exp(logits - jnp.max(logits, axis=-1, keepdims=True)).astype(BF16)

    def chunk_values(c0):
        e = weights.pop(c0)
        vb = vb_ref[c0:c0 + BAND, :]
        half = LANES // HEAD_DIM
        krow = lax.broadcasted_iota(jnp.int32, (KEYS_EXT, LANES), 0)
        klane = lax.broadcasted_iota(jnp.int32, (KEYS_EXT, LANES), 1)
        outs = []
        for pair in range(N_KV_HEADS // half):
            num = den = None
            for sub in range(half):
                kh = pair * half + sub
                in_head = (klane >= sub * HEAD_DIM) & (klane < (sub + 1) * HEAD_DIM)
                v_kh = jnp.concatenate([vb[:, pair * LANES:(pair + 1) * LANES], pad[:, 0:LANES]], axis=0)
                v_kh = jnp.where(in_head, v_kh, jnp.zeros_like(v_kh))
                ones = jnp.where(in_head & (krow <= BAND), 1.0, 0.0).astype(BF16)
                pv = _dot(e[kh * rows:(kh + 1) * rows, :], jnp.concatenate([v_kh, ones], axis=1))
                num = pv[:, 0:LANES] if num is None else num + pv[:, 0:LANES]
                den = pv[:, LANES:] if den is None else den + pv[:, LANES:]
            outs.append(num / den)
        out = jnp.concatenate(outs, axis=1)
        for g in range(GROUP):
            att_ref[c0:c0 + CHUNK, g * KV_DIM:(g + 1) * KV_DIM] = out[g * CHUNK:(g + 1) * CHUNK, :].astype(BF16)

    chunk_units = []
    for c0 in range(0, tile_m, CHUNK):
        chunk_units += [functools.partial(chunk_logits, c0), functools.partial(chunk_values, c0)]
    o_ref[...] = _mlp_ple_rows(mid_ref[1 - slot], p_ref[...], vec_ref, w1_ref, w2_ref, gate_ref, proj_ref, final,
                               filler=chunk_units)

    mid_ref[slot] = x + _dot(att_ref[...], wo_ref[...])

    @pl.when(t >= 0)
    def _():
        kb_ref[0:WINDOW, :] = kb_ref[tile_m:tile_m + WINDOW, :]
        vb_ref[0:WINDOW, :] = vb_ref[tile_m:tile_m + WINDOW, :]


def _attn_mlp_prompt(x, p3d, layer, lw, bias, tile_m, final):
    b, s, _ = x.shape
    tiles_per_stream = s // tile_m
    n_tiles = b * tiles_per_stream
    mixer_tile = lambda t: jnp.minimum(t, n_tiles - 1)
    mlp_tile = lambda t: jnp.maximum(t - 1, 0)
    state_spec = pl.BlockSpec((1, WINDOW, KV_DIM), lambda t: (mixer_tile(t) // tiles_per_stream, 0, 0))
    y, ks, vs = pl.pallas_call(
        functools.partial(_attn_mlp_prompt_kernel, tile_m=tile_m, tiles_per_stream=tiles_per_stream, final=final),
        grid=(n_tiles + 1,),
        in_specs=[
            pl.BlockSpec((tile_m, D_MODEL), lambda t: (mixer_tile(t), 0)),
            pl.BlockSpec((None, tile_m, PLE_DIM), lambda t: (layer, mlp_tile(t), 0)),
            _const_spec((VEC_ROWS, D_MODEL)),
            _const_spec(bias.shape),
            _const_spec((D_MODEL, D_MODEL)),
            _const_spec((D_MODEL, 2 * KV_DIM)),
            _const_spec((D_MODEL, D_MODEL)),
            _const_spec((D_MODEL, D_FF)),
            _const_spec((D_FF, D_MODEL)),
            _const_spec((D_MODEL, D_MODEL)),
            _const_spec((PLE_DIM, D_MODEL)),
        ],
        out_specs=[
            pl.BlockSpec((tile_m, D_MODEL), lambda t: (mlp_tile(t), 0)),
            state_spec,
            state_spec,
        ],
        out_shape=[
            jax.ShapeDtypeStruct((b * s, D_MODEL), F32),
            jax.ShapeDtypeStruct((b, WINDOW, KV_DIM), F32),
            jax.ShapeDtypeStruct((b, WINDOW, KV_DIM), F32),
        ],
        scratch_shapes=[
            pltpu.VMEM((tile_m, D_MODEL), BF16),
            pltpu.VMEM((tile_m + WINDOW, KV_DIM), BF16),
            pltpu.VMEM((tile_m + WINDOW, KV_DIM), BF16),
            pltpu.VMEM((tile_m, D_MODEL), BF16),
            pltpu.VMEM((2, tile_m, D_MODEL), F32),
        ],
        compiler_params=pltpu.CompilerParams(
            dimension_semantics=("arbitrary",), vmem_limit_bytes=VMEM_LIMIT),
        name="attn_mlp_prompt",
    )(x.reshape(b * s, D_MODEL), p3d, lw["vec"], bias, lw["wq"], lw["wkv"], lw["wo"],
      lw["w1"], lw["w2"], lw["gate"], lw["proj"])
    return y.reshape(b, s, D_MODEL), ks, vs


def _attn_sample_kernel(x_ref, kc_ref, vc_ref, vec_ref, bias_ref, wq_ref, wkv_ref, wo_ref,
                        o_ref, ks_ref, vs_ref, kf_ref, vf_ref, *, streams, steps):
    cache = kc_ref.shape[1]
    keys = cache + steps
    x = x_ref[...]
    h = _rms(x, _row(vec_ref, V_NORM_MIX)).astype(BF16)
    q = _dot(h, wq_ref[...]).reshape(streams, steps, D_MODEL)
    kv = _dot(h, wkv_ref[...])
    kf_ref[:, 0:cache, :] = kc_ref[...]
    vf_ref[:, 0:cache, :] = vc_ref[...]
    kf_ref[:, cache:keys, :] = kv[:, :KV_DIM].reshape(streams, steps, KV_DIM)
    vf_ref[:, cache:keys, :] = kv[:, KV_DIM:].reshape(streams, steps, KV_DIM)
    ks_ref[...] = kf_ref[:, steps:keys, :]
    vs_ref[...] = vf_ref[:, steps:keys, :]

    qs = jnp.concatenate([q[:, :, g * KV_DIM:(g + 1) * KV_DIM] for g in range(GROUP)], axis=1).astype(BF16)
    kf = kf_ref[...].astype(BF16)
    vf = vf_ref[...].astype(BF16)
    out = jnp.zeros((streams, GROUP * steps, KV_DIM), F32)
    for kh in range(N_KV_HEADS):
        k_kh = jnp.where(_kv_lane_mask(kf.shape, kh), kf, jnp.zeros_like(kf))
        logits = jnp.einsum("bqd,bkd->bqk", qs, k_kh, preferred_element_type=F32)
        logits = logits + bias_ref[kh, :, 0:keys][None]
        probs = _sink_softmax(logits, bias_ref[kh, :, keys:keys + 1][None])
        o_kh = jnp.einsum("bqk,bkd->bqd", probs.astype(BF16), vf, preferred_element_type=F32)
        out = jnp.where(_kv_lane_mask(out.shape, kh), o_kh, out)
    att = jnp.concatenate([out[:, g * steps:(g + 1) * steps, :] for g in range(GROUP)], axis=2)
    o_ref[...] = x + _dot(att.reshape(streams * steps, D_MODEL).astype(BF16), wo_ref[...])


def _attn_sample(x2d, k_left, v_left, lw, bias, streams, steps):
    rows = streams * steps
    cache = k_left.shape[1]
    return pl.pallas_call(
        functools.partial(_attn_sample_kernel, streams=streams, steps=steps),
        grid=(1,),
        in_specs=[
            _const_spec((rows, D_MODEL)),
            _const_spec(k_left.shape),
            _const_spec(v_left.shape),
            _const_spec((VEC_ROWS, D_MODEL)),
            _const_spec(bias.shape),
            _const_spec((D_MODEL, D_MODEL)),
            _const_spec((D_MODEL, 2 * KV_DIM)),
            _const_spec((D_MODEL, D_MODEL)),
        ],
        out_specs=[
            pl.BlockSpec((rows, D_MODEL), lambda t: (0, 0)),
            pl.BlockSpec((streams, cache, KV_DIM), lambda t: (0, 0, 0)),
            pl.BlockSpec((streams, cache, KV_DIM), lambda t: (0, 0, 0)),
        ],
        out_shape=[
            jax.ShapeDtypeStruct((rows, D_MODEL), F32),
            jax.ShapeDtypeStruct((streams, cache, KV_DIM), F32),
            jax.ShapeDtypeStruct((streams, cache, KV_DIM), F32),
        ],
        scratch_shapes=[
            pltpu.VMEM((streams, cache + steps, KV_DIM), F32),
            pltpu.VMEM((streams, cache + steps, KV_DIM), F32),
        ],
        compiler_params=pltpu.CompilerParams(
            dimension_semantics=("arbitrary",), vmem_limit_bytes=VMEM_LIMIT),
        name="attn_sample",
    )(x2d, k_left, v_left, lw["vec"], bias, lw["wq"], lw["wkv"], lw["wo"])


def _t5_bucket(rel):
    nb = NUM_BUCKETS // 2
    max_exact = nb // 2
    ret = (rel > 0).astype(jnp.int32) * nb
    n = jnp.abs(rel)
    nf = jnp.maximum(n, 1).astype(F32)
    large = max_exact + (jnp.log(nf / max_exact) / math.log(MAX_DISTANCE / max_exact)
                         * (nb - max_exact)).astype(jnp.int32)
    large = jnp.minimum(large, nb - 1)
    return ret + jnp.where(n < max_exact, n, large)


def _bias_table(rel_bias, sinks, q_len, k_len, offset):
    rel = jnp.arange(k_len, dtype=jnp.int32)[None, :] - offset - jnp.arange(q_len, dtype=jnp.int32)[:, None]
    onehot = (_t5_bucket(rel)[:, :, None] == jnp.arange(NUM_BUCKETS, dtype=jnp.int32)).astype(F32)
    b = jnp.einsum("qkn,nh->hqk", onehot, rel_bias.astype(F32), precision=lax.Precision.HIGHEST)
    b = b.reshape(N_KV_HEADS, GROUP * q_len, k_len)
    s = jnp.repeat(sinks.astype(F32).reshape(N_KV_HEADS, GROUP), q_len, axis=1)[:, :, None]
    pad = jnp.full((N_KV_HEADS, GROUP * q_len, KEYS_EXT - k_len - 1), NEG_INF, F32)
    return jnp.concatenate([b, s, pad], axis=2)


def _pack_vec(rows):
    zero = jnp.zeros((D_MODEL,), F32)
    return jnp.stack([rows[i].astype(F32) if i in rows else zero for i in range(VEC_ROWS)])


def kernel(x_prompt, x_sample, cache_conv, cache_k, cache_v, p_prompt, p_sample, rel_bias, norm_mix, norm_mlp, norm_ple, norm_final, conv_pw1, conv_pw1_b, conv_dw, conv_dw_b, conv_ln_g, conv_ln_b, conv_pw2, conv_pw2_b, attn_wq, attn_wk, attn_wv, attn_wo, attn_sinks, mlp_w1, mlp_w2, ple_proj, ple_gate):
    batch, seq, _ = x_prompt.shape
    streams, steps, _ = x_sample.shape
    cache = cache_k.shape[2]
    assert seq % TILE_M == 0 and TILE_M % CHUNK == 0 and TILE_M >= WINDOW and steps % SUBLANES == 0

    layers = []
    for i in range(DEPTH):
        j = i // 2
        rows = {V_NORM_MIX: norm_mix[i], V_NORM_MLP: norm_mlp[i], V_NORM_PLE: norm_ple[i], V_NORM_FINAL: norm_final}
        lw = {"w1": mlp_w1[i].astype(BF16), "w2": mlp_w2[i].astype(BF16),
              "gate": ple_gate[i].astype(BF16), "proj": ple_proj[i].astype(BF16)}
        if i % 2 == 0:
            rows.update({V_PW1_B_LO: conv_pw1_b[j, :D_MODEL], V_PW1_B_HI: conv_pw1_b[j, D_MODEL:],
                         V_DW_B: conv_dw_b[j], V_LN_G: conv_ln_g[j], V_LN_B: conv_ln_b[j], V_PW2_B: conv_pw2_b[j]})
            lw["pw1"] = conv_pw1[j].astype(BF16)
            lw["pw2"] = conv_pw2[j].astype(BF16)
            lw["dw"] = jnp.concatenate([conv_dw[j].astype(F32), jnp.zeros((1, D_MODEL), F32)], axis=0)
            lw["dwb"] = jnp.broadcast_to(conv_dw[j].astype(F32)[:, None, :], (CONV_WIDTH, SUBLANES, D_MODEL))
        else:
            wq = (attn_wq[j] * (HEAD_DIM ** -0.5)).reshape(D_MODEL, N_KV_HEADS, GROUP, HEAD_DIM).transpose(0, 2, 1, 3)
            lw["wq"] = wq.reshape(D_MODEL, D_MODEL).astype(BF16)
            lw["wkv"] = jnp.concatenate([attn_wk[j], attn_wv[j]], axis=1).astype(BF16)
            wo = attn_wo[j].reshape(N_KV_HEADS, GROUP, HEAD_DIM, D_MODEL).transpose(1, 0, 2, 3)
            lw["wo"] = wo.reshape(D_MODEL, D_MODEL).astype(BF16)
            lw["bias_prompt"] = _bias_table(rel_bias, attn_sinks[j], CHUNK, BAND, WINDOW).reshape(N_HEADS * CHUNK, 256)
            lw["bias_sample"] = _bias_table(rel_bias, attn_sinks[j], steps, cache + steps, cache)
        lw["vec"] = _pack_vec(rows)
        layers.append(lw)

    p_prompt2 = p_prompt.reshape(DEPTH, batch * seq, PLE_DIM)
    p_sample2 = p_sample.reshape(DEPTH, streams * steps, PLE_DIM)

    xp = x_prompt
    xs = x_sample.reshape(streams * steps, D_MODEL)
    conv_p, k_p, v_p, conv_s, k_s, v_s = [], [], [], [], [], []
    for i, lw in enumerate(layers):
        j = i // 2
        final = i == DEPTH - 1
        if i % 2 == 0:
            xp, cs = _conv_mlp_prompt(xp, p_prompt2, i, lw, TILE_M)
            conv_p.append(cs)
            xs, cs = _conv_sample(xs, cache_conv[j], lw, streams, steps)
            conv_s.append(cs)
        else:
            xp, ks, vs = _attn_mlp_prompt(xp, p_prompt2, i, lw, lw["bias_prompt"], TILE_M, final)
            k_p.append(ks)
            v_p.append(vs)
            xs, ks, vs = _attn_sample(xs, cache_k[j].reshape(streams, cache, KV_DIM),
                                      cache_v[j].reshape(streams, cache, KV_DIM), lw, lw["bias_sample"], streams, steps)
            k_s.append(ks)
            v_s.append(vs)
        xs = _mlp_ple(xs, p_sample2, i, lw, streams * steps, final)

    def heads(ts, n):
        return jnp.stack(ts).reshape(len(ts), n, -1, N_KV_HEADS, HEAD_DIM)

    return (xp, xs.reshape(streams, steps, D_MODEL), jnp.stack(conv_p), heads(k_p, batch), heads(v_p, batch),
            jnp.stack(conv_s), heads(k_s, streams), heads(v_s, streams))
```

```python
import functools
import math

import jax
import jax.numpy as jnp
from jax import lax
from jax.experimental import pallas as pl
from jax.experimental.pallas import tpu as pltpu

D_MODEL = 1024
DEPTH = 4
CHUNK = 64
CONV_WIDTH = 31
WINDOW = 128
N_HEADS = 16
N_KV_HEADS = 4
GROUP = N_HEADS // N_KV_HEADS
HEAD_DIM = 64
KV_DIM = N_KV_HEADS * HEAD_DIM
D_FF = 4 * D_MODEL
PLE_DIM = 256
NUM_BUCKETS = 32
MAX_DISTANCE = 128
EPS = 1e-6
NEG_INF = -1e30

BAND = WINDOW + CHUNK
KEYS_EXT = 256
HALO = 32
SUBLANES = 8
LANES = 128
CONV_ROWS = 64
FF_CHUNK = 1024
MXU_COLS = 256
LEAD_WEIGHT = 2
TAIL_PARTS = 2
TILE_M = 512
VMEM_LIMIT = 60 * 1024 * 1024

V_NORM_MIX, V_PW1_B_LO, V_PW1_B_HI, V_DW_B, V_LN_G, V_LN_B, V_PW2_B, V_NORM_MLP, V_NORM_PLE, V_NORM_FINAL = range(10)
VEC_ROWS = 16

F32 = jnp.float32
BF16 = jnp.bfloat16


def _dot(a, b):
    return jnp.dot(a, b, preferred_element_type=F32)


def _sigmoid(x):
    return 1.0 / (1.0 + jnp.exp(-x))


def _rms(x, g):
    return x * lax.rsqrt(jnp.mean(x * x, axis=-1, keepdims=True) + EPS) * g


def _row(vec_ref, i):
    return vec_ref[i:i + 1, :]


def _layernorm_silu(c, g, b):
    mu = jnp.mean(c, axis=-1, keepdims=True)
    d = c - mu
    var = jnp.mean(d * d, axis=-1, keepdims=True)
    y = d * lax.rsqrt(var + EPS) * g + b
    return y * _sigmoid(y)


def _sink_softmax(logits, sink):
    m = jnp.maximum(jnp.max(logits, axis=-1, keepdims=True), sink)
    e = jnp.exp(logits - m)
    return e / (jnp.sum(e, axis=-1, keepdims=True) + jnp.exp(sink - m))


def _kv_lane_mask(shape, kh):
    lane = lax.broadcasted_iota(jnp.int32, shape, len(shape) - 1)
    return (lane >= kh * HEAD_DIM) & (lane < (kh + 1) * HEAD_DIM)


def _mlp_ple_rows(x, p, vec_ref, w1_ref, w2_ref, gate_ref, proj_ref, final, filler=(), lead=()):
    filler = list(filler)
    pieces = D_MODEL // MXU_COLS
    n_pieces = (D_FF // FF_CHUNK) * (FF_CHUNK // MXU_COLS + pieces) + LEAD_WEIGHT * len(lead)
    done = [0, 0]

    def fill(weight=1):
        done[0] += weight
        while done[1] < len(filler) * done[0] // n_pieces:
            filler[done[1]]()
            done[1] += 1

    for piece in lead:
        piece()
        fill(LEAD_WEIGHT)
    h = _rms(x, _row(vec_ref, V_NORM_MLP)).astype(BF16)
    xs = [x[:, n * MXU_COLS:(n + 1) * MXU_COLS] for n in range(pieces)]
    for c in range(D_FF // FF_CHUNK):
        hid = []
        for n in range(FF_CHUNK // MXU_COLS):
            lo = c * FF_CHUNK + n * MXU_COLS
            a = _dot(h, w1_ref[:, lo:lo + MXU_COLS])
            hid.append(jnp.square(jnp.maximum(a, 0.0)).astype(BF16))
            fill()
        hid = jnp.concatenate(hid, axis=1)
        for n in range(pieces):
            xs[n] = xs[n] + _dot(hid, w2_ref[c * FF_CHUNK:(c + 1) * FF_CHUNK, n * MXU_COLS:(n + 1) * MXU_COLS])
            fill()
    assert done == [n_pieces, len(filler)]
    x = jnp.concatenate(xs, axis=1)
    hg = _rms(x, _row(vec_ref, V_NORM_PLE)).astype(BF16)
    gate = _sigmoid(_dot(hg, gate_ref[...]))
    x = x + _dot(p.astype(BF16), proj_ref[...]) * gate
    if final:
        x = _rms(x, _row(vec_ref, V_NORM_FINAL))
    return x


def _mlp_ple_kernel(x_ref, p_ref, vec_ref, w1_ref, w2_ref, gate_ref, proj_ref, o_ref, *, final):
    o_ref[...] = _mlp_ple_rows(x_ref[...], p_ref[...], vec_ref, w1_ref, w2_ref, gate_ref, proj_ref, final)


def _const_spec(shape):
    nd = len(shape)
    return pl.BlockSpec(shape, lambda *_: (0,) * nd, pipeline_mode=pl.Buffered(1))


def _mlp_ple(x2d, p4d, layer, lw, tile_m, final):
    rows = x2d.shape[0]
    return pl.pallas_call(
        functools.partial(_mlp_ple_kernel, final=final),
        grid=(rows // tile_m,),
        in_specs=[
            pl.BlockSpec((tile_m, D_MODEL), lambda t: (t, 0)),
            pl.BlockSpec((None, tile_m, PLE_DIM), lambda t: (layer, t, 0)),
            _const_spec((VEC_ROWS, D_MODEL)),
            _const_spec((D_MODEL, D_FF)),
            _const_spec((D_FF, D_MODEL)),
            _const_spec((D_MODEL, D_MODEL)),
            _const_spec((PLE_DIM, D_MODEL)),
        ],
        out_specs=pl.BlockSpec((tile_m, D_MODEL), lambda t: (t, 0)),
        out_shape=jax.ShapeDtypeStruct((rows, D_MODEL), F32),
        compiler_params=pltpu.CompilerParams(
            dimension_semantics=("arbitrary",), vmem_limit_bytes=VMEM_LIMIT),
        name=f"mlp_ple_l{layer}_r{rows}",
    )(x2d, p4d, lw["vec"], lw["w1"], lw["w2"], lw["gate"], lw["proj"])


def _glu(x, vec_ref, pw1_ref):
    h = _rms(x, _row(vec_ref, V_NORM_MIX)).astype(BF16)
    a = _dot(h, pw1_ref[:, :D_MODEL]) + _row(vec_ref, V_PW1_B_LO)
    g = _dot(h, pw1_ref[:, D_MODEL:]) + _row(vec_ref, V_PW1_B_HI)
    return a * _sigmoid(g)


def _depthwise_unit(ext_ref, dwb_ref, c_ref, r0, l, below):
    first = HALO - (CONV_WIDTH - 1)
    groups = CONV_ROWS // SUBLANES
    spans = -(-(first + CONV_WIDTH) // SUBLANES)
    lanes = slice(l * LANES, (l + 1) * LANES)
    has_below = bool(below)
    win = [ext_ref[r0 + SUBLANES * g:r0 + SUBLANES * (g + 1), lanes]
           for g in range(groups + spans - (1 if has_below else 0))]
    out = None
    for s in range(SUBLANES):
        taps = [(j, SUBLANES * j + s - first) for j in range(spans)
                if 0 <= SUBLANES * j + s - first < CONV_WIDTH]
        phase = []
        for g in range(groups + (1 if s and not has_below else 0)):
            p = None
            for j, k in taps:
                term = dwb_ref[k, :, lanes] * win[g + j]
                p = term if p is None else p + term
            phase.append(p)
        if s:
            if has_below:
                phase.append(below[s])
            below[s] = phase[0]
        shifted = jnp.concatenate(phase, axis=0)[s:s + CONV_ROWS, :]
        out = shifted if out is None else out + shifted
    c_ref[r0:r0 + CONV_ROWS, lanes] = out


def _conv_mlp_prompt_kernel(x_ref, p_ref, vec_ref, dwb_ref, pw1_ref, pw2_ref, w1_ref, w2_ref, gate_ref, proj_ref,
                            o_ref, cs_ref, ext_ref, c_ref, mid_ref, *, tile_m, tiles_per_stream):
    t = pl.program_id(0)
    ext_rows = tile_m + HALO
    slot = t % 2

    @pl.when(t % tiles_per_stream == 0)
    def _():
        ext_ref[0:HALO, :] = jnp.zeros((HALO, D_MODEL), F32)
        ext_ref[ext_rows:ext_rows + SUBLANES, :] = jnp.zeros((SUBLANES, D_MODEL), F32)

    @pl.when(t == 0)
    def _():
        mid_ref[1] = jnp.zeros((tile_m, D_MODEL), F32)

    x = x_ref[...]
    h = _rms(x, _row(vec_ref, V_NORM_MIX)).astype(BF16)

    def glu_piece(n):
        cols = slice(n * MXU_COLS, (n + 1) * MXU_COLS)
        gcols = slice(D_MODEL + n * MXU_COLS, D_MODEL + (n + 1) * MXU_COLS)
        a = _dot(h, pw1_ref[:, cols]) + vec_ref[V_PW1_B_LO:V_PW1_B_LO + 1, cols]
        g = _dot(h, pw1_ref[:, gcols]) + vec_ref[V_PW1_B_HI:V_PW1_B_HI + 1, cols]
        ext_ref[HALO:ext_rows, cols] = a * _sigmoid(g)

    glu_piece(0)
    conv_units = []
    for l in range(D_MODEL // LANES):
        below = {}
        conv_units += [functools.partial(_depthwise_unit, ext_ref, dwb_ref, c_ref, r * CONV_ROWS, l, below)
                       for r in reversed(range(tile_m // CONV_ROWS))]
    o_ref[...] = _mlp_ple_rows(mid_ref[1 - slot], p_ref[...], vec_ref, w1_ref, w2_ref, gate_ref, proj_ref, False,
                               filler=conv_units,
                               lead=[functools.partial(glu_piece, n) for n in range(1, D_MODEL // MXU_COLS)])
    cs_ref[0] = ext_ref[ext_rows - (CONV_WIDTH - 1):ext_rows, :]
    for part in range(TAIL_PARTS):
        rows = slice(part * tile_m // TAIL_PARTS, (part + 1) * tile_m // TAIL_PARTS)
        act = _layernorm_silu(c_ref[rows, :] + _row(vec_ref, V_DW_B), _row(vec_ref, V_LN_G), _row(vec_ref, V_LN_B))
        mid_ref[slot, rows, :] = x[rows, :] + _dot(act.astype(BF16), pw2_ref[...]) + _row(vec_ref, V_PW2_B)

    @pl.when(t >= 0)
    def _():
        ext_ref[0:HALO, :] = ext_ref[tile_m:ext_rows, :]


def _conv_mlp_prompt(x, p3d, layer, lw, tile_m):
    b, s, _ = x.shape
    tiles_per_stream = s // tile_m
    n_tiles = b * tiles_per_stream
    mixer_tile = lambda t: jnp.minimum(t, n_tiles - 1)
    mlp_tile = lambda t: jnp.maximum(t - 1, 0)
    y, cs = pl.pallas_call(
        functools.partial(_conv_mlp_prompt_kernel, tile_m=tile_m, tiles_per_stream=tiles_per_stream),
        grid=(n_tiles + 1,),
        in_specs=[
            pl.BlockSpec((tile_m, D_MODEL), lambda t: (mixer_tile(t), 0)),
            pl.BlockSpec((None, tile_m, PLE_DIM), lambda t: (layer, mlp_tile(t), 0)),
            _const_spec((VEC_ROWS, D_MODEL)),
            _const_spec((CONV_WIDTH, SUBLANES, D_MODEL)),
            _const_spec((D_MODEL, 2 * D_MODEL)),
            _const_spec((D_MODEL, D_MODEL)),
            _const_spec((D_MODEL, D_FF)),
            _const_spec((D_FF, D_MODEL)),
            _const_spec((D_MODEL, D_MODEL)),
            _const_spec((PLE_DIM, D_MODEL)),
        ],
        out_specs=[
            pl.BlockSpec((tile_m, D_MODEL), lambda t: (mlp_tile(t), 0)),
            pl.BlockSpec((1, CONV_WIDTH - 1, D_MODEL), lambda t: (mixer_tile(t) // tiles_per_stream, 0, 0)),
        ],
        out_shape=[
            jax.ShapeDtypeStruct((b * s, D_MODEL), F32),
            jax.ShapeDtypeStruct((b, CONV_WIDTH - 1, D_MODEL), F32),
        ],
        scratch_shapes=[
            pltpu.VMEM((tile_m + HALO + SUBLANES, D_MODEL), F32),
            pltpu.VMEM((tile_m, D_MODEL), F32),
            pltpu.VMEM((2, tile_m, D_MODEL), F32),
        ],
        compiler_params=pltpu.CompilerParams(
            dimension_semantics=("arbitrary",), vmem_limit_bytes=VMEM_LIMIT),
        name="conv_mlp_prompt",
    )(x.reshape(b * s, D_MODEL), p3d, lw["vec"], lw["dwb"], lw["pw1"], lw["pw2"],
      lw["w1"], lw["w2"], lw["gate"], lw["proj"])
    return y.reshape(b, s, D_MODEL), cs


def _conv_sample_kernel(x_ref, left_ref, vec_ref, dw_ref, pw1_ref, pw2_ref, o_ref, cs_ref, ext_ref, *, streams, steps):
    x = x_ref[...]
    ext_ref[:, 0:HALO - (CONV_WIDTH - 1), :] = jnp.zeros((streams, HALO - (CONV_WIDTH - 1), D_MODEL), F32)
    ext_ref[:, HALO - (CONV_WIDTH - 1):HALO, :] = left_ref[...]
    ext_ref[:, HALO:HALO + steps, :] = _glu(x, vec_ref, pw1_ref).reshape(streams, steps, D_MODEL)
    acc = jnp.broadcast_to(_row(vec_ref, V_DW_B).reshape(1, 1, D_MODEL), (streams, steps, D_MODEL))
    for k in range(CONV_WIDTH):
        lo = k + HALO - (CONV_WIDTH - 1)
        acc = acc + dw_ref[k:k + 1, :].reshape(1, 1, D_MODEL) * ext_ref[:, lo:lo + steps, :]
    act = _layernorm_silu(acc.reshape(streams * steps, D_MODEL), _row(vec_ref, V_LN_G), _row(vec_ref, V_LN_B))
    cs_ref[...] = ext_ref[:, HALO + steps - (CONV_WIDTH - 1):HALO + steps, :]
    o_ref[...] = x + _dot(act.astype(BF16), pw2_ref[...]) + _row(vec_ref, V_PW2_B)


def _conv_sample(x2d, left, lw, streams, steps):
    rows = streams * steps
    return pl.pallas_call(
        functools.partial(_conv_sample_kernel, streams=streams, steps=steps),
        grid=(1,),
        in_specs=[
            _const_spec((rows, D_MODEL)),
            _const_spec((streams, CONV_WIDTH - 1, D_MODEL)),
            _const_spec((VEC_ROWS, D_MODEL)),
            _const_spec((CONV_WIDTH + 1, D_MODEL)),
            _const_spec((D_MODEL, 2 * D_MODEL)),
            _const_spec((D_MODEL, D_MODEL)),
        ],
        out_specs=[
            pl.BlockSpec((rows, D_MODEL), lambda t: (0, 0)),
            pl.BlockSpec((streams, CONV_WIDTH - 1, D_MODEL), lambda t: (0, 0, 0)),
        ],
        out_shape=[
            jax.ShapeDtypeStruct((rows, D_MODEL), F32),
            jax.ShapeDtypeStruct((streams, CONV_WIDTH - 1, D_MODEL), F32),
        ],
        scratch_shapes=[pltpu.VMEM((streams, HALO + steps, D_MODEL), F32)],
        compiler_params=pltpu.CompilerParams(
            dimension_semantics=("arbitrary",), vmem_limit_bytes=VMEM_LIMIT),
        name="conv_sample",
    )(x2d, left, lw["vec"], lw["dw"], lw["pw1"], lw["pw2"])


def _attn_mlp_prompt_kernel(x_ref, p_ref, vec_ref, bias_ref, wq_ref, wkv_ref, wo_ref, w1_ref, w2_ref, gate_ref,
                            proj_ref, o_ref, ks_ref, vs_ref, q_ref, kb_ref, vb_ref, att_ref, mid_ref,
                            *, tile_m, tiles_per_stream, final):
    t = pl.program_id(0)
    slot = t % 2
    tile_in_stream = t % tiles_per_stream

    @pl.when(tile_in_stream == 0)
    def _():
        kb_ref[0:WINDOW, :] = jnp.zeros((WINDOW, KV_DIM), BF16)
        vb_ref[0:WINDOW, :] = jnp.zeros((WINDOW, KV_DIM), BF16)

    @pl.when(t == 0)
    def _():
        mid_ref[1] = jnp.zeros((tile_m, D_MODEL), F32)

    x = x_ref[...]
    h = _rms(x, _row(vec_ref, V_NORM_MIX)).astype(BF16)
    q_ref[...] = _dot(h, wq_ref[...]).astype(BF16)
    kv = _dot(h, wkv_ref[...])
    k_new = kv[:, :KV_DIM]
    v_new = kv[:, KV_DIM:]
    kb_ref[WINDOW:WINDOW + tile_m, :] = k_new.astype(BF16)
    vb_ref[WINDOW:WINDOW + tile_m, :] = v_new.astype(BF16)
    ks_ref[0] = k_new[tile_m - WINDOW:, :]
    vs_ref[0] = v_new[tile_m - WINDOW:, :]

    rows = GROUP * CHUNK
    pad = jnp.zeros((KEYS_EXT - BAND, KV_DIM), BF16)
    weights = {}

    def chunk_logits(c0):
        q = jnp.concatenate(
            [q_ref[c0:c0 + CHUNK, g * KV_DIM:(g + 1) * KV_DIM] for g in range(GROUP)], axis=0)
        qm = jnp.concatenate(
            [jnp.where(_kv_lane_mask(q.shape, kh), q, jnp.zeros_like(q)) for kh in range(N_KV_HEADS)], axis=0)
        kb = jnp.concatenate([kb_ref[c0:c0 + BAND, :], pad], axis=0)
        logits = lax.dot_general(qm, kb, (((1,), (1,)), ((), ())), preferred_element_type=F32)
        logits = logits + bias_ref[...]
        key_pos = tile_in_stream * tile_m + c0 - WINDOW + lax.broadcasted_iota(jnp.int32, (1, KEYS_EXT), 1)
        logits = jnp.where(key_pos >= 0, logits, NEG_INF)
        weights[c0] = jnp.exp(logits - jnp.max(logits, axis=-1, keepdims=True)).astype(BF16)

    def chunk_values(c0):
        e = weights.pop(c0)
        vb = vb_ref[c0:c0 + BAND, :]
        half = LANES // HEAD_DIM
        krow = lax.broadcasted_iota(jnp.int32, (KEYS_EXT, LANES), 0)
        klane = lax.broadcasted_iota(jnp.int32, (KEYS_EXT, LANES), 1)
        outs = []
        for pair in range(N_KV_HEADS // half):
            num = den = None
            for sub in range(half):
                kh = pair * half + sub
                in_head = (klane >= sub * HEAD_DIM) & (klane < (sub + 1) * HEAD_DIM)
                v_kh = jnp.concatenate([vb[:, pair * LANES:(pair + 1) * LANES], pad[:, 0:LANES]], axis=0)
                v_kh = jnp.where(in_head, v_kh, jnp.zeros_like(v_kh))
                ones = jnp.where(in_head & (krow <= BAND), 1.0, 0.0).astype(BF16)
                pv = _dot(e[kh * rows:(kh + 1) * rows, :], jnp.concatenate([v_kh, ones], axis=1))
                num = pv[:, 0:LANES] if num is None else num + pv[:, 0:LANES]
                den = pv[:, LANES:] if den is None else den + pv[:, LANES:]
            outs.append(num / den)
        out = jnp.concatenate(outs, axis=1)
        for g in range(GROUP):
            att_ref[c0:c0 + CHUNK, g * KV_DIM:(g + 1) * KV_DIM] = out[g * CHUNK:(g + 1) * CHUNK, :].astype(BF16)

    chunk_units = []
    for c0 in range(0, tile_m, CHUNK):
        chunk_units += [functools.partial(chunk_logits, c0), functools.partial(chunk_values, c0)]
    o_ref[...] = _mlp_ple_rows(mid_ref[1 - slot], p_ref[...], vec_ref, w1_ref, w2_ref, gate_ref, proj_ref, final,
                               filler=chunk_units)

    mid_ref[slot] = x + _dot(att_ref[...], wo_ref[...])

    @pl.when(t >= 0)
    def _():
        kb_ref[0:WINDOW, :] = kb_ref[tile_m:tile_m + WINDOW, :]
        vb_ref[0:WINDOW, :] = vb_ref[tile_m:tile_m + WINDOW, :]


def _attn_mlp_prompt(x, p3d, layer, lw, bias, tile_m, final):
    b, s, _ = x.shape
    tiles_per_stream = s // tile_m
    n_tiles = b * tiles_per_stream
    mixer_tile = lambda t: jnp.minimum(t, n_tiles - 1)
    mlp_tile = lambda t: jnp.maximum(t - 1, 0)
    state_spec = pl.BlockSpec((1, WINDOW, KV_DIM), lambda t: (mixer_tile(t) // tiles_per_stream, 0, 0))
    y, ks, vs = pl.pallas_call(
        functools.partial(_attn_mlp_prompt_kernel, tile_m=tile_m, tiles_per_stream=tiles_per_stream, final=final),
        grid=(n_tiles + 1,),
        in_specs=[
            pl.BlockSpec((tile_m, D_MODEL), lambda t: (mixer_tile(t), 0)),
            pl.BlockSpec((None, tile_m, PLE_DIM), lambda t: (layer, mlp_tile(t), 0)),
            _const_spec((VEC_ROWS, D_MODEL)),
            _const_spec(bias.shape),
            _const_spec((D_MODEL, D_MODEL)),
            _const_spec((D_MODEL, 2 * KV_DIM)),
            _const_spec((D_MODEL, D_MODEL)),
            _const_spec((D_MODEL, D_FF)),
            _const_spec((D_FF, D_MODEL)),
            _const_spec((D_MODEL, D_MODEL)),
            _const_spec((PLE_DIM, D_MODEL)),
        ],
        out_specs=[
            pl.BlockSpec((tile_m, D_MODEL), lambda t: (mlp_tile(t), 0)),
            state_spec,
            state_spec,
        ],
        out_shape=[
            jax.ShapeDtypeStruct((b * s, D_MODEL), F32),
            jax.ShapeDtypeStruct((b, WINDOW, KV_DIM), F32),
            jax.ShapeDtypeStruct((b, WINDOW, KV_DIM), F32),
        ],
        scratch_shapes=[
            pltpu.VMEM((tile_m, D_MODEL), BF16),
            pltpu.VMEM((tile_m + WINDOW, KV_DIM), BF16),
            pltpu.VMEM((tile_m + WINDOW, KV_DIM), BF16),
            pltpu.VMEM((tile_m, D_MODEL), BF16),
            pltpu.VMEM((2, tile_m, D_MODEL), F32),
        ],
        compiler_params=pltpu.CompilerParams(
            dimension_semantics=("arbitrary",), vmem_limit_bytes=VMEM_LIMIT),
        name="attn_mlp_prompt",
    )(x.reshape(b * s, D_MODEL), p3d, lw["vec"], bias, lw["wq"], lw["wkv"], lw["wo"],
      lw["w1"], lw["w2"], lw["gate"], lw["proj"])
    return y.reshape(b, s, D_MODEL), ks, vs


def _attn_sample_kernel(x_ref, kc_ref, vc_ref, vec_ref, bias_ref, wq_ref, wkv_ref, wo_ref,
                        o_ref, ks_ref, vs_ref, kf_ref, vf_ref, *, streams, steps):
    cache = kc_ref.shape[1]
    keys = cache + steps
    x = x_ref[...]
    h = _rms(x, _row(vec_ref, V_NORM_MIX)).astype(BF16)
    q = _dot(h, wq_ref[...]).reshape(streams, steps, D_MODEL)
    kv = _dot(h, wkv_ref[...])
    kf_ref[:, 0:cache, :] = kc_ref[...]
    vf_ref[:, 0:cache, :] = vc_ref[...]
    kf_ref[:, cache:keys, :] = kv[:, :KV_DIM].reshape(streams, steps, KV_DIM)
    vf_ref[:, cache:keys, :] = kv[:, KV_DIM:].reshape(streams, steps, KV_DIM)
    ks_ref[...] = kf_ref[:, steps:keys, :]
    vs_ref[...] = vf_ref[:, steps:keys, :]

    qs = jnp.concatenate([q[:, :, g * KV_DIM:(g + 1) * KV_DIM] for g in range(GROUP)], axis=1).astype(BF16)
    kf = kf_ref[...].astype(BF16)
    vf = vf_ref[...].astype(BF16)
    out = jnp.zeros((streams, GROUP * steps, KV_DIM), F32)
    for kh in range(N_KV_HEADS):
        k_kh = jnp.where(_kv_lane_mask(kf.shape, kh), kf, jnp.zeros_like(kf))
        logits = jnp.einsum("bqd,bkd->bqk", qs, k_kh, preferred_element_type=F32)
        logits = logits + bias_ref[kh, :, 0:keys][None]
        probs = _sink_softmax(logits, bias_ref[kh, :, keys:keys + 1][None])
        o_kh = jnp.einsum("bqk,bkd->bqd", probs.astype(BF16), vf, preferred_element_type=F32)
        out = jnp.where(_kv_lane_mask(out.shape, kh), o_kh, out)
    att = jnp.concatenate([out[:, g * steps:(g + 1) * steps, :] for g in range(GROUP)], axis=2)
    o_ref[...] = x + _dot(att.reshape(streams * steps, D_MODEL).astype(BF16), wo_ref[...])


def _attn_sample(x2d, k_left, v_left, lw, bias, streams, steps):
    rows = streams * steps
    cache = k_left.shape[1]
    return pl.pallas_call(
        functools.partial(_attn_sample_kernel, streams=streams, steps=steps),
        grid=(1,),
        in_specs=[
            _const_spec((rows, D_MODEL)),
            _const_spec(k_left.shape),
            _const_spec(v_left.shape),
            _const_spec((VEC_ROWS, D_MODEL)),
            _const_spec(bias.shape),
            _const_spec((D_MODEL, D_MODEL)),
            _const_spec((D_MODEL, 2 * KV_DIM)),
            _const_spec((D_MODEL, D_MODEL)),
        ],
        out_specs=[
            pl.BlockSpec((rows, D_MODEL), lambda t: (0, 0)),
            pl.BlockSpec((streams, cache, KV_DIM), lambda t: (0, 0, 0)),
            pl.BlockSpec((streams, cache, KV_DIM), lambda t: (0, 0, 0)),
        ],
        out_shape=[
            jax.ShapeDtypeStruct((rows, D_MODEL), F32),
            jax.ShapeDtypeStruct((streams, cache, KV_DIM), F32),
            jax.ShapeDtypeStruct((streams, cache, KV_DIM), F32),
        ],
        scratch_shapes=[
            pltpu.VMEM((streams, cache + steps, KV_DIM), F32),
            pltpu.VMEM((streams, cache + steps, KV_DIM), F32),
        ],
        compiler_params=pltpu.CompilerParams(
            dimension_semantics=("arbitrary",), vmem_limit_bytes=VMEM_LIMIT),
        name="attn_sample",
    )(x2d, k_left, v_left, lw["vec"], bias, lw["wq"], lw["wkv"], lw["wo"])


def _t5_bucket(rel):
    nb = NUM_BUCKETS // 2
    max_exact = nb // 2
    ret = (rel > 0).astype(jnp.int32) * nb
    n = jnp.abs(rel)
    nf = jnp.maximum(n, 1).astype(F32)
    large = max_exact + (jnp.log(nf / max_exact) / math.log(MAX_DISTANCE / max_exact)
                         * (nb - max_exact)).astype(jnp.int32)
    large = jnp.minimum(large, nb - 1)
    return ret + jnp.where(n < max_exact, n, large)


def _bias_table(rel_bias, sinks, q_len, k_len, offset):
    rel = jnp.arange(k_len, dtype=jnp.int32)[None, :] - offset - jnp.arange(q_len, dtype=jnp.int32)[:, None]
    onehot = (_t5_bucket(rel)[:, :, None] == jnp.arange(NUM_BUCKETS, dtype=jnp.int32)).astype(F32)
    b = jnp.einsum("qkn,nh->hqk", onehot, rel_bias.astype(F32), precision=lax.Precision.HIGHEST)
    b = b.reshape(N_KV_HEADS, GROUP * q_len, k_len)
    s = jnp.repeat(sinks.astype(F32).reshape(N_KV_HEADS, GROUP), q_len, axis=1)[:, :, None]
    pad = jnp.full((N_KV_HEADS, GROUP * q_len, KEYS_EXT - k_len - 1), NEG_INF, F32)
    return jnp.concatenate([b, s, pad], axis=2)


def _pack_vec(rows):
    zero = jnp.zeros((D_MODEL,), F32)
    return jnp.stack([rows[i].astype(F32) if i in rows else zero for i in range(VEC_ROWS)])


def kernel(x_prompt, x_sample, cache_conv, cache_k, cache_v, p_prompt, p_sample, rel_bias, norm_mix, norm_mlp, norm_ple, norm_final, conv_pw1, conv_pw1_b, conv_dw, conv_dw_b, conv_ln_g, conv_ln_b, conv_pw2, conv_pw2_b, attn_wq, attn_wk, attn_wv, attn_wo, attn_sinks, mlp_w1, mlp_w2, ple_proj, ple_gate):
    batch, seq, _ = x_prompt.shape
    streams, steps, _ = x_sample.shape
    cache = cache_k.shape[2]
    assert seq % TILE_M == 0 and TILE_M % CHUNK == 0 and TILE_M >= WINDOW and steps % SUBLANES == 0

    layers = []
    for i in range(DEPTH):
        j = i // 2
        rows = {V_NORM_MIX: norm_mix[i], V_NORM_MLP: norm_mlp[i], V_NORM_PLE: norm_ple[i], V_NORM_FINAL: norm_final}
        lw = {"w1": mlp_w1[i].astype(BF16), "w2": mlp_w2[i].astype(BF16),
              "gate": ple_gate[i].astype(BF16), "proj": ple_proj[i].astype(BF16)}
        if i % 2 == 0:
            rows.update({V_PW1_B_LO: conv_pw1_b[j, :D_MODEL], V_PW1_B_HI: conv_pw1_b[j, D_MODEL:],
                         V_DW_B: conv_dw_b[j], V_LN_G: conv_ln_g[j], V_LN_B: conv_ln_b[j], V_PW2_B: conv_pw2_b[j]})
            lw["pw1"] = conv_pw1[j].astype(BF16)
            lw["pw2"] = conv_pw2[j].astype(BF16)
            lw["dw"] = jnp.concatenate([conv_dw[j].astype(F32), jnp.zeros((1, D_MODEL), F32)], axis=0)
            lw["dwb"] = jnp.broadcast_to(conv_dw[j].astype(F32)[:, None, :], (CONV_WIDTH, SUBLANES, D_MODEL))
        else:
            wq = (attn_wq[j] * (HEAD_DIM ** -0.5)).reshape(D_MODEL, N_KV_HEADS, GROUP, HEAD_DIM).transpose(0, 2, 1, 3)
            lw["wq"] = wq.reshape(D_MODEL, D_MODEL).astype(BF16)
            lw["wkv"] = jnp.concatenate([attn_wk[j], attn_wv[j]], axis=1).astype(BF16)
            wo = attn_wo[j].reshape(N_KV_HEADS, GROUP, HEAD_DIM, D_MODEL).transpose(1, 0, 2, 3)
            lw["wo"] = wo.reshape(D_MODEL, D_MODEL).astype(BF16)
            lw["bias_prompt"] = _bias_table(rel_bias, attn_sinks[j], CHUNK, BAND, WINDOW).reshape(N_HEADS * CHUNK, 256)
            lw["bias_sample"] = _bias_table(rel_bias, attn_sinks[j], steps, cache + steps, cache)
        lw["vec"] = _pack_vec(rows)
        layers.append(lw)

    p_prompt2 = p_prompt.reshape(DEPTH, batch * seq, PLE_DIM)
    p_sample2 = p_sample.reshape(DEPTH, streams * steps, PLE_DIM)

    xp = x_prompt
    xs = x_sample.reshape(streams * steps, D_MODEL)
    conv_p, k_p, v_p, conv_s, k_s, v_s = [], [], [], [], [], []
    for i, lw in enumerate(layers):
        j = i // 2
        final = i == DEPTH - 1
        if i % 2 == 0:
            xp, cs = _conv_mlp_prompt(xp, p_prompt2, i, lw, TILE_M)
            conv_p.append(cs)
            xs, cs = _conv_sample(xs, cache_conv[j], lw, streams, steps)
            conv_s.append(cs)
        else:
            xp, ks, vs = _attn_mlp_prompt(xp, p_prompt2, i, lw, lw["bias_prompt"], TILE_M, final)
            k_p.append(ks)
            v_p.append(vs)
            xs, ks, vs = _attn_sample(xs, cache_k[j].reshape(streams, cache, KV_DIM),
                                      cache_v[j].reshape(streams, cache, KV_DIM), lw, lw["bias_sample"], streams, steps)
            k_s.append(ks)
            v_s.append(vs)
        xs = _mlp_ple(xs, p_sample2, i, lw, streams * steps, final)

    def heads(ts, n):
        return jnp.stack(ts).reshape(len(ts), n, -1, N_KV_HEADS, HEAD_DIM)

    return (xp, xs.reshape(streams, steps, D_MODEL), jnp.stack(conv_p), heads(k_p, batch), heads(v_p, batch),
            jnp.stack(conv_s), heads(k_s, streams), heads(v_s, streams))
```

```python
import functools
import math

import jax
import jax.numpy as jnp
from jax import lax
from jax.experimental import pallas as pl
from jax.experimental.pallas import tpu as pltpu

D_MODEL = 1024
DEPTH = 4
CHUNK = 64
CONV_WIDTH = 31
WINDOW = 128
N_HEADS = 16
N_KV_HEADS = 4
GROUP = N_HEADS // N_KV_HEADS
HEAD_DIM = 64
KV_DIM = N_KV_HEADS * HEAD_DIM
D_FF = 4 * D_MODEL
PLE_DIM = 256
NUM_BUCKETS = 32
MAX_DISTANCE = 128
EPS = 1e-6
NEG_INF = -1e30

BAND = WINDOW + CHUNK
KEYS_EXT = 256
ONES_ROWS = 16
HALO = 32
SUBLANES = 8
LANES = 128
CONV_ROWS = 64
FF_CHUNK = 1024
MXU_COLS = 256
LEAD_WEIGHT = 2
TILE_M = 512
VMEM_LIMIT = 60 * 1024 * 1024

V_NORM_MIX, V_PW1_B_LO, V_PW1_B_HI, V_DW_B, V_LN_G, V_LN_B, V_PW2_B, V_NORM_MLP, V_NORM_PLE, V_NORM_FINAL = range(10)
VEC_ROWS = 16

F32 = jnp.float32
BF16 = jnp.bfloat16


def _dot(a, b):
    return jnp.dot(a, b, preferred_element_type=F32)


def _sigmoid(x):
    return 1.0 / (1.0 + jnp.exp(-x))


def _rms(x, g):
    return x * lax.rsqrt(jnp.mean(x * x, axis=-1, keepdims=True) + EPS) * g


def _row(vec_ref, i):
    return vec_ref[i:i + 1, :]


def _layernorm_silu(c, g, b):
    mu = jnp.mean(c, axis=-1, keepdims=True)
    d = c - mu
    var = jnp.mean(d * d, axis=-1, keepdims=True)
    y = d * lax.rsqrt(var + EPS) * g + b
    return y * _sigmoid(y)


def _sink_softmax(logits, sink):
    m = jnp.maximum(jnp.max(logits, axis=-1, keepdims=True), sink)
    e = jnp.exp(logits - m)
    return e / (jnp.sum(e, axis=-1, keepdims=True) + jnp.exp(sink - m))


def _kv_lane_mask(shape, kh):
    lane = lax.broadcasted_iota(jnp.int32, shape, len(shape) - 1)
    return (lane >= kh * HEAD_DIM) & (lane < (kh + 1) * HEAD_DIM)


def _mlp_ple_rows(x, p, vec_ref, w1_ref, w2_ref, gate_ref, proj_ref, final, filler=(), lead=()):
    filler = list(filler)
    pieces = D_MODEL // MXU_COLS
    n_pieces = (D_FF // FF_CHUNK) * (FF_CHUNK // MXU_COLS + pieces) + LEAD_WEIGHT * len(lead)
    done = [0, 0]

    def fill(weight=1):
        done[0] += weight
        while done[1] < len(filler) * done[0] // n_pieces:
            filler[done[1]]()
            done[1] += 1

    for piece in lead:
        piece()
        fill(LEAD_WEIGHT)
    h = _rms(x, _row(vec_ref, V_NORM_MLP)).astype(BF16)
    xs = [x[:, n * MXU_COLS:(n + 1) * MXU_COLS] for n in range(pieces)]
    for c in range(D_FF // FF_CHUNK):
        hid = []
        for n in range(FF_CHUNK // MXU_COLS):
            lo = c * FF_CHUNK + n * MXU_COLS
            a = _dot(h, w1_ref[:, lo:lo + MXU_COLS])
            hid.append(jnp.square(jnp.maximum(a, 0.0)).astype(BF16))
            fill()
        hid = jnp.concatenate(hid, axis=1)
        for n in range(pieces):
            xs[n] = xs[n] + _dot(hid, w2_ref[c * FF_CHUNK:(c + 1) * FF_CHUNK, n * MXU_COLS:(n + 1) * MXU_COLS])
            fill()
    assert done == [n_pieces, len(filler)]
    x = jnp.concatenate(xs, axis=1)
    hg = _rms(x, _row(vec_ref, V_NORM_PLE)).astype(BF16)
    gate = _sigmoid(_dot(hg, gate_ref[...]))
    x = x + _dot(p.astype(BF16), proj_ref[...]) * gate
    if final:
        x = _rms(x, _row(vec_ref, V_NORM_FINAL))
    return x


def _mlp_ple_kernel(x_ref, p_ref, vec_ref, w1_ref, w2_ref, gate_ref, proj_ref, o_ref, *, final):
    o_ref[...] = _mlp_ple_rows(x_ref[...], p_ref[...], vec_ref, w1_ref, w2_ref, gate_ref, proj_ref, final)


def _const_spec(shape):
    nd = len(shape)
    return pl.BlockSpec(shape, lambda *_: (0,) * nd, pipeline_mode=pl.Buffered(1))


def _mlp_ple(x2d, p4d, layer, lw, tile_m, final):
    rows = x2d.shape[0]
    return pl.pallas_call(
        functools.partial(_mlp_ple_kernel, final=final),
        grid=(rows // tile_m,),
        in_specs=[
            pl.BlockSpec((tile_m, D_MODEL), lambda t: (t, 0)),
            pl.BlockSpec((None, tile_m, PLE_DIM), lambda t: (layer, t, 0)),
            _const_spec((VEC_ROWS, D_MODEL)),
            _const_spec((D_MODEL, D_FF)),
            _const_spec((D_FF, D_MODEL)),
            _const_spec((D_MODEL, D_MODEL)),
            _const_spec((PLE_DIM, D_MODEL)),
        ],
        out_specs=pl.BlockSpec((tile_m, D_MODEL), lambda t: (t, 0)),
        out_shape=jax.ShapeDtypeStruct((rows, D_MODEL), F32),
        compiler_params=pltpu.CompilerParams(
            dimension_semantics=("arbitrary",), vmem_limit_bytes=VMEM_LIMIT),
        name=f"mlp_ple_l{layer}_r{rows}",
    )(x2d, p4d, lw["vec"], lw["w1"], lw["w2"], lw["gate"], lw["proj"])


def _glu(x, vec_ref, pw1_ref):
    h = _rms(x, _row(vec_ref, V_NORM_MIX)).astype(BF16)
    a = _dot(h, pw1_ref[:, :D_MODEL]) + _row(vec_ref, V_PW1_B_LO)
    g = _dot(h, pw1_ref[:, D_MODEL:]) + _row(vec_ref, V_PW1_B_HI)
    return a * _sigmoid(g)


def _depthwise_unit(ext_ref, dwb_ref, c_ref, r0, l):
    first = HALO - (CONV_WIDTH - 1)
    groups = CONV_ROWS // SUBLANES
    spans = -(-(first + CONV_WIDTH) // SUBLANES)
    lanes = slice(l * LANES, (l + 1) * LANES)
    win = [ext_ref[r0 + SUBLANES * g:r0 + SUBLANES * (g + 1), lanes] for g in range(groups + spans)]
    out = None
    for s in range(SUBLANES):
        taps = [(j, SUBLANES * j + s - first) for j in range(spans)
                if 0 <= SUBLANES * j + s - first < CONV_WIDTH]
        phase = []
        for g in range(groups + (1 if s else 0)):
            p = None
            for j, k in taps:
                term = dwb_ref[k, :, lanes] * win[g + j]
                p = term if p is None else p + term
            phase.append(p)
        shifted = jnp.concatenate(phase, axis=0)[s:s + CONV_ROWS, :]
        out = shifted if out is None else out + shifted
    c_ref[r0:r0 + CONV_ROWS, lanes] = out


def _conv_mlp_prompt_kernel(x_ref, p_ref, vec_ref, dwb_ref, pw1_ref, pw2_ref, w1_ref, w2_ref, gate_ref, proj_ref,
                            o_ref, cs_ref, ext_ref, c_ref, mid_ref, *, tile_m, tiles_per_stream):
    t = pl.program_id(0)
    ext_rows = tile_m + HALO
    slot = t % 2

    @pl.when(t % tiles_per_stream == 0)
    def _():
        ext_ref[0:HALO, :] = jnp.zeros((HALO, D_MODEL), F32)
        ext_ref[ext_rows:ext_rows + SUBLANES, :] = jnp.zeros((SUBLANES, D_MODEL), F32)

    @pl.when(t == 0)
    def _():
        mid_ref[1] = jnp.zeros((tile_m, D_MODEL), F32)

    x = x_ref[...]
    h = _rms(x, _row(vec_ref, V_NORM_MIX)).astype(BF16)

    def glu_piece(n):
        cols = slice(n * MXU_COLS, (n + 1) * MXU_COLS)
        gcols = slice(D_MODEL + n * MXU_COLS, D_MODEL + (n + 1) * MXU_COLS)
        a = _dot(h, pw1_ref[:, cols]) + vec_ref[V_PW1_B_LO:V_PW1_B_LO + 1, cols]
        g = _dot(h, pw1_ref[:, gcols]) + vec_ref[V_PW1_B_HI:V_PW1_B_HI + 1, cols]
        ext_ref[HALO:ext_rows, cols] = a * _sigmoid(g)

    glu_piece(0)
    conv_units = [functools.partial(_depthwise_unit, ext_ref, dwb_ref, c_ref, r * CONV_ROWS, l)
                  for l in range(D_MODEL // LANES) for r in range(tile_m // CONV_ROWS)]
    o_ref[...] = _mlp_ple_rows(mid_ref[1 - slot], p_ref[...], vec_ref, w1_ref, w2_ref, gate_ref, proj_ref, False,
                               filler=conv_units,
                               lead=[functools.partial(glu_piece, n) for n in range(1, D_MODEL // MXU_COLS)])
    cs_ref[0] = ext_ref[ext_rows - (CONV_WIDTH - 1):ext_rows, :]
    act = _layernorm_silu(c_ref[...] + _row(vec_ref, V_DW_B), _row(vec_ref, V_LN_G), _row(vec_ref, V_LN_B))
    mid_ref[slot] = x + _dot(act.astype(BF16), pw2_ref[...]) + _row(vec_ref, V_PW2_B)

    @pl.when(t >= 0)
    def _():
        ext_ref[0:HALO, :] = ext_ref[tile_m:ext_rows, :]


def _conv_mlp_prompt(x, p3d, layer, lw, tile_m):
    b, s, _ = x.shape
    tiles_per_stream = s // tile_m
    n_tiles = b * tiles_per_stream
    mixer_tile = lambda t: jnp.minimum(t, n_tiles - 1)
    mlp_tile = lambda t: jnp.maximum(t - 1, 0)
    y, cs = pl.pallas_call(
        functools.partial(_conv_mlp_prompt_kernel, tile_m=tile_m, tiles_per_stream=tiles_per_stream),
        grid=(n_tiles + 1,),
        in_specs=[
            pl.BlockSpec((tile_m, D_MODEL), lambda t: (mixer_tile(t), 0)),
            pl.BlockSpec((None, tile_m, PLE_DIM), lambda t: (layer, mlp_tile(t), 0)),
            _const_spec((VEC_ROWS, D_MODEL)),
            _const_spec((CONV_WIDTH, SUBLANES, D_MODEL)),
            _const_spec((D_MODEL, 2 * D_MODEL)),
            _const_spec((D_MODEL, D_MODEL)),
            _const_spec((D_MODEL, D_FF)),
            _const_spec((D_FF, D_MODEL)),
            _const_spec((D_MODEL, D_MODEL)),
            _const_spec((PLE_DIM, D_MODEL)),
        ],
        out_specs=[
            pl.BlockSpec((tile_m, D_MODEL), lambda t: (mlp_tile(t), 0)),
            pl.BlockSpec((1, CONV_WIDTH - 1, D_MODEL), lambda t: (mixer_tile(t) // tiles_per_stream, 0, 0)),
        ],
        out_shape=[
            jax.ShapeDtypeStruct((b * s, D_MODEL), F32),
            jax.ShapeDtypeStruct((b, CONV_WIDTH - 1, D_MODEL), F32),
        ],
        scratch_shapes=[
            pltpu.VMEM((tile_m + HALO + SUBLANES, D_MODEL), F32),
            pltpu.VMEM((tile_m, D_MODEL), F32),
            pltpu.VMEM((2, tile_m, D_MODEL), F32),
        ],
        compiler_params=pltpu.CompilerParams(
            dimension_semantics=("arbitrary",), vmem_limit_bytes=VMEM_LIMIT),
        name="conv_mlp_prompt",
    )(x.reshape(b * s, D_MODEL), p3d, lw["vec"], lw["dwb"], lw["pw1"], lw["pw2"],
      lw["w1"], lw["w2"], lw["gate"], lw["proj"])
    return y.reshape(b, s, D_MODEL), cs


def _conv_sample_kernel(x_ref, left_ref, vec_ref, dw_ref, pw1_ref, pw2_ref, o_ref, cs_ref, ext_ref, *, streams, steps):
    x = x_ref[...]
    ext_ref[:, 0:HALO - (CONV_WIDTH - 1), :] = jnp.zeros((streams, HALO - (CONV_WIDTH - 1), D_MODEL), F32)
    ext_ref[:, HALO - (CONV_WIDTH - 1):HALO, :] = left_ref[...]
    ext_ref[:, HALO:HALO + steps, :] = _glu(x, vec_ref, pw1_ref).reshape(streams, steps, D_MODEL)
    acc = jnp.broadcast_to(_row(vec_ref, V_DW_B).reshape(1, 1, D_MODEL), (streams, steps, D_MODEL))
    for k in range(CONV_WIDTH):
        lo = k + HALO - (CONV_WIDTH - 1)
        acc = acc + dw_ref[k:k + 1, :].reshape(1, 1, D_MODEL) * ext_ref[:, lo:lo + steps, :]
    act = _layernorm_silu(acc.reshape(streams * steps, D_MODEL), _row(vec_ref, V_LN_G), _row(vec_ref, V_LN_B))
    cs_ref[...] = ext_ref[:, HALO + steps - (CONV_WIDTH - 1):HALO + steps, :]
    o_ref[...] = x + _dot(act.astype(BF16), pw2_ref[...]) + _row(vec_ref, V_PW2_B)


def _conv_sample(x2d, left, lw, streams, steps):
    rows = streams * steps
    return pl.pallas_call(
        functools.partial(_conv_sample_kernel, streams=streams, steps=steps),
        grid=(1,),
        in_specs=[
            _const_spec((rows, D_MODEL)),
            _const_spec((streams, CONV_WIDTH - 1, D_MODEL)),
            _const_spec((VEC_ROWS, D_MODEL)),
            _const_spec((CONV_WIDTH + 1, D_MODEL)),
            _const_spec((D_MODEL, 2 * D_MODEL)),
            _const_spec((D_MODEL, D_MODEL)),
        ],
        out_specs=[
            pl.BlockSpec((rows, D_MODEL), lambda t: (0, 0)),
            pl.BlockSpec((streams, CONV_WIDTH - 1, D_MODEL), lambda t: (0, 0, 0)),
        ],
        out_shape=[
            jax.ShapeDtypeStruct((rows, D_MODEL), F32),
            jax.ShapeDtypeStruct((streams, CONV_WIDTH - 1, D_MODEL), F32),
        ],
        scratch_shapes=[pltpu.VMEM((streams, HALO + steps, D_MODEL), F32)],
        compiler_params=pltpu.CompilerParams(
            dimension_semantics=("arbitrary",), vmem_limit_bytes=VMEM_LIMIT),
        name="conv_sample",
    )(x2d, left, lw["vec"], lw["dw"], lw["pw1"], lw["pw2"])


def _attn_mlp_prompt_kernel(x_ref, p_ref, vec_ref, bias_ref, wq_ref, wkv_ref, wo_ref, w1_ref, w2_ref, gate_ref,
                            proj_ref, o_ref, ks_ref, vs_ref, q_ref, kb_ref, vb_ref, att_ref, mid_ref,
                            *, tile_m, tiles_per_stream, final):
    t = pl.program_id(0)
    slot = t % 2
    tile_in_stream = t % tiles_per_stream

    @pl.when(tile_in_stream == 0)
    def _():
        kb_ref[0:WINDOW, :] = jnp.zeros((WINDOW, KV_DIM), BF16)
        vb_ref[0:WINDOW, :] = jnp.zeros((WINDOW, KV_DIM), BF16)

    @pl.when(t == 0)
    def _():
        mid_ref[1] = jnp.zeros((tile_m, D_MODEL), F32)

    x = x_ref[...]
    h = _rms(x, _row(vec_ref, V_NORM_MIX)).astype(BF16)
    q_ref[...] = _dot(h, wq_ref[...]).astype(BF16)
    kv = _dot(h, wkv_ref[...])
    k_new = kv[:, :KV_DIM]
    v_new = kv[:, KV_DIM:]
    kb_ref[WINDOW:WINDOW + tile_m, :] = k_new.astype(BF16)
    vb_ref[WINDOW:WINDOW + tile_m, :] = v_new.astype(BF16)
    ks_ref[0] = k_new[tile_m - WINDOW:, :]
    vs_ref[0] = v_new[tile_m - WINDOW:, :]

    rows = GROUP * CHUNK
    pad = jnp.zeros((KEYS_EXT - BAND, KV_DIM), BF16)
    weights = {}

    def chunk_logits(c0):
        q = jnp.concatenate(
            [q_ref[c0:c0 + CHUNK, g * KV_DIM:(g + 1) * KV_DIM] for g in range(GROUP)], axis=0)
        qm = jnp.concatenate(
            [jnp.where(_kv_lane_mask(q.shape, kh), q, jnp.zeros_like(q)) for kh in range(N_KV_HEADS)], axis=0)
        kb = jnp.concatenate([kb_ref[c0:c0 + BAND, :], pad], axis=0)
        logits = lax.dot_general(qm, kb, (((1,), (1,)), ((), ())), preferred_element_type=F32)
        logits = logits + bias_ref[...]
        key_pos = tile_in_stream * tile_m + c0 - WINDOW + lax.broadcasted_iota(jnp.int32, (1, KEYS_EXT), 1)
        logits = jnp.where(key_pos >= 0, logits, NEG_INF)
        weights[c0] = jnp.exp(logits - jnp.max(logits, axis=-1, keepdims=True)).astype(BF16)

    def chunk_values(c0):
        e = weights.pop(c0)
        vt = jnp.concatenate([vb_ref[c0:c0 + BAND, :], pad], axis=0).astype(F32).T.astype(BF16)
        key = lax.broadcasted_iota(jnp.int32, (ONES_ROWS, KEYS_EXT), 1)
        ones = jnp.where(key <= BAND, 1.0, 0.0).astype(BF16)
        parts = []
        for kh in range(N_KV_HEADS):
            lhs = jnp.concatenate([vt[kh * HEAD_DIM:(kh + 1) * HEAD_DIM, :], ones], axis=0)
            pv = lax.dot_general(lhs, e[kh * rows:(kh + 1) * rows, :], (((1,), (1,)), ((), ())),
                                 preferred_element_type=F32)
            parts.append(pv[0:HEAD_DIM, :] / pv[HEAD_DIM:HEAD_DIM + 1, :])
        out = jnp.concatenate(parts, axis=0).T
        for g in range(GROUP):
            att_ref[c0:c0 + CHUNK, g * KV_DIM:(g + 1) * KV_DIM] = out[g * CHUNK:(g + 1) * CHUNK, :].astype(BF16)

    chunk_units = []
    for c0 in range(0, tile_m, CHUNK):
        chunk_units += [functools.partial(chunk_logits, c0), functools.partial(chunk_values, c0)]
    o_ref[...] = _mlp_ple_rows(mid_ref[1 - slot], p_ref[...], vec_ref, w1_ref, w2_ref, gate_ref, proj_ref, final,
                               filler=chunk_units)

    mid_ref[slot] = x + _dot(att_ref[...], wo_ref[...])

    @pl.when(t >= 0)
    def _():
        kb_ref[0:WINDOW, :] = kb_ref[tile_m:tile_m + WINDOW, :]
        vb_ref[0:WINDOW, :] = vb_ref[tile_m:tile_m + WINDOW, :]


def _attn_mlp_prompt(x, p3d, layer, lw, bias, tile_m, final):
    b, s, _ = x.shape
    tiles_per_stream = s // tile_m
    n_tiles = b * tiles_per_stream
    mixer_tile = lambda t: jnp.minimum(t, n_tiles - 1)
    mlp_tile = lambda t: jnp.maximum(t - 1, 0)
    state_spec = pl.BlockSpec((1, WINDOW, KV_DIM), lambda t: (mixer_tile(t) // tiles_per_stream, 0, 0))
    y, ks, vs = pl.pallas_call(
        functools.partial(_attn_mlp_prompt_kernel, tile_m=tile_m, tiles_per_stream=tiles_per_stream, final=final),
        grid=(n_tiles + 1,),
        in_specs=[
            pl.BlockSpec((tile_m, D_MODEL), lambda t: (mixer_tile(t), 0)),
            pl.BlockSpec((None, tile_m, PLE_DIM), lambda t: (layer, mlp_tile(t), 0)),
            _const_spec((VEC_ROWS, D_MODEL)),
            _const_spec(bias.shape),
            _const_spec((D_MODEL, D_MODEL)),
            _const_spec((D_MODEL, 2 * KV_DIM)),
            _const_spec((D_MODEL, D_MODEL)),
            _const_spec((D_MODEL, D_FF)),
            _const_spec((D_FF, D_MODEL)),
            _const_spec((D_MODEL, D_MODEL)),
            _const_spec((PLE_DIM, D_MODEL)),
        ],
        out_specs=[
            pl.BlockSpec((tile_m, D_MODEL), lambda t: (mlp_tile(t), 0)),
            state_spec,
            state_spec,
        ],
        out_shape=[
            jax.ShapeDtypeStruct((b * s, D_MODEL), F32),
            jax.ShapeDtypeStruct((b, WINDOW, KV_DIM), F32),
            jax.ShapeDtypeStruct((b, WINDOW, KV_DIM), F32),
        ],
        scratch_shapes=[
            pltpu.VMEM((tile_m, D_MODEL), BF16),
            pltpu.VMEM((tile_m + WINDOW, KV_DIM), BF16),
            pltpu.VMEM((tile_m + WINDOW, KV_DIM), BF16),
            pltpu.VMEM((tile_m, D_MODEL), BF16),
            pltpu.VMEM((2, tile_m, D_MODEL), F32),
        ],
        compiler_params=pltpu.CompilerParams(
            dimension_semantics=("arbitrary",), vmem_limit_bytes=VMEM_LIMIT),
        name="attn_mlp_prompt",
    )(x.reshape(b * s, D_MODEL), p3d, lw["vec"], bias, lw["wq"], lw["wkv"], lw["wo"],
      lw["w1"], lw["w2"], lw["gate"], lw["proj"])
    return y.reshape(b, s, D_MODEL), ks, vs


def _attn_sample_kernel(x_ref, kc_ref, vc_ref, vec_ref, bias_ref, wq_ref, wkv_ref, wo_ref,
                        o_ref, ks_ref, vs_ref, kf_ref, vf_ref, *, streams, steps):
    cache = kc_ref.shape[1]
    keys = cache + steps
    x = x_ref[...]
    h = _rms(x, _row(vec_ref, V_NORM_MIX)).astype(BF16)
    q = _dot(h, wq_ref[...]).reshape(streams, steps, D_MODEL)
    kv = _dot(h, wkv_ref[...])
    kf_ref[:, 0:cache, :] = kc_ref[...]
    vf_ref[:, 0:cache, :] = vc_ref[...]
    kf_ref[:, cache:keys, :] = kv[:, :KV_DIM].reshape(streams, steps, KV_DIM)
    vf_ref[:, cache:keys, :] = kv[:, KV_DIM:].reshape(streams, steps, KV_DIM)
    ks_ref[...] = kf_ref[:, steps:keys, :]
    vs_ref[...] = vf_ref[:, steps:keys, :]

    qs = jnp.concatenate([q[:, :, g * KV_DIM:(g + 1) * KV_DIM] for g in range(GROUP)], axis=1).astype(BF16)
    kf = kf_ref[...].astype(BF16)
    vf = vf_ref[...].astype(BF16)
    out = jnp.zeros((streams, GROUP * steps, KV_DIM), F32)
    for kh in range(N_KV_HEADS):
        k_kh = jnp.where(_kv_lane_mask(kf.shape, kh), kf, jnp.zeros_like(kf))
        logits = jnp.einsum("bqd,bkd->bqk", qs, k_kh, preferred_element_type=F32)
        logits = logits + bias_ref[kh, :, 0:keys][None]
        probs = _sink_softmax(logits, bias_ref[kh, :, keys:keys + 1][None])
        o_kh = jnp.einsum("bqk,bkd->bqd", probs.astype(BF16), vf, preferred_element_type=F32)
        out = jnp.where(_kv_lane_mask(out.shape, kh), o_kh, out)
    att = jnp.concatenate([out[:, g * steps:(g + 1) * steps, :] for g in range(GROUP)], axis=2)
    o_ref[...] = x + _dot(att.reshape(streams * steps, D_MODEL).astype(BF16), wo_ref[...])


def _attn_sample(x2d, k_left, v_left, lw, bias, streams, steps):
    rows = streams * steps
    cache = k_left.shape[1]
    return pl.pallas_call(
        functools.partial(_attn_sample_kernel, streams=streams, steps=steps),
        grid=(1,),
        in_specs=[
            _const_spec((rows, D_MODEL)),
            _const_spec(k_left.shape),
            _const_spec(v_left.shape),
            _const_spec((VEC_ROWS, D_MODEL)),
            _const_spec(bias.shape),
            _const_spec((D_MODEL, D_MODEL)),
            _const_spec((D_MODEL, 2 * KV_DIM)),
            _const_spec((D_MODEL, D_MODEL)),
        ],
        out_specs=[
            pl.BlockSpec((rows, D_MODEL), lambda t: (0, 0)),
            pl.BlockSpec((streams, cache, KV_DIM), lambda t: (0, 0, 0)),
            pl.BlockSpec((streams, cache, KV_DIM), lambda t: (0, 0, 0)),
        ],
        out_shape=[
            jax.ShapeDtypeStruct((rows, D_MODEL), F32),
            jax.ShapeDtypeStruct((streams, cache, KV_DIM), F32),
            jax.ShapeDtypeStruct((streams, cache, KV_DIM), F32),
        ],
        scratch_shapes=[
            pltpu.VMEM((streams, cache + steps, KV_DIM), F32),
            pltpu.VMEM((streams, cache + steps, KV_DIM), F32),
        ],
        compiler_params=pltpu.CompilerParams(
            dimension_semantics=("arbitrary",), vmem_limit_bytes=VMEM_LIMIT),
        name="attn_sample",
    )(x2d, k_left, v_left, lw["vec"], bias, lw["wq"], lw["wkv"], lw["wo"])


def _t5_bucket(rel):
    nb = NUM_BUCKETS // 2
    max_exact = nb // 2
    ret = (rel > 0).astype(jnp.int32) * nb
    n = jnp.abs(rel)
    nf = jnp.maximum(n, 1).astype(F32)
    large = max_exact + (jnp.log(nf / max_exact) / math.log(MAX_DISTANCE / max_exact)
                         * (nb - max_exact)).astype(jnp.int32)
    large = jnp.minimum(large, nb - 1)
    return ret + jnp.where(n < max_exact, n, large)


def _bias_table(rel_bias, sinks, q_len, k_len, offset):
    rel = jnp.arange(k_len, dtype=jnp.int32)[None, :] - offset - jnp.arange(q_len, dtype=jnp.int32)[:, None]
    onehot = (_t5_bucket(rel)[:, :, None] == jnp.arange(NUM_BUCKETS, dtype=jnp.int32)).astype(F32)
    b = jnp.einsum("qkn,nh->hqk", onehot, rel_bias.astype(F32), precision=lax.Precision.HIGHEST)
    b = b.reshape(N_KV_HEADS, GROUP * q_len, k_len)
    s = jnp.repeat(sinks.astype(F32).reshape(N_KV_HEADS, GROUP), q_len, axis=1)[:, :, None]
    pad = jnp.full((N_KV_HEADS, GROUP * q_len, KEYS_EXT - k_len - 1), NEG_INF, F32)
    return jnp.concatenate([b, s, pad], axis=2)


def _pack_vec(rows):
    zero = jnp.zeros((D_MODEL,), F32)
    return jnp.stack([rows[i].astype(F32) if i in rows else zero for i in range(VEC_ROWS)])


def kernel(x_prompt, x_sample, cache_conv, cache_k, cache_v, p_prompt, p_sample, rel_bias, norm_mix, norm_mlp, norm_ple, norm_final, conv_pw1, conv_pw1_b, conv_dw, conv_dw_b, conv_ln_g, conv_ln_b, conv_pw2, conv_pw2_b, attn_wq, attn_wk, attn_wv, attn_wo, attn_sinks, mlp_w1, mlp_w2, ple_proj, ple_gate):
    batch, seq, _ = x_prompt.shape
    streams, steps, _ = x_sample.shape
    cache = cache_k.shape[2]
    assert seq % TILE_M == 0 and TILE_M % CHUNK == 0 and TILE_M >= WINDOW and steps % SUBLANES == 0

    layers = []
    for i in range(DEPTH):
        j = i // 2
        rows = {V_NORM_MIX: norm_mix[i], V_NORM_MLP: norm_mlp[i], V_NORM_PLE: norm_ple[i], V_NORM_FINAL: norm_final}
        lw = {"w1": mlp_w1[i].astype(BF16), "w2": mlp_w2[i].astype(BF16),
              "gate": ple_gate[i].astype(BF16), "proj": ple_proj[i].astype(BF16)}
        if i % 2 == 0:
            rows.update({V_PW1_B_LO: conv_pw1_b[j, :D_MODEL], V_PW1_B_HI: conv_pw1_b[j, D_MODEL:],
                         V_DW_B: conv_dw_b[j], V_LN_G: conv_ln_g[j], V_LN_B: conv_ln_b[j], V_PW2_B: conv_pw2_b[j]})
            lw["pw1"] = conv_pw1[j].astype(BF16)
            lw["pw2"] = conv_pw2[j].astype(BF16)
            lw["dw"] = jnp.concatenate([conv_dw[j].astype(F32), jnp.zeros((1, D_MODEL), F32)], axis=0)
            lw["dwb"] = jnp.broadcast_to(conv_dw[j].astype(F32)[:, None, :], (CONV_WIDTH, SUBLANES, D_MODEL))
        else:
            wq = (attn_wq[j] * (HEAD_DIM ** -0.5)).reshape(D_MODEL, N_KV_HEADS, GROUP, HEAD_DIM).transpose(0, 2, 1, 3)
            lw["wq"] = wq.reshape(D_MODEL, D_MODEL).astype(BF16)
            lw["wkv"] = jnp.concatenate([attn_wk[j], attn_wv[j]], axis=1).astype(BF16)
            wo = attn_wo[j].reshape(N_KV_HEADS, GROUP, HEAD_DIM, D_MODEL).transpose(1, 0, 2, 3)
            lw["wo"] = wo.reshape(D_MODEL, D_MODEL).astype(BF16)
            lw["bias_prompt"] = _bias_table(rel_bias, attn_sinks[j], CHUNK, BAND, WINDOW).reshape(N_HEADS * CHUNK, 256)
            lw["bias_sample"] = _bias_table(rel_bias, attn_sinks[j], steps, cache + steps, cache)
        lw["vec"] = _pack_vec(rows)
        layers.append(lw)

    p_prompt2 = p_prompt.reshape(DEPTH, batch * seq, PLE_DIM)
    p_sample2 = p_sample.reshape(DEPTH, streams * steps, PLE_DIM)

    xp = x_prompt
    xs = x_sample.reshape(streams * steps, D_MODEL)
    conv_p, k_p, v_p, conv_s, k_s, v_s = [], [], [], [], [], []
    for i, lw in enumerate(layers):
        j = i // 2
        final = i == DEPTH - 1
        if i % 2 == 0:
            xp, cs = _conv_mlp_prompt(xp, p_prompt2, i, lw, TILE_M)
            conv_p.append(cs)
            xs, cs = _conv_sample(xs, cache_conv[j], lw, streams, steps)
            conv_s.append(cs)
        else:
            xp, ks, vs = _attn_mlp_prompt(xp, p_prompt2, i, lw, lw["bias_prompt"], TILE_M, final)
            k_p.append(ks)
            v_p.append(vs)
            xs, ks, vs = _attn_sample(xs, cache_k[j].reshape(streams, cache, KV_DIM),
                                      cache_v[j].reshape(streams, cache, KV_DIM), lw, lw["bias_sample"], streams, steps)
            k_s.append(ks)
            v_s.append(vs)
        xs = _mlp_ple(xs, p_sample2, i, lw, streams * steps, final)

    def heads(ts, n):
        return jnp.stack(ts).reshape(len(ts), n, -1, N_KV_HEADS, HEAD_DIM)

    return (xp, xs.reshape(streams, steps, D_MODEL), jnp.stack(conv_p), heads(k_p, batch), heads(v_p, batch),
            jnp.stack(conv_s), heads(k_s, streams), heads(v_s, streams))
```

```python
import functools
import math

import jax
import jax.numpy as jnp
from jax import lax
from jax.experimental import pallas as pl
from jax.experimental.pallas import tpu as pltpu

D_MODEL = 1024
DEPTH = 4
CHUNK = 64
CONV_WIDTH = 31
WINDOW = 128
N_HEADS = 16
N_KV_HEADS = 4
GROUP = N_HEADS // N_KV_HEADS
HEAD_DIM = 64
KV_DIM = N_KV_HEADS * HEAD_DIM
D_FF = 4 * D_MODEL
PLE_DIM = 256
NUM_BUCKETS = 32
MAX_DISTANCE = 128
EPS = 1e-6
NEG_INF = -1e30

BAND = WINDOW + CHUNK
KEYS_EXT = 256
HALO = 32
SUBLANES = 8
LANES = 128
CONV_ROWS = 64
FF_CHUNK = 1024
MXU_COLS = 256
LEAD_WEIGHT = 2
TILE_M = 512
VMEM_LIMIT = 60 * 1024 * 1024

V_NORM_MIX, V_PW1_B_LO, V_PW1_B_HI, V_DW_B, V_LN_G, V_LN_B, V_PW2_B, V_NORM_MLP, V_NORM_PLE, V_NORM_FINAL = range(10)
VEC_ROWS = 16

F32 = jnp.float32
BF16 = jnp.bfloat16


def _dot(a, b):
    return jnp.dot(a, b, preferred_element_type=F32)


def _sigmoid(x):
    return 1.0 / (1.0 + jnp.exp(-x))


def _rms(x, g):
    return x * lax.rsqrt(jnp.mean(x * x, axis=-1, keepdims=True) + EPS) * g


def _row(vec_ref, i):
    return vec_ref[i:i + 1, :]


def _layernorm_silu(c, g, b):
    mu = jnp.mean(c, axis=-1, keepdims=True)
    d = c - mu
    var = jnp.mean(d * d, axis=-1, keepdims=True)
    y = d * lax.rsqrt(var + EPS) * g + b
    return y * _sigmoid(y)


def _sink_softmax(logits, sink):
    m = jnp.maximum(jnp.max(logits, axis=-1, keepdims=True), sink)
    e = jnp.exp(logits - m)
    return e / (jnp.sum(e, axis=-1, keepdims=True) + jnp.exp(sink - m))


def _kv_lane_mask(shape, kh):
    lane = lax.broadcasted_iota(jnp.int32, shape, len(shape) - 1)
    return (lane >= kh * HEAD_DIM) & (lane < (kh + 1) * HEAD_DIM)


def _mlp_ple_rows(x, p, vec_ref, w1_ref, w2_ref, gate_ref, proj_ref, final, filler=(), lead=(), tail=()):
    filler = list(filler)
    pieces = D_MODEL // MXU_COLS
    n_pieces = (D_FF // FF_CHUNK) * (FF_CHUNK // MXU_COLS + pieces) + LEAD_WEIGHT * len(lead)
    done = [0, 0]

    def fill(weight=1):
        done[0] += weight
        while done[1] < len(filler) * done[0] // n_pieces:
            filler[done[1]]()
            done[1] += 1

    for piece in lead:
        piece()
        fill(LEAD_WEIGHT)
    h = _rms(x, _row(vec_ref, V_NORM_MLP)).astype(BF16)
    xs = [x[:, n * MXU_COLS:(n + 1) * MXU_COLS] for n in range(pieces)]
    for c in range(D_FF // FF_CHUNK):
        hid = []
        for n in range(FF_CHUNK // MXU_COLS):
            lo = c * FF_CHUNK + n * MXU_COLS
            a = _dot(h, w1_ref[:, lo:lo + MXU_COLS])
            hid.append(jnp.square(jnp.maximum(a, 0.0)).astype(BF16))
            fill()
        hid = jnp.concatenate(hid, axis=1)
        for n in range(pieces):
            xs[n] = xs[n] + _dot(hid, w2_ref[c * FF_CHUNK:(c + 1) * FF_CHUNK, n * MXU_COLS:(n + 1) * MXU_COLS])
            fill()
    assert done == [n_pieces, len(filler)]
    x = jnp.concatenate(xs, axis=1)
    hg = _rms(x, _row(vec_ref, V_NORM_PLE)).astype(BF16)
    if tail:
        assert len(tail) == pieces
        gate = []
        for n in range(pieces):
            gate.append(_dot(hg, gate_ref[:, n * MXU_COLS:(n + 1) * MXU_COLS]))
            tail[n]()
        gate = _sigmoid(jnp.concatenate(gate, axis=1))
    else:
        gate = _sigmoid(_dot(hg, gate_ref[...]))
    x = x + _dot(p.astype(BF16), proj_ref[...]) * gate
    if final:
        x = _rms(x, _row(vec_ref, V_NORM_FINAL))
    return x


def _mlp_ple_kernel(x_ref, p_ref, vec_ref, w1_ref, w2_ref, gate_ref, proj_ref, o_ref, *, final):
    o_ref[...] = _mlp_ple_rows(x_ref[...], p_ref[...], vec_ref, w1_ref, w2_ref, gate_ref, proj_ref, final)


def _const_spec(shape):
    nd = len(shape)
    return pl.BlockSpec(shape, lambda *_: (0,) * nd, pipeline_mode=pl.Buffered(1))


def _mlp_ple(x2d, p4d, layer, lw, tile_m, final):
    rows = x2d.shape[0]
    return pl.pallas_call(
        functools.partial(_mlp_ple_kernel, final=final),
        grid=(rows // tile_m,),
        in_specs=[
            pl.BlockSpec((tile_m, D_MODEL), lambda t: (t, 0)),
            pl.BlockSpec((None, tile_m, PLE_DIM), lambda t: (layer, t, 0)),
            _const_spec((VEC_ROWS, D_MODEL)),
            _const_spec((D_MODEL, D_FF)),
            _const_spec((D_FF, D_MODEL)),
            _const_spec((D_MODEL, D_MODEL)),
            _const_spec((PLE_DIM, D_MODEL)),
        ],
        out_specs=pl.BlockSpec((tile_m, D_MODEL), lambda t: (t, 0)),
        out_shape=jax.ShapeDtypeStruct((rows, D_MODEL), F32),
        compiler_params=pltpu.CompilerParams(
            dimension_semantics=("arbitrary",), vmem_limit_bytes=VMEM_LIMIT),
        name=f"mlp_ple_l{layer}_r{rows}",
    )(x2d, p4d, lw["vec"], lw["w1"], lw["w2"], lw["gate"], lw["proj"])


def _glu(x, vec_ref, pw1_ref):
    h = _rms(x, _row(vec_ref, V_NORM_MIX)).astype(BF16)
    a = _dot(h, pw1_ref[:, :D_MODEL]) + _row(vec_ref, V_PW1_B_LO)
    g = _dot(h, pw1_ref[:, D_MODEL:]) + _row(vec_ref, V_PW1_B_HI)
    return a * _sigmoid(g)


def _depthwise_unit(ext_ref, dwb_ref, c_ref, r0, l):
    first = HALO - (CONV_WIDTH - 1)
    groups = CONV_ROWS // SUBLANES
    spans = -(-(first + CONV_WIDTH) // SUBLANES)
    lanes = slice(l * LANES, (l + 1) * LANES)
    win = [ext_ref[r0 + SUBLANES * g:r0 + SUBLANES * (g + 1), lanes] for g in range(groups + spans)]
    out = None
    for s in range(SUBLANES):
        taps = [(j, SUBLANES * j + s - first) for j in range(spans)
                if 0 <= SUBLANES * j + s - first < CONV_WIDTH]
        phase = []
        for g in range(groups + (1 if s else 0)):
            p = None
            for j, k in taps:
                term = dwb_ref[k, :, lanes] * win[g + j]
                p = term if p is None else p + term
            phase.append(p)
        shifted = jnp.concatenate(phase, axis=0)[s:s + CONV_ROWS, :]
        out = shifted if out is None else out + shifted
    c_ref[r0:r0 + CONV_ROWS, lanes] = out


def _conv_mlp_prompt_kernel(x_ref, p_ref, vec_ref, dwb_ref, pw1_ref, pw2_ref, w1_ref, w2_ref, gate_ref, proj_ref,
                            o_ref, cs_ref, ext_ref, c_ref, act_ref, mid_ref, *, tile_m, tiles_per_stream):
    t = pl.program_id(0)
    ext_rows = tile_m + HALO
    slot = t % 2

    @pl.when(t % tiles_per_stream == 0)
    def _():
        ext_ref[0:HALO, :] = jnp.zeros((HALO, D_MODEL), F32)
        ext_ref[ext_rows:ext_rows + SUBLANES, :] = jnp.zeros((SUBLANES, D_MODEL), F32)

    @pl.when(t == 0)
    def _():
        mid_ref[1] = jnp.zeros((tile_m, D_MODEL), F32)

    x = x_ref[...]
    h = _rms(x, _row(vec_ref, V_NORM_MIX)).astype(BF16)

    def glu_piece(n):
        cols = slice(n * MXU_COLS, (n + 1) * MXU_COLS)
        gcols = slice(D_MODEL + n * MXU_COLS, D_MODEL + (n + 1) * MXU_COLS)
        a = _dot(h, pw1_ref[:, cols]) + vec_ref[V_PW1_B_LO:V_PW1_B_LO + 1, cols]
        g = _dot(h, pw1_ref[:, gcols]) + vec_ref[V_PW1_B_HI:V_PW1_B_HI + 1, cols]
        ext_ref[HALO:ext_rows, cols] = a * _sigmoid(g)

    glu_piece(0)
    conv_units = [functools.partial(_depthwise_unit, ext_ref, dwb_ref, c_ref, r * CONV_ROWS, l)
                  for l in range(D_MODEL // LANES) for r in range(tile_m // CONV_ROWS)]
    pieces = D_MODEL // MXU_COLS

    def norm_part(i):
        rows = slice(i * tile_m // pieces, (i + 1) * tile_m // pieces)
        act = _layernorm_silu(c_ref[rows, :] + _row(vec_ref, V_DW_B), _row(vec_ref, V_LN_G), _row(vec_ref, V_LN_B))
        act_ref[rows, :] = act.astype(BF16)

    o_ref[...] = _mlp_ple_rows(mid_ref[1 - slot], p_ref[...], vec_ref, w1_ref, w2_ref, gate_ref, proj_ref, False,
                               filler=conv_units,
                               lead=[functools.partial(glu_piece, n) for n in range(1, pieces)],
                               tail=[functools.partial(norm_part, i) for i in range(pieces)])
    cs_ref[0] = ext_ref[ext_rows - (CONV_WIDTH - 1):ext_rows, :]
    mid_ref[slot] = x + _dot(act_ref[...], pw2_ref[...]) + _row(vec_ref, V_PW2_B)

    @pl.when(t >= 0)
    def _():
        ext_ref[0:HALO, :] = ext_ref[tile_m:ext_rows, :]


def _conv_mlp_prompt(x, p3d, layer, lw, tile_m):
    b, s, _ = x.shape
    tiles_per_stream = s // tile_m
    n_tiles = b * tiles_per_stream
    mixer_tile = lambda t: jnp.minimum(t, n_tiles - 1)
    mlp_tile = lambda t: jnp.maximum(t - 1, 0)
    y, cs = pl.pallas_call(
        functools.partial(_conv_mlp_prompt_kernel, tile_m=tile_m, tiles_per_stream=tiles_per_stream),
        grid=(n_tiles + 1,),
        in_specs=[
            pl.BlockSpec((tile_m, D_MODEL), lambda t: (mixer_tile(t), 0)),
            pl.BlockSpec((None, tile_m, PLE_DIM), lambda t: (layer, mlp_tile(t), 0)),
            _const_spec((VEC_ROWS, D_MODEL)),
            _const_spec((CONV_WIDTH, SUBLANES, D_MODEL)),
            _const_spec((D_MODEL, 2 * D_MODEL)),
            _const_spec((D_MODEL, D_MODEL)),
            _const_spec((D_MODEL, D_FF)),
            _const_spec((D_FF, D_MODEL)),
            _const_spec((D_MODEL, D_MODEL)),
            _const_spec((PLE_DIM, D_MODEL)),
        ],
        out_specs=[
            pl.BlockSpec((tile_m, D_MODEL), lambda t: (mlp_tile(t), 0)),
            pl.BlockSpec((1, CONV_WIDTH - 1, D_MODEL), lambda t: (mixer_tile(t) // tiles_per_stream, 0, 0)),
        ],
        out_shape=[
            jax.ShapeDtypeStruct((b * s, D_MODEL), F32),
            jax.ShapeDtypeStruct((b, CONV_WIDTH - 1, D_MODEL), F32),
        ],
        scratch_shapes=[
            pltpu.VMEM((tile_m + HALO + SUBLANES, D_MODEL), F32),
            pltpu.VMEM((tile_m, D_MODEL), F32),
            pltpu.VMEM((tile_m, D_MODEL), BF16),
            pltpu.VMEM((2, tile_m, D_MODEL), F32),
        ],
        compiler_params=pltpu.CompilerParams(
            dimension_semantics=("arbitrary",), vmem_limit_bytes=VMEM_LIMIT),
        name="conv_mlp_prompt",
    )(x.reshape(b * s, D_MODEL), p3d, lw["vec"], lw["dwb"], lw["pw1"], lw["pw2"],
      lw["w1"], lw["w2"], lw["gate"], lw["proj"])
    return y.reshape(b, s, D_MODEL), cs


def _conv_sample_kernel(x_ref, left_ref, vec_ref, dw_ref, pw1_ref, pw2_ref, o_ref, cs_ref, ext_ref, *, streams, steps):
    x = x_ref[...]
    ext_ref[:, 0:HALO - (CONV_WIDTH - 1), :] = jnp.zeros((streams, HALO - (CONV_WIDTH - 1), D_MODEL), F32)
    ext_ref[:, HALO - (CONV_WIDTH - 1):HALO, :] = left_ref[...]
    ext_ref[:, HALO:HALO + steps, :] = _glu(x, vec_ref, pw1_ref).reshape(streams, steps, D_MODEL)
    acc = jnp.broadcast_to(_row(vec_ref, V_DW_B).reshape(1, 1, D_MODEL), (streams, steps, D_MODEL))
    for k in range(CONV_WIDTH):
        lo = k + HALO - (CONV_WIDTH - 1)
        acc = acc + dw_ref[k:k + 1, :].reshape(1, 1, D_MODEL) * ext_ref[:, lo:lo + steps, :]
    act = _layernorm_silu(acc.reshape(streams * steps, D_MODEL), _row(vec_ref, V_LN_G), _row(vec_ref, V_LN_B))
    cs_ref[...] = ext_ref[:, HALO + steps - (CONV_WIDTH - 1):HALO + steps, :]
    o_ref[...] = x + _dot(act.astype(BF16), pw2_ref[...]) + _row(vec_ref, V_PW2_B)


def _conv_sample(x2d, left, lw, streams, steps):
    rows = streams * steps
    return pl.pallas_call(
        functools.partial(_conv_sample_kernel, streams=streams, steps=steps),
        grid=(1,),
        in_specs=[
            _const_spec((rows, D_MODEL)),
            _const_spec((streams, CONV_WIDTH - 1, D_MODEL)),
            _const_spec((VEC_ROWS, D_MODEL)),
            _const_spec((CONV_WIDTH + 1, D_MODEL)),
            _const_spec((D_MODEL, 2 * D_MODEL)),
            _const_spec((D_MODEL, D_MODEL)),
        ],
        out_specs=[
            pl.BlockSpec((rows, D_MODEL), lambda t: (0, 0)),
            pl.BlockSpec((streams, CONV_WIDTH - 1, D_MODEL), lambda t: (0, 0, 0)),
        ],
        out_shape=[
            jax.ShapeDtypeStruct((rows, D_MODEL), F32),
            jax.ShapeDtypeStruct((streams, CONV_WIDTH - 1, D_MODEL), F32),
        ],
        scratch_shapes=[pltpu.VMEM((streams, HALO + steps, D_MODEL), F32)],
        compiler_params=pltpu.CompilerParams(
            dimension_semantics=("arbitrary",), vmem_limit_bytes=VMEM_LIMIT),
        name="conv_sample",
    )(x2d, left, lw["vec"], lw["dw"], lw["pw1"], lw["pw2"])


def _attn_mlp_prompt_kernel(x_ref, p_ref, vec_ref, bias_ref, wq_ref, wkv_ref, wo_ref, w1_ref, w2_ref, gate_ref,
                            proj_ref, o_ref, ks_ref, vs_ref, q_ref, kb_ref, vb_ref, att_ref, mid_ref,
                            *, tile_m, tiles_per_stream, final):
    t = pl.program_id(0)
    slot = t % 2
    tile_in_stream = t % tiles_per_stream

    @pl.when(tile_in_stream == 0)
    def _():
        kb_ref[0:WINDOW, :] = jnp.zeros((WINDOW, KV_DIM), BF16)
        vb_ref[0:WINDOW, :] = jnp.zeros((WINDOW, KV_DIM), BF16)

    @pl.when(t == 0)
    def _():
        mid_ref[1] = jnp.zeros((tile_m, D_MODEL), F32)

    x = x_ref[...]
    h = _rms(x, _row(vec_ref, V_NORM_MIX)).astype(BF16)
    q_ref[...] = _dot(h, wq_ref[...]).astype(BF16)
    kv = _dot(h, wkv_ref[...])
    k_new = kv[:, :KV_DIM]
    v_new = kv[:, KV_DIM:]
    kb_ref[WINDOW:WINDOW + tile_m, :] = k_new.astype(BF16)
    vb_ref[WINDOW:WINDOW + tile_m, :] = v_new.astype(BF16)
    ks_ref[0] = k_new[tile_m - WINDOW:, :]
    vs_ref[0] = v_new[tile_m - WINDOW:, :]

    rows = GROUP * CHUNK
    pad = jnp.zeros((KEYS_EXT - BAND, KV_DIM), BF16)
    weights = {}

    def chunk_logits(c0):
        q = jnp.concatenate(
            [q_ref[c0:c0 + CHUNK, g * KV_DIM:(g + 1) * KV_DIM] for g in range(GROUP)], axis=0)
        qm = jnp.concatenate(
            [jnp.where(_kv_lane_mask(q.shape, kh), q, jnp.zeros_like(q)) for kh in range(N_KV_HEADS)], axis=0)
        kb = jnp.concatenate([kb_ref[c0:c0 + BAND, :], pad], axis=0)
        logits = lax.dot_general(qm, kb, (((1,), (1,)), ((), ())), preferred_element_type=F32)
        logits = logits + bias_ref[...]
        key_pos = tile_in_stream * tile_m + c0 - WINDOW + lax.broadcasted_iota(jnp.int32, (1, KEYS_EXT), 1)
        logits = jnp.where(key_pos >= 0, logits, NEG_INF)
        weights[c0] = jnp.exp(logits - jnp.max(logits, axis=-1, keepdims=True)).astype(BF16)

    def chunk_values(c0):
        e = weights.pop(c0)
        vb = vb_ref[c0:c0 + BAND, :]
        half = LANES // HEAD_DIM
        krow = lax.broadcasted_iota(jnp.int32, (KEYS_EXT, LANES), 0)
        klane = lax.broadcasted_iota(jnp.int32, (KEYS_EXT, LANES), 1)
        outs = []
        for pair in range(N_KV_HEADS // half):
            num = den = None
            for sub in range(half):
                kh = pair * half + sub
                in_head = (klane >= sub * HEAD_DIM) & (klane < (sub + 1) * HEAD_DIM)
                v_kh = jnp.concatenate([vb[:, pair * LANES:(pair + 1) * LANES], pad[:, 0:LANES]], axis=0)
                v_kh = jnp.where(in_head, v_kh, jnp.zeros_like(v_kh))
                ones = jnp.where(in_head & (krow <= BAND), 1.0, 0.0).astype(BF16)
                pv = _dot(e[kh * rows:(kh + 1) * rows, :], jnp.concatenate([v_kh, ones], axis=1))
                num = pv[:, 0:LANES] if num is None else num + pv[:, 0:LANES]
                den = pv[:, LANES:] if den is None else den + pv[:, LANES:]
            outs.append(num / den)
        out = jnp.concatenate(outs, axis=1)
        for g in range(GROUP):
            att_ref[c0:c0 + CHUNK, g * KV_DIM:(g + 1) * KV_DIM] = out[g * CHUNK:(g + 1) * CHUNK, :].astype(BF16)

    chunk_units = []
    for c0 in range(0, tile_m, CHUNK):
        chunk_units += [functools.partial(chunk_logits, c0), functools.partial(chunk_values, c0)]
    o_ref[...] = _mlp_ple_rows(mid_ref[1 - slot], p_ref[...], vec_ref, w1_ref, w2_ref, gate_ref, proj_ref, final,
                               filler=chunk_units)

    mid_ref[slot] = x + _dot(att_ref[...], wo_ref[...])

    @pl.when(t >= 0)
    def _():
        kb_ref[0:WINDOW, :] = kb_ref[tile_m:tile_m + WINDOW, :]
        vb_ref[0:WINDOW, :] = vb_ref[tile_m:tile_m + WINDOW, :]


def _attn_mlp_prompt(x, p3d, layer, lw, bias, tile_m, final):
    b, s, _ = x.shape
    tiles_per_stream = s // tile_m
    n_tiles = b * tiles_per_stream
    mixer_tile = lambda t: jnp.minimum(t, n_tiles - 1)
    mlp_tile = lambda t: jnp.maximum(t - 1, 0)
    state_spec = pl.BlockSpec((1, WINDOW, KV_DIM), lambda t: (mixer_tile(t) // tiles_per_stream, 0, 0))
    y, ks, vs = pl.pallas_call(
        functools.partial(_attn_mlp_prompt_kernel, tile_m=tile_m, tiles_per_stream=tiles_per_stream, final=final),
        grid=(n_tiles + 1,),
        in_specs=[
            pl.BlockSpec((tile_m, D_MODEL), lambda t: (mixer_tile(t), 0)),
            pl.BlockSpec((None, tile_m, PLE_DIM), lambda t: (layer, mlp_tile(t), 0)),
            _const_spec((VEC_ROWS, D_MODEL)),
            _const_spec(bias.shape),
            _const_spec((D_MODEL, D_MODEL)),
            _const_spec((D_MODEL, 2 * KV_DIM)),
            _const_spec((D_MODEL, D_MODEL)),
            _const_spec((D_MODEL, D_FF)),
            _const_spec((D_FF, D_MODEL)),
            _const_spec((D_MODEL, D_MODEL)),
            _const_spec((PLE_DIM, D_MODEL)),
        ],
        out_specs=[
            pl.BlockSpec((tile_m, D_MODEL), lambda t: (mlp_tile(t), 0)),
            state_spec,
            state_spec,
        ],
        out_shape=[
            jax.ShapeDtypeStruct((b * s, D_MODEL), F32),
            jax.ShapeDtypeStruct((b, WINDOW, KV_DIM), F32),
            jax.ShapeDtypeStruct((b, WINDOW, KV_DIM), F32),
        ],
        scratch_shapes=[
            pltpu.VMEM((tile_m, D_MODEL), BF16),
            pltpu.VMEM((tile_m + WINDOW, KV_DIM), BF16),
            pltpu.VMEM((tile_m + WINDOW, KV_DIM), BF16),
            pltpu.VMEM((tile_m, D_MODEL), BF16),
            pltpu.VMEM((2, tile_m, D_MODEL), F32),
        ],
        compiler_params=pltpu.CompilerParams(
            dimension_semantics=("arbitrary",), vmem_limit_bytes=VMEM_LIMIT),
        name="attn_mlp_prompt",
    )(x.reshape(b * s, D_MODEL), p3d, lw["vec"], bias, lw["wq"], lw["wkv"], lw["wo"],
      lw["w1"], lw["w2"], lw["gate"], lw["proj"])
    return y.reshape(b, s, D_MODEL), ks, vs


def _attn_sample_kernel(x_ref, kc_ref, vc_ref, vec_ref, bias_ref, wq_ref, wkv_ref, wo_ref,
                        o_ref, ks_ref, vs_ref, kf_ref, vf_ref, *, streams, steps):
    cache = kc_ref.shape[1]
    keys = cache + steps
    x = x_ref[...]
    h = _rms(x, _row(vec_ref, V_NORM_MIX)).astype(BF16)
    q = _dot(h, wq_ref[...]).reshape(streams, steps, D_MODEL)
    kv = _dot(h, wkv_ref[...])
    kf_ref[:, 0:cache, :] = kc_ref[...]
    vf_ref[:, 0:cache, :] = vc_ref[...]
    kf_ref[:, cache:keys, :] = kv[:, :KV_DIM].reshape(streams, steps, KV_DIM)
    vf_ref[:, cache:keys, :] = kv[:, KV_DIM:].reshape(streams, steps, KV_DIM)
    ks_ref[...] = kf_ref[:, steps:keys, :]
    vs_ref[...] = vf_ref[:, steps:keys, :]

    qs = jnp.concatenate([q[:, :, g * KV_DIM:(g + 1) * KV_DIM] for g in range(GROUP)], axis=1).astype(BF16)
    kf = kf_ref[...].astype(BF16)
    vf = vf_ref[...].astype(BF16)
    out = jnp.zeros((streams, GROUP * steps, KV_DIM), F32)
    for kh in range(N_KV_HEADS):
        k_kh = jnp.where(_kv_lane_mask(kf.shape, kh), kf, jnp.zeros_like(kf))
        logits = jnp.einsum("bqd,bkd->bqk", qs, k_kh, preferred_element_type=F32)
        logits = logits + bias_ref[kh, :, 0:keys][None]
        probs = _sink_softmax(logits, bias_ref[kh, :, keys:keys + 1][None])
        o_kh = jnp.einsum("bqk,bkd->bqd", probs.astype(BF16), vf, preferred_element_type=F32)
        out = jnp.where(_kv_lane_mask(out.shape, kh), o_kh, out)
    att = jnp.concatenate([out[:, g * steps:(g + 1) * steps, :] for g in range(GROUP)], axis=2)
    o_ref[...] = x + _dot(att.reshape(streams * steps, D_MODEL).astype(BF16), wo_ref[...])


def _attn_sample(x2d, k_left, v_left, lw, bias, streams, steps):
    rows = streams * steps
    cache = k_left.shape[1]
    return pl.pallas_call(
        functools.partial(_attn_sample_kernel, streams=streams, steps=steps),
        grid=(1,),
        in_specs=[
            _const_spec((rows, D_MODEL)),
            _const_spec(k_left.shape),
            _const_spec(v_left.shape),
            _const_spec((VEC_ROWS, D_MODEL)),
            _const_spec(bias.shape),
            _const_spec((D_MODEL, D_MODEL)),
            _const_spec((D_MODEL, 2 * KV_DIM)),
            _const_spec((D_MODEL, D_MODEL)),
        ],
        out_specs=[
            pl.BlockSpec((rows, D_MODEL), lambda t: (0, 0)),
            pl.BlockSpec((streams, cache, KV_DIM), lambda t: (0, 0, 0)),
            pl.BlockSpec((streams, cache, KV_DIM), lambda t: (0, 0, 0)),
        ],
        out_shape=[
            jax.ShapeDtypeStruct((rows, D_MODEL), F32),
            jax.ShapeDtypeStruct((streams, cache, KV_DIM), F32),
            jax.ShapeDtypeStruct((streams, cache, KV_DIM), F32),
        ],
        scratch_shapes=[
            pltpu.VMEM((streams, cache + steps, KV_DIM), F32),
            pltpu.VMEM((streams, cache + steps, KV_DIM), F32),
        ],
        compiler_params=pltpu.CompilerParams(
            dimension_semantics=("arbitrary",), vmem_limit_bytes=VMEM_LIMIT),
        name="attn_sample",
    )(x2d, k_left, v_left, lw["vec"], bias, lw["wq"], lw["wkv"], lw["wo"])


def _t5_bucket(rel):
    nb = NUM_BUCKETS // 2
    max_exact = nb // 2
    ret = (rel > 0).astype(jnp.int32) * nb
    n = jnp.abs(rel)
    nf = jnp.maximum(n, 1).astype(F32)
    large = max_exact + (jnp.log(nf / max_exact) / math.log(MAX_DISTANCE / max_exact)
                         * (nb - max_exact)).astype(jnp.int32)
    large = jnp.minimum(large, nb - 1)
    return ret + jnp.where(n < max_exact, n, large)


def _bias_table(rel_bias, sinks, q_len, k_len, offset):
    rel = jnp.arange(k_len, dtype=jnp.int32)[None, :] - offset - jnp.arange(q_len, dtype=jnp.int32)[:, None]
    onehot = (_t5_bucket(rel)[:, :, None] == jnp.arange(NUM_BUCKETS, dtype=jnp.int32)).astype(F32)
    b = jnp.einsum("qkn,nh->hqk", onehot, rel_bias.astype(F32), precision=lax.Precision.HIGHEST)
    b = b.reshape(N_KV_HEADS, GROUP * q_len, k_len)
    s = jnp.repeat(sinks.astype(F32).reshape(N_KV_HEADS, GROUP), q_len, axis=1)[:, :, None]
    pad = jnp.full((N_KV_HEADS, GROUP * q_len, KEYS_EXT - k_len - 1), NEG_INF, F32)
    return jnp.concatenate([b, s, pad], axis=2)


def _pack_vec(rows):
    zero = jnp.zeros((D_MODEL,), F32)
    return jnp.stack([rows[i].astype(F32) if i in rows else zero for i in range(VEC_ROWS)])


def kernel(x_prompt, x_sample, cache_conv, cache_k, cache_v, p_prompt, p_sample, rel_bias, norm_mix, norm_mlp, norm_ple, norm_final, conv_pw1, conv_pw1_b, conv_dw, conv_dw_b, conv_ln_g, conv_ln_b, conv_pw2, conv_pw2_b, attn_wq, attn_wk, attn_wv, attn_wo, attn_sinks, mlp_w1, mlp_w2, ple_proj, ple_gate):
    batch, seq, _ = x_prompt.shape
    streams, steps, _ = x_sample.shape
    cache = cache_k.shape[2]
    assert seq % TILE_M == 0 and TILE_M % CHUNK == 0 and TILE_M >= WINDOW and steps % SUBLANES == 0

    layers = []
    for i in range(DEPTH):
        j = i // 2
        rows = {V_NORM_MIX: norm_mix[i], V_NORM_MLP: norm_mlp[i], V_NORM_PLE: norm_ple[i], V_NORM_FINAL: norm_final}
        lw = {"w1": mlp_w1[i].astype(BF16), "w2": mlp_w2[i].astype(BF16),
              "gate": ple_gate[i].astype(BF16), "proj": ple_proj[i].astype(BF16)}
        if i % 2 == 0:
            rows.update({V_PW1_B_LO: conv_pw1_b[j, :D_MODEL], V_PW1_B_HI: conv_pw1_b[j, D_MODEL:],
                         V_DW_B: conv_dw_b[j], V_LN_G: conv_ln_g[j], V_LN_B: conv_ln_b[j], V_PW2_B: conv_pw2_b[j]})
            lw["pw1"] = conv_pw1[j].astype(BF16)
            lw["pw2"] = conv_pw2[j].astype(BF16)
            lw["dw"] = jnp.concatenate([conv_dw[j].astype(F32), jnp.zeros((1, D_MODEL), F32)], axis=0)
            lw["dwb"] = jnp.broadcast_to(conv_dw[j].astype(F32)[:, None, :], (CONV_WIDTH, SUBLANES, D_MODEL))
        else:
            wq = (attn_wq[j] * (HEAD_DIM ** -0.5)).reshape(D_MODEL, N_KV_HEADS, GROUP, HEAD_DIM).transpose(0, 2, 1, 3)
            lw["wq"] = wq.reshape(D_MODEL, D_MODEL).astype(BF16)
            lw["wkv"] = jnp.concatenate([attn_wk[j], attn_wv[j]], axis=1).astype(BF16)
            wo = attn_wo[j].reshape(N_KV_HEADS, GROUP, HEAD_DIM, D_MODEL).transpose(1, 0, 2, 3)
            lw["wo"] = wo.reshape(D_MODEL, D_MODEL).astype(BF16)
            lw["bias_prompt"] = _bias_table(rel_bias, attn_sinks[j], CHUNK, BAND, WINDOW).reshape(N_HEADS * CHUNK, 256)
            lw["bias_sample"] = _bias_table(rel_bias, attn_sinks[j], steps, cache + steps, cache)
        lw["vec"] = _pack_vec(rows)
        layers.append(lw)

    p_prompt2 = p_prompt.reshape(DEPTH, batch * seq, PLE_DIM)
    p_sample2 = p_sample.reshape(DEPTH, streams * steps, PLE_DIM)

    xp = x_prompt
    xs = x_sample.reshape(streams * steps, D_MODEL)
    conv_p, k_p, v_p, conv_s, k_s, v_s = [], [], [], [], [], []
    for i, lw in enumerate(layers):
        j = i // 2
        final = i == DEPTH - 1
        if i % 2 == 0:
            xp, cs = _conv_mlp_prompt(xp, p_prompt2, i, lw, TILE_M)
            conv_p.append(cs)
            xs, cs = _conv_sample(xs, cache_conv[j], lw, streams, steps)
            conv_s.append(cs)
        else:
            xp, ks, vs = _attn_mlp_prompt(xp, p_prompt2, i, lw, lw["bias_prompt"], TILE_M, final)
            k_p.append(ks)
            v_p.append(vs)
            xs, ks, vs = _attn_sample(xs, cache_k[j].reshape(streams, cache, KV_DIM),
                                      cache_v[j].reshape(streams, cache, KV_DIM), lw, lw["bias_sample"], streams, steps)
            k_s.append(ks)
            v_s.append(vs)
        xs = _mlp_ple(xs, p_sample2, i, lw, streams * steps, final)

    def heads(ts, n):
        return jnp.stack(ts).reshape(len(ts), n, -1, N_KV_HEADS, HEAD_DIM)

    return (xp, xs.reshape(streams, steps, D_MODEL), jnp.stack(conv_p), heads(k_p, batch), heads(v_p, batch),
            jnp.stack(conv_s), heads(k_s, streams), heads(v_s, streams))
```

```python
import functools
import math

import jax
import jax.numpy as jnp
from jax import lax
from jax.experimental import pallas as pl
from jax.experimental.pallas import tpu as pltpu

D_MODEL = 1024
DEPTH = 4
CHUNK = 64
CONV_WIDTH = 31
WINDOW = 128
N_HEADS = 16
N_KV_HEADS = 4
GROUP = N_HEADS // N_KV_HEADS
HEAD_DIM = 64
KV_DIM = N_KV_HEADS * HEAD_DIM
D_FF = 4 * D_MODEL
PLE_DIM = 256
NUM_BUCKETS = 32
MAX_DISTANCE = 128
EPS = 1e-6
NEG_INF = -1e30

BAND = WINDOW + CHUNK
KEYS_EXT = 256
HALO = 32
SUBLANES = 8
LANES = 128
CONV_ROWS = 64
FF_CHUNK = 1024
MXU_COLS = 256
LEAD_WEIGHT = 2
TILE_M = 512
VMEM_LIMIT = 60 * 1024 * 1024

V_NORM_MIX, V_PW1_B_LO, V_PW1_B_HI, V_DW_B, V_LN_G, V_LN_B, V_PW2_B, V_NORM_MLP, V_NORM_PLE, V_NORM_FINAL = range(10)
VEC_ROWS = 16

F32 = jnp.float32
BF16 = jnp.bfloat16


def _dot(a, b):
    return jnp.dot(a, b, preferred_element_type=F32)


def _sigmoid(x):
    return 1.0 / (1.0 + jnp.exp(-x))


def _rms(x, g):
    return x * lax.rsqrt(jnp.mean(x * x, axis=-1, keepdims=True) + EPS) * g


def _row(vec_ref, i):
    return vec_ref[i:i + 1, :]


def _layernorm_silu(c, g, b):
    mu = jnp.mean(c, axis=-1, keepdims=True)
    d = c - mu
    var = jnp.mean(d * d, axis=-1, keepdims=True)
    y = d * lax.rsqrt(var + EPS) * g + b
    return y * _sigmoid(y)


def _sink_softmax(logits, sink):
    m = jnp.maximum(jnp.max(logits, axis=-1, keepdims=True), sink)
    e = jnp.exp(logits - m)
    return e / (jnp.sum(e, axis=-1, keepdims=True) + jnp.exp(sink - m))


def _kv_lane_mask(shape, kh):
    lane = lax.broadcasted_iota(jnp.int32, shape, len(shape) - 1)
    return (lane >= kh * HEAD_DIM) & (lane < (kh + 1) * HEAD_DIM)


def _mlp_ple_rows(x, p, vec_ref, w1_ref, w2_ref, gate_ref, proj_ref, final, filler=(), lead=(), tail=()):
    filler = list(filler)
    pieces = D_MODEL // MXU_COLS
    n_pieces = (D_FF // FF_CHUNK) * (FF_CHUNK // MXU_COLS + pieces) + LEAD_WEIGHT * len(lead)
    done = [0, 0]

    def fill(weight=1):
        done[0] += weight
        while done[1] < len(filler) * done[0] // n_pieces:
            filler[done[1]]()
            done[1] += 1

    for piece in lead:
        piece()
        fill(LEAD_WEIGHT)
    h = _rms(x, _row(vec_ref, V_NORM_MLP)).astype(BF16)
    xs = [x[:, n * MXU_COLS:(n + 1) * MXU_COLS] for n in range(pieces)]
    for c in range(D_FF // FF_CHUNK):
        hid = []
        for n in range(FF_CHUNK // MXU_COLS):
            lo = c * FF_CHUNK + n * MXU_COLS
            a = _dot(h, w1_ref[:, lo:lo + MXU_COLS])
            hid.append(jnp.square(jnp.maximum(a, 0.0)).astype(BF16))
            fill()
        hid = jnp.concatenate(hid, axis=1)
        for n in range(pieces):
            xs[n] = xs[n] + _dot(hid, w2_ref[c * FF_CHUNK:(c + 1) * FF_CHUNK, n * MXU_COLS:(n + 1) * MXU_COLS])
            fill()
    assert done == [n_pieces, len(filler)]
    x = jnp.concatenate(xs, axis=1)
    for thunk in tail:
        thunk()
    hg = _rms(x, _row(vec_ref, V_NORM_PLE)).astype(BF16)
    gate = _sigmoid(_dot(hg, gate_ref[...]))
    x = x + _dot(p.astype(BF16), proj_ref[...]) * gate
    if final:
        x = _rms(x, _row(vec_ref, V_NORM_FINAL))
    return x


def _mlp_ple_kernel(x_ref, p_ref, vec_ref, w1_ref, w2_ref, gate_ref, proj_ref, o_ref, *, final):
    o_ref[...] = _mlp_ple_rows(x_ref[...], p_ref[...], vec_ref, w1_ref, w2_ref, gate_ref, proj_ref, final)


def _const_spec(shape):
    nd = len(shape)
    return pl.BlockSpec(shape, lambda *_: (0,) * nd, pipeline_mode=pl.Buffered(1))


def _mlp_ple(x2d, p4d, layer, lw, tile_m, final):
    rows = x2d.shape[0]
    return pl.pallas_call(
        functools.partial(_mlp_ple_kernel, final=final),
        grid=(rows // tile_m,),
        in_specs=[
            pl.BlockSpec((tile_m, D_MODEL), lambda t: (t, 0)),
            pl.BlockSpec((None, tile_m, PLE_DIM), lambda t: (layer, t, 0)),
            _const_spec((VEC_ROWS, D_MODEL)),
            _const_spec((D_MODEL, D_FF)),
            _const_spec((D_FF, D_MODEL)),
            _const_spec((D_MODEL, D_MODEL)),
            _const_spec((PLE_DIM, D_MODEL)),
        ],
        out_specs=pl.BlockSpec((tile_m, D_MODEL), lambda t: (t, 0)),
        out_shape=jax.ShapeDtypeStruct((rows, D_MODEL), F32),
        compiler_params=pltpu.CompilerParams(
            dimension_semantics=("arbitrary",), vmem_limit_bytes=VMEM_LIMIT),
        name=f"mlp_ple_l{layer}_r{rows}",
    )(x2d, p4d, lw["vec"], lw["w1"], lw["w2"], lw["gate"], lw["proj"])


def _glu(x, vec_ref, pw1_ref):
    h = _rms(x, _row(vec_ref, V_NORM_MIX)).astype(BF16)
    a = _dot(h, pw1_ref[:, :D_MODEL]) + _row(vec_ref, V_PW1_B_LO)
    g = _dot(h, pw1_ref[:, D_MODEL:]) + _row(vec_ref, V_PW1_B_HI)
    return a * _sigmoid(g)


def _depthwise_unit(ext_ref, dwb_ref, c_ref, r0, l):
    first = HALO - (CONV_WIDTH - 1)
    groups = CONV_ROWS // SUBLANES
    spans = -(-(first + CONV_WIDTH) // SUBLANES)
    lanes = slice(l * LANES, (l + 1) * LANES)
    win = [ext_ref[r0 + SUBLANES * g:r0 + SUBLANES * (g + 1), lanes] for g in range(groups + spans)]
    out = None
    for s in range(SUBLANES):
        taps = [(j, SUBLANES * j + s - first) for j in range(spans)
                if 0 <= SUBLANES * j + s - first < CONV_WIDTH]
        phase = []
        for g in range(groups + (1 if s else 0)):
            p = None
            for j, k in taps:
                term = dwb_ref[k, :, lanes] * win[g + j]
                p = term if p is None else p + term
            phase.append(p)
        shifted = jnp.concatenate(phase, axis=0)[s:s + CONV_ROWS, :]
        out = shifted if out is None else out + shifted
    c_ref[r0:r0 + CONV_ROWS, lanes] = out


def _conv_mlp_prompt_kernel(x_ref, p_ref, vec_ref, dwb_ref, pw1_ref, pw2_ref, w1_ref, w2_ref, gate_ref, proj_ref,
                            o_ref, cs_ref, ext_ref, c_ref, act_ref, mid_ref, *, tile_m, tiles_per_stream):
    t = pl.program_id(0)
    ext_rows = tile_m + HALO
    slot = t % 2

    @pl.when(t % tiles_per_stream == 0)
    def _():
        ext_ref[0:HALO, :] = jnp.zeros((HALO, D_MODEL), F32)
        ext_ref[ext_rows:ext_rows + SUBLANES, :] = jnp.zeros((SUBLANES, D_MODEL), F32)

    @pl.when(t == 0)
    def _():
        mid_ref[1] = jnp.zeros((tile_m, D_MODEL), F32)

    x = x_ref[...]
    h = _rms(x, _row(vec_ref, V_NORM_MIX)).astype(BF16)

    def glu_piece(n):
        cols = slice(n * MXU_COLS, (n + 1) * MXU_COLS)
        gcols = slice(D_MODEL + n * MXU_COLS, D_MODEL + (n + 1) * MXU_COLS)
        a = _dot(h, pw1_ref[:, cols]) + vec_ref[V_PW1_B_LO:V_PW1_B_LO + 1, cols]
        g = _dot(h, pw1_ref[:, gcols]) + vec_ref[V_PW1_B_HI:V_PW1_B_HI + 1, cols]
        ext_ref[HALO:ext_rows, cols] = a * _sigmoid(g)

    glu_piece(0)
    conv_units = [functools.partial(_depthwise_unit, ext_ref, dwb_ref, c_ref, r * CONV_ROWS, l)
                  for l in range(D_MODEL // LANES) for r in range(tile_m // CONV_ROWS)]
    pieces = D_MODEL // MXU_COLS

    def norm_part(i):
        rows = slice(i * tile_m // pieces, (i + 1) * tile_m // pieces)
        act = _layernorm_silu(c_ref[rows, :] + _row(vec_ref, V_DW_B), _row(vec_ref, V_LN_G), _row(vec_ref, V_LN_B))
        act_ref[rows, :] = act.astype(BF16)

    def project():
        mid_ref[slot] = x + _dot(act_ref[...], pw2_ref[...]) + _row(vec_ref, V_PW2_B)

    o_ref[...] = _mlp_ple_rows(mid_ref[1 - slot], p_ref[...], vec_ref, w1_ref, w2_ref, gate_ref, proj_ref, False,
                               filler=conv_units + [functools.partial(norm_part, i) for i in range(pieces)],
                               lead=[functools.partial(glu_piece, n) for n in range(1, pieces)],
                               tail=[project])
    cs_ref[0] = ext_ref[ext_rows - (CONV_WIDTH - 1):ext_rows, :]

    @pl.when(t >= 0)
    def _():
        ext_ref[0:HALO, :] = ext_ref[tile_m:ext_rows, :]


def _conv_mlp_prompt(x, p3d, layer, lw, tile_m):
    b, s, _ = x.shape
    tiles_per_stream = s // tile_m
    n_tiles = b * tiles_per_stream
    mixer_tile = lambda t: jnp.minimum(t, n_tiles - 1)
    mlp_tile = lambda t: jnp.maximum(t - 1, 0)
    y, cs = pl.pallas_call(
        functools.partial(_conv_mlp_prompt_kernel, tile_m=tile_m, tiles_per_stream=tiles_per_stream),
        grid=(n_tiles + 1,),
        in_specs=[
            pl.BlockSpec((tile_m, D_MODEL), lambda t: (mixer_tile(t), 0)),
            pl.BlockSpec((None, tile_m, PLE_DIM), lambda t: (layer, mlp_tile(t), 0)),
            _const_spec((VEC_ROWS, D_MODEL)),
            _const_spec((CONV_WIDTH, SUBLANES, D_MODEL)),
            _const_spec((D_MODEL, 2 * D_MODEL)),
            _const_spec((D_MODEL, D_MODEL)),
            _const_spec((D_MODEL, D_FF)),
            _const_spec((D_FF, D_MODEL)),
            _const_spec((D_MODEL, D_MODEL)),
            _const_spec((PLE_DIM, D_MODEL)),
        ],
        out_specs=[
            pl.BlockSpec((tile_m, D_MODEL), lambda t: (mlp_tile(t), 0)),
            pl.BlockSpec((1, CONV_WIDTH - 1, D_MODEL), lambda t: (mixer_tile(t) // tiles_per_stream, 0, 0)),
        ],
        out_shape=[
            jax.ShapeDtypeStruct((b * s, D_MODEL), F32),
            jax.ShapeDtypeStruct((b, CONV_WIDTH - 1, D_MODEL), F32),
        ],
        scratch_shapes=[
            pltpu.VMEM((tile_m + HALO + SUBLANES, D_MODEL), F32),
            pltpu.VMEM((tile_m, D_MODEL), F32),
            pltpu.VMEM((tile_m, D_MODEL), BF16),
            pltpu.VMEM((2, tile_m, D_MODEL), F32),
        ],
        compiler_params=pltpu.CompilerParams(
            dimension_semantics=("arbitrary",), vmem_limit_bytes=VMEM_LIMIT),
        name="conv_mlp_prompt",
    )(x.reshape(b * s, D_MODEL), p3d, lw["vec"], lw["dwb"], lw["pw1"], lw["pw2"],
      lw["w1"], lw["w2"], lw["gate"], lw["proj"])
    return y.reshape(b, s, D_MODEL), cs


def _conv_sample_kernel(x_ref, left_ref, vec_ref, dw_ref, pw1_ref, pw2_ref, o_ref, cs_ref, ext_ref, *, streams, steps):
    x = x_ref[...]
    ext_ref[:, 0:HALO - (CONV_WIDTH - 1), :] = jnp.zeros((streams, HALO - (CONV_WIDTH - 1), D_MODEL), F32)
    ext_ref[:, HALO - (CONV_WIDTH - 1):HALO, :] = left_ref[...]
    ext_ref[:, HALO:HALO + steps, :] = _glu(x, vec_ref, pw1_ref).reshape(streams, steps, D_MODEL)
    acc = jnp.broadcast_to(_row(vec_ref, V_DW_B).reshape(1, 1, D_MODEL), (streams, steps, D_MODEL))
    for k in range(CONV_WIDTH):
        lo = k + HALO - (CONV_WIDTH - 1)
        acc = acc + dw_ref[k:k + 1, :].reshape(1, 1, D_MODEL) * ext_ref[:, lo:lo + steps, :]
    act = _layernorm_silu(acc.reshape(streams * steps, D_MODEL), _row(vec_ref, V_LN_G), _row(vec_ref, V_LN_B))
    cs_ref[...] = ext_ref[:, HALO + steps - (CONV_WIDTH - 1):HALO + steps, :]
    o_ref[...] = x + _dot(act.astype(BF16), pw2_ref[...]) + _row(vec_ref, V_PW2_B)


def _conv_sample(x2d, left, lw, streams, steps):
    rows = streams * steps
    return pl.pallas_call(
        functools.partial(_conv_sample_kernel, streams=streams, steps=steps),
        grid=(1,),
        in_specs=[
            _const_spec((rows, D_MODEL)),
            _const_spec((streams, CONV_WIDTH - 1, D_MODEL)),
            _const_spec((VEC_ROWS, D_MODEL)),
            _const_spec((CONV_WIDTH + 1, D_MODEL)),
            _const_spec((D_MODEL, 2 * D_MODEL)),
            _const_spec((D_MODEL, D_MODEL)),
        ],
        out_specs=[
            pl.BlockSpec((rows, D_MODEL), lambda t: (0, 0)),
            pl.BlockSpec((streams, CONV_WIDTH - 1, D_MODEL), lambda t: (0, 0, 0)),
        ],
        out_shape=[
            jax.ShapeDtypeStruct((rows, D_MODEL), F32),
            jax.ShapeDtypeStruct((streams, CONV_WIDTH - 1, D_MODEL), F32),
        ],
        scratch_shapes=[pltpu.VMEM((streams, HALO + steps, D_MODEL), F32)],
        compiler_params=pltpu.CompilerParams(
            dimension_semantics=("arbitrary",), vmem_limit_bytes=VMEM_LIMIT),
        name="conv_sample",
    )(x2d, left, lw["vec"], lw["dw"], lw["pw1"], lw["pw2"])


def _attn_mlp_prompt_kernel(x_ref, p_ref, vec_ref, bias_ref, wq_ref, wkv_ref, wo_ref, w1_ref, w2_ref, gate_ref,
                            proj_ref, o_ref, ks_ref, vs_ref, q_ref, kb_ref, vb_ref, att_ref, mid_ref,
                            *, tile_m, tiles_per_stream, final):
    t = pl.program_id(0)
    slot = t % 2
    tile_in_stream = t % tiles_per_stream

    @pl.when(tile_in_stream == 0)
    def _():
        kb_ref[0:WINDOW, :] = jnp.zeros((WINDOW, KV_DIM), BF16)
        vb_ref[0:WINDOW, :] = jnp.zeros((WINDOW, KV_DIM), BF16)

    @pl.when(t == 0)
    def _():
        mid_ref[1] = jnp.zeros((tile_m, D_MODEL), F32)

    x = x_ref[...]
    h = _rms(x, _row(vec_ref, V_NORM_MIX)).astype(BF16)
    q_ref[...] = _dot(h, wq_ref[...]).astype(BF16)
    kv = _dot(h, wkv_ref[...])
    k_new = kv[:, :KV_DIM]
    v_new = kv[:, KV_DIM:]
    kb_ref[WINDOW:WINDOW + tile_m, :] = k_new.astype(BF16)
    vb_ref[WINDOW:WINDOW + tile_m, :] = v_new.astype(BF16)
    ks_ref[0] = k_new[tile_m - WINDOW:, :]
    vs_ref[0] = v_new[tile_m - WINDOW:, :]

    rows = GROUP * CHUNK
    pad = jnp.zeros((KEYS_EXT - BAND, KV_DIM), BF16)
    weights = {}

    def chunk_logits(c0):
        q = jnp.concatenate(
            [q_ref[c0:c0 + CHUNK, g * KV_DIM:(g + 1) * KV_DIM] for g in range(GROUP)], axis=0)
        qm = jnp.concatenate(
            [jnp.where(_kv_lane_mask(q.shape, kh), q, jnp.zeros_like(q)) for kh in range(N_KV_HEADS)], axis=0)
        kb = jnp.concatenate([kb_ref[c0:c0 + BAND, :], pad], axis=0)
        logits = lax.dot_general(qm, kb, (((1,), (1,)), ((), ())), preferred_element_type=F32)
        logits = logits + bias_ref[...]
        key_pos = tile_in_stream * tile_m + c0 - WINDOW + lax.broadcasted_iota(jnp.int32, (1, KEYS_EXT), 1)
        logits = jnp.where(key_pos >= 0, logits, NEG_INF)
        weights[c0] = jnp.exp(logits - jnp.max(logits, axis=-1, keepdims=True)).astype(BF16)

    def chunk_values(c0):
        e = weights.pop(c0)
        vb = vb_ref[c0:c0 + BAND, :]
        half = LANES // HEAD_DIM
        krow = lax.broadcasted_iota(jnp.int32, (KEYS_EXT, LANES), 0)
        klane = lax.broadcasted_iota(jnp.int32, (KEYS_EXT, LANES), 1)
        outs = []
        for pair in range(N_KV_HEADS // half):
            num = den = None
            for sub in range(half):
                kh = pair * half + sub
                in_head = (klane >= sub * HEAD_DIM) & (klane < (sub + 1) * HEAD_DIM)
                v_kh = jnp.concatenate([vb[:, pair * LANES:(pair + 1) * LANES], pad[:, 0:LANES]], axis=0)
                v_kh = jnp.where(in_head, v_kh, jnp.zeros_like(v_kh))
                ones = jnp.where(in_head & (krow <= BAND), 1.0, 0.0).astype(BF16)
                pv = _dot(e[kh * rows:(kh + 1) * rows, :], jnp.concatenate([v_kh, ones], axis=1))
                num = pv[:, 0:LANES] if num is None else num + pv[:, 0:LANES]
                den = pv[:, LANES:] if den is None else den + pv[:, LANES:]
            outs.append(num / den)
        out = jnp.concatenate(outs, axis=1)
        for g in range(GROUP):
            att_ref[c0:c0 + CHUNK, g * KV_DIM:(g + 1) * KV_DIM] = out[g * CHUNK:(g + 1) * CHUNK, :].astype(BF16)

    chunk_units = []
    for c0 in range(0, tile_m, CHUNK):
        chunk_units += [functools.partial(chunk_logits, c0), functools.partial(chunk_values, c0)]
    o_ref[...] = _mlp_ple_rows(mid_ref[1 - slot], p_ref[...], vec_ref, w1_ref, w2_ref, gate_ref, proj_ref, final,
                               filler=chunk_units)

    mid_ref[slot] = x + _dot(att_ref[...], wo_ref[...])

    @pl.when(t >= 0)
    def _():
        kb_ref[0:WINDOW, :] = kb_ref[tile_m:tile_m + WINDOW, :]
        vb_ref[0:WINDOW, :] = vb_ref[tile_m:tile_m + WINDOW, :]


def _attn_mlp_prompt(x, p3d, layer, lw, bias, tile_m, final):
    b, s, _ = x.shape
    tiles_per_stream = s // tile_m
    n_tiles = b * tiles_per_stream
    mixer_tile = lambda t: jnp.minimum(t, n_tiles - 1)
    mlp_tile = lambda t: jnp.maximum(t - 1, 0)
    state_spec = pl.BlockSpec((1, WINDOW, KV_DIM), lambda t: (mixer_tile(t) // tiles_per_stream, 0, 0))
    y, ks, vs = pl.pallas_call(
        functools.partial(_attn_mlp_prompt_kernel, tile_m=tile_m, tiles_per_stream=tiles_per_stream, final=final),
        grid=(n_tiles + 1,),
        in_specs=[
            pl.BlockSpec((tile_m, D_MODEL), lambda t: (mixer_tile(t), 0)),
            pl.BlockSpec((None, tile_m, PLE_DIM), lambda t: (layer, mlp_tile(t), 0)),
            _const_spec((VEC_ROWS, D_MODEL)),
            _const_spec(bias.shape),
            _const_spec((D_MODEL, D_MODEL)),
            _const_spec((D_MODEL, 2 * KV_DIM)),
            _const_spec((D_MODEL, D_MODEL)),
            _const_spec((D_MODEL, D_FF)),
            _const_spec((D_FF, D_MODEL)),
            _const_spec((D_MODEL, D_MODEL)),
            _const_spec((PLE_DIM, D_MODEL)),
        ],
        out_specs=[
            pl.BlockSpec((tile_m, D_MODEL), lambda t: (mlp_tile(t), 0)),
            state_spec,
            state_spec,
        ],
        out_shape=[
            jax.ShapeDtypeStruct((b * s, D_MODEL), F32),
            jax.ShapeDtypeStruct((b, WINDOW, KV_DIM), F32),
            jax.ShapeDtypeStruct((b, WINDOW, KV_DIM), F32),
        ],
        scratch_shapes=[
            pltpu.VMEM((tile_m, D_MODEL), BF16),
            pltpu.VMEM((tile_m + WINDOW, KV_DIM), BF16),
            pltpu.VMEM((tile_m + WINDOW, KV_DIM), BF16),
            pltpu.VMEM((tile_m, D_MODEL), BF16),
            pltpu.VMEM((2, tile_m, D_MODEL), F32),
        ],
        compiler_params=pltpu.CompilerParams(
            dimension_semantics=("arbitrary",), vmem_limit_bytes=VMEM_LIMIT),
        name="attn_mlp_prompt",
    )(x.reshape(b * s, D_MODEL), p3d, lw["vec"], bias, lw["wq"], lw["wkv"], lw["wo"],
      lw["w1"], lw["w2"], lw["gate"], lw["proj"])
    return y.reshape(b, s, D_MODEL), ks, vs


def _attn_sample_kernel(x_ref, kc_ref, vc_ref, vec_ref, bias_ref, wq_ref, wkv_ref, wo_ref,
                        o_ref, ks_ref, vs_ref, kf_ref, vf_ref, *, streams, steps):
    cache = kc_ref.shape[1]
    keys = cache + steps
    x = x_ref[...]
    h = _rms(x, _row(vec_ref, V_NORM_MIX)).astype(BF16)
    q = _dot(h, wq_ref[...]).reshape(streams, steps, D_MODEL)
    kv = _dot(h, wkv_ref[...])
    kf_ref[:, 0:cache, :] = kc_ref[...]
    vf_ref[:, 0:cache, :] = vc_ref[...]
    kf_ref[:, cache:keys, :] = kv[:, :KV_DIM].reshape(streams, steps, KV_DIM)
    vf_ref[:, cache:keys, :] = kv[:, KV_DIM:].reshape(streams, steps, KV_DIM)
    ks_ref[...] = kf_ref[:, steps:keys, :]
    vs_ref[...] = vf_ref[:, steps:keys, :]

    qs = jnp.concatenate([q[:, :, g * KV_DIM:(g + 1) * KV_DIM] for g in range(GROUP)], axis=1).astype(BF16)
    kf = kf_ref[...].astype(BF16)
    vf = vf_ref[...].astype(BF16)
    out = jnp.zeros((streams, GROUP * steps, KV_DIM), F32)
    for kh in range(N_KV_HEADS):
        k_kh = jnp.where(_kv_lane_mask(kf.shape, kh), kf, jnp.zeros_like(kf))
        logits = jnp.einsum("bqd,bkd->bqk", qs, k_kh, preferred_element_type=F32)
        logits = logits + bias_ref[kh, :, 0:keys][None]
        probs = _sink_softmax(logits, bias_ref[kh, :, keys:keys + 1][None])
        o_kh = jnp.einsum("bqk,bkd->bqd", probs.astype(BF16), vf, preferred_element_type=F32)
        out = jnp.where(_kv_lane_mask(out.shape, kh), o_kh, out)
    att = jnp.concatenate([out[:, g * steps:(g + 1) * steps, :] for g in range(GROUP)], axis=2)
    o_ref[...] = x + _dot(att.reshape(streams * steps, D_MODEL).astype(BF16), wo_ref[...])


def _attn_sample(x2d, k_left, v_left, lw, bias, streams, steps):
    rows = streams * steps
    cache = k_left.shape[1]
    return pl.pallas_call(
        functools.partial(_attn_sample_kernel, streams=streams, steps=steps),
        grid=(1,),
        in_specs=[
            _const_spec((rows, D_MODEL)),
            _const_spec(k_left.shape),
            _const_spec(v_left.shape),
            _const_spec((VEC_ROWS, D_MODEL)),
            _const_spec(bias.shape),
            _const_spec((D_MODEL, D_MODEL)),
            _const_spec((D_MODEL, 2 * KV_DIM)),
            _const_spec((D_MODEL, D_MODEL)),
        ],
        out_specs=[
            pl.BlockSpec((rows, D_MODEL), lambda t: (0, 0)),
            pl.BlockSpec((streams, cache, KV_DIM), lambda t: (0, 0, 0)),
            pl.BlockSpec((streams, cache, KV_DIM), lambda t: (0, 0, 0)),
        ],
        out_shape=[
            jax.ShapeDtypeStruct((rows, D_MODEL), F32),
            jax.ShapeDtypeStruct((streams, cache, KV_DIM), F32),
            jax.ShapeDtypeStruct((streams, cache, KV_DIM), F32),
        ],
        scratch_shapes=[
            pltpu.VMEM((streams, cache + steps, KV_DIM), F32),
            pltpu.VMEM((streams, cache + steps, KV_DIM), F32),
        ],
        compiler_params=pltpu.CompilerParams(
            dimension_semantics=("arbitrary",), vmem_limit_bytes=VMEM_LIMIT),
        name="attn_sample",
    )(x2d, k_left, v_left, lw["vec"], bias, lw["wq"], lw["wkv"], lw["wo"])


def _t5_bucket(rel):
    nb = NUM_BUCKETS // 2
    max_exact = nb // 2
    ret = (rel > 0).astype(jnp.int32) * nb
    n = jnp.abs(rel)
    nf = jnp.maximum(n, 1).astype(F32)
    large = max_exact + (jnp.log(nf / max_exact) / math.log(MAX_DISTANCE / max_exact)
                         * (nb - max_exact)).astype(jnp.int32)
    large = jnp.minimum(large, nb - 1)
    return ret + jnp.where(n < max_exact, n, large)


def _bias_table(rel_bias, sinks, q_len, k_len, offset):
    rel = jnp.arange(k_len, dtype=jnp.int32)[None, :] - offset - jnp.arange(q_len, dtype=jnp.int32)[:, None]
    onehot = (_t5_bucket(rel)[:, :, None] == jnp.arange(NUM_BUCKETS, dtype=jnp.int32)).astype(F32)
    b = jnp.einsum("qkn,nh->hqk", onehot, rel_bias.astype(F32), precision=lax.Precision.HIGHEST)
    b = b.reshape(N_KV_HEADS, GROUP * q_len, k_len)
    s = jnp.repeat(sinks.astype(F32).reshape(N_KV_HEADS, GROUP), q_len, axis=1)[:, :, None]
    pad = jnp.full((N_KV_HEADS, GROUP * q_len, KEYS_EXT - k_len - 1), NEG_INF, F32)
    return jnp.concatenate([b, s, pad], axis=2)


def _pack_vec(rows):
    zero = jnp.zeros((D_MODEL,), F32)
    return jnp.stack([rows[i].astype(F32) if i in rows else zero for i in range(VEC_ROWS)])


def kernel(x_prompt, x_sample, cache_conv, cache_k, cache_v, p_prompt, p_sample, rel_bias, norm_mix, norm_mlp, norm_ple, norm_final, conv_pw1, conv_pw1_b, conv_dw, conv_dw_b, conv_ln_g, conv_ln_b, conv_pw2, conv_pw2_b, attn_wq, attn_wk, attn_wv, attn_wo, attn_sinks, mlp_w1, mlp_w2, ple_proj, ple_gate):
    batch, seq, _ = x_prompt.shape
    streams, steps, _ = x_sample.shape
    cache = cache_k.shape[2]
    assert seq % TILE_M == 0 and TILE_M % CHUNK == 0 and TILE_M >= WINDOW and steps % SUBLANES == 0

    layers = []
    for i in range(DEPTH):
        j = i // 2
        rows = {V_NORM_MIX: norm_mix[i], V_NORM_MLP: norm_mlp[i], V_NORM_PLE: norm_ple[i], V_NORM_FINAL: norm_final}
        lw = {"w1": mlp_w1[i].astype(BF16), "w2": mlp_w2[i].astype(BF16),
              "gate": ple_gate[i].astype(BF16), "proj": ple_proj[i].astype(BF16)}
        if i % 2 == 0:
            rows.update({V_PW1_B_LO: conv_pw1_b[j, :D_MODEL], V_PW1_B_HI: conv_pw1_b[j, D_MODEL:],
                         V_DW_B: conv_dw_b[j], V_LN_G: conv_ln_g[j], V_LN_B: conv_ln_b[j], V_PW2_B: conv_pw2_b[j]})
            lw["pw1"] = conv_pw1[j].astype(BF16)
            lw["pw2"] = conv_pw2[j].astype(BF16)
            lw["dw"] = jnp.concatenate([conv_dw[j].astype(F32), jnp.zeros((1, D_MODEL), F32)], axis=0)
            lw["dwb"] = jnp.broadcast_to(conv_dw[j].astype(F32)[:, None, :], (CONV_WIDTH, SUBLANES, D_MODEL))
        else:
            wq = (attn_wq[j] * (HEAD_DIM ** -0.5)).reshape(D_MODEL, N_KV_HEADS, GROUP, HEAD_DIM).transpose(0, 2, 1, 3)
            lw["wq"] = wq.reshape(D_MODEL, D_MODEL).astype(BF16)
            lw["wkv"] = jnp.concatenate([attn_wk[j], attn_wv[j]], axis=1).astype(BF16)
            wo = attn_wo[j].reshape(N_KV_HEADS, GROUP, HEAD_DIM, D_MODEL).transpose(1, 0, 2, 3)
            lw["wo"] = wo.reshape(D_MODEL, D_MODEL).astype(BF16)
            lw["bias_prompt"] = _bias_table(rel_bias, attn_sinks[j], CHUNK, BAND, WINDOW).reshape(N_HEADS * CHUNK, 256)
            lw["bias_sample"] = _bias_table(rel_bias, attn_sinks[j], steps, cache + steps, cache)
        lw["vec"] = _pack_vec(rows)
        layers.append(lw)

    p_prompt2 = p_prompt.reshape(DEPTH, batch * seq, PLE_DIM)
    p_sample2 = p_sample.reshape(DEPTH, streams * steps, PLE_DIM)

    xp = x_prompt
    xs = x_sample.reshape(streams * steps, D_MODEL)
    conv_p, k_p, v_p, conv_s, k_s, v_s = [], [], [], [], [], []
    for i, lw in enumerate(layers):
        j = i // 2
        final = i == DEPTH - 1
        if i % 2 == 0:
            xp, cs = _conv_mlp_prompt(xp, p_prompt2, i, lw, TILE_M)
            conv_p.append(cs)
            xs, cs = _conv_sample(xs, cache_conv[j], lw, streams, steps)
            conv_s.append(cs)
        else:
            xp, ks, vs = _attn_mlp_prompt(xp, p_prompt2, i, lw, lw["bias_prompt"], TILE_M, final)
            k_p.append(ks)
            v_p.append(vs)
            xs, ks, vs = _attn_sample(xs, cache_k[j].reshape(streams, cache, KV_DIM),
                                      cache_v[j].reshape(streams, cache, KV_DIM), lw, lw["bias_sample"], streams, steps)
            k_s.append(ks)
            v_s.append(vs)
        xs = _mlp_ple(xs, p_sample2, i, lw, streams * steps, final)

    def heads(ts, n):
        return jnp.stack(ts).reshape(len(ts), n, -1, N_KV_HEADS, HEAD_DIM)

    return (xp, xs.reshape(streams, steps, D_MODEL), jnp.stack(conv_p), heads(k_p, batch), heads(v_p, batch),
            jnp.stack(conv_s), heads(k_s, streams), heads(v_s, streams))
```

```python
import functools
import math

import jax
import jax.numpy as jnp
from jax import lax
from jax.experimental import pallas as pl
from jax.experimental.pallas import tpu as pltpu

D_MODEL = 1024
DEPTH = 4
CHUNK = 64
CONV_WIDTH = 31
WINDOW = 128
N_HEADS = 16
N_KV_HEADS = 4
GROUP = N_HEADS // N_KV_HEADS
HEAD_DIM = 64
KV_DIM = N_KV_HEADS * HEAD_DIM
D_FF = 4 * D_MODEL
PLE_DIM = 256
NUM_BUCKETS = 32
MAX_DISTANCE = 128
EPS = 1e-6
NEG_INF = -1e30

BAND = WINDOW + CHUNK
KEYS_EXT = 256
HALO = 32
SUBLANES = 8
LANES = 128
CONV_ROWS = 64
FF_CHUNK = 1024
MXU_COLS = 256
LEAD_WEIGHT = 2
TILE_M = 512
VMEM_LIMIT = 60 * 1024 * 1024

V_NORM_MIX, V_PW1_B_LO, V_PW1_B_HI, V_DW_B, V_LN_G, V_LN_B, V_PW2_B, V_NORM_MLP, V_NORM_PLE, V_NORM_FINAL = range(10)
VEC_ROWS = 16

F32 = jnp.float32
BF16 = jnp.bfloat16


def _dot(a, b):
    return jnp.dot(a, b, preferred_element_type=F32)


def _sigmoid(x):
    return 1.0 / (1.0 + jnp.exp(-x))


def _rms(x, g):
    return x * lax.rsqrt(jnp.mean(x * x, axis=-1, keepdims=True) + EPS) * g


def _row(vec_ref, i):
    return vec_ref[i:i + 1, :]


def _layernorm_silu(c, g, b):
    mu = jnp.mean(c, axis=-1, keepdims=True)
    d = c - mu
    var = jnp.mean(d * d, axis=-1, keepdims=True)
    y = d * lax.rsqrt(var + EPS) * g + b
    return y * _sigmoid(y)


def _sink_softmax(logits, sink):
    m = jnp.maximum(jnp.max(logits, axis=-1, keepdims=True), sink)
    e = jnp.exp(logits - m)
    return e / (jnp.sum(e, axis=-1, keepdims=True) + jnp.exp(sink - m))


def _kv_lane_mask(shape, kh):
    lane = lax.broadcasted_iota(jnp.int32, shape, len(shape) - 1)
    return (lane >= kh * HEAD_DIM) & (lane < (kh + 1) * HEAD_DIM)


def _mlp_ple_rows(x, p, vec_ref, w1_ref, w2_ref, gate_ref, proj_ref, final, filler=(), lead=(), tail=()):
    filler = list(filler)
    pieces = D_MODEL // MXU_COLS
    n_pieces = (D_FF // FF_CHUNK) * (FF_CHUNK // MXU_COLS + pieces) + LEAD_WEIGHT * len(lead)
    done = [0, 0]

    def fill(weight=1):
        done[0] += weight
        while done[1] < len(filler) * done[0] // n_pieces:
            filler[done[1]]()
            done[1] += 1

    for piece in lead:
        piece()
        fill(LEAD_WEIGHT)
    h = _rms(x, _row(vec_ref, V_NORM_MLP)).astype(BF16)
    xs = [x[:, n * MXU_COLS:(n + 1) * MXU_COLS] for n in range(pieces)]
    for c in range(D_FF // FF_CHUNK):
        hid = []
        for n in range(FF_CHUNK // MXU_COLS):
            lo = c * FF_CHUNK + n * MXU_COLS
            a = _dot(h, w1_ref[:, lo:lo + MXU_COLS])
            hid.append(jnp.square(jnp.maximum(a, 0.0)).astype(BF16))
            fill()
        hid = jnp.concatenate(hid, axis=1)
        for n in range(pieces):
            xs[n] = xs[n] + _dot(hid, w2_ref[c * FF_CHUNK:(c + 1) * FF_CHUNK, n * MXU_COLS:(n + 1) * MXU_COLS])
            fill()
    assert done == [n_pieces, len(filler)]
    x = jnp.concatenate(xs, axis=1)
    hg = _rms(x, _row(vec_ref, V_NORM_PLE)).astype(BF16)
    if tail:
        assert len(tail) == pieces
        gate = []
        for n in range(pieces):
            gate.append(_dot(hg, gate_ref[:, n * MXU_COLS:(n + 1) * MXU_COLS]))
            tail[n]()
        gate = _sigmoid(jnp.concatenate(gate, axis=1))
    else:
        gate = _sigmoid(_dot(hg, gate_ref[...]))
    x = x + _dot(p.astype(BF16), proj_ref[...]) * gate
    if final:
        x = _rms(x, _row(vec_ref, V_NORM_FINAL))
    return x


def _mlp_ple_kernel(x_ref, p_ref, vec_ref, w1_ref, w2_ref, gate_ref, proj_ref, o_ref, *, final):
    o_ref[...] = _mlp_ple_rows(x_ref[...], p_ref[...], vec_ref, w1_ref, w2_ref, gate_ref, proj_ref, final)


def _const_spec(shape):
    nd = len(shape)
    return pl.BlockSpec(shape, lambda *_: (0,) * nd, pipeline_mode=pl.Buffered(1))


def _mlp_ple(x2d, p4d, layer, lw, tile_m, final):
    rows = x2d.shape[0]
    return pl.pallas_call(
        functools.partial(_mlp_ple_kernel, final=final),
        grid=(rows // tile_m,),
        in_specs=[
            pl.BlockSpec((tile_m, D_MODEL), lambda t: (t, 0)),
            pl.BlockSpec((None, tile_m, PLE_DIM), lambda t: (layer, t, 0)),
            _const_spec((VEC_ROWS, D_MODEL)),
            _const_spec((D_MODEL, D_FF)),
            _const_spec((D_FF, D_MODEL)),
            _const_spec((D_MODEL, D_MODEL)),
            _const_spec((PLE_DIM, D_MODEL)),
        ],
        out_specs=pl.BlockSpec((tile_m, D_MODEL), lambda t: (t, 0)),
        out_shape=jax.ShapeDtypeStruct((rows, D_MODEL), F32),
        compiler_params=pltpu.CompilerParams(
            dimension_semantics=("arbitrary",), vmem_limit_bytes=VMEM_LIMIT),
        name=f"mlp_ple_l{layer}_r{rows}",
    )(x2d, p4d, lw["vec"], lw["w1"], lw["w2"], lw["gate"], lw["proj"])


def _glu(x, vec_ref, pw1_ref):
    h = _rms(x, _row(vec_ref, V_NORM_MIX)).astype(BF16)
    a = _dot(h, pw1_ref[:, :D_MODEL]) + _row(vec_ref, V_PW1_B_LO)
    g = _dot(h, pw1_ref[:, D_MODEL:]) + _row(vec_ref, V_PW1_B_HI)
    return a * _sigmoid(g)


def _depthwise_unit(ext_ref, dwb_ref, c_ref, r0, l):
    first = HALO - (CONV_WIDTH - 1)
    groups = CONV_ROWS // SUBLANES
    spans = -(-(first + CONV_WIDTH) // SUBLANES)
    lanes = slice(l * LANES, (l + 1) * LANES)
    win = [ext_ref[r0 + SUBLANES * g:r0 + SUBLANES * (g + 1), lanes] for g in range(groups + spans)]
    out = None
    for s in range(SUBLANES):
        taps = [(j, SUBLANES * j + s - first) for j in range(spans)
                if 0 <= SUBLANES * j + s - first < CONV_WIDTH]
        phase = []
        for g in range(groups + (1 if s else 0)):
            p = None
            for j, k in taps:
                term = dwb_ref[k, :, lanes] * win[g + j]
                p = term if p is None else p + term
            phase.append(p)
        shifted = jnp.concatenate(phase, axis=0)[s:s + CONV_ROWS, :]
        out = shifted if out is None else out + shifted
    c_ref[r0:r0 + CONV_ROWS, lanes] = out


def _conv_mlp_prompt_kernel(x_ref, p_ref, vec_ref, dwb_ref, pw1_ref, pw2_ref, w1_ref, w2_ref, gate_ref, proj_ref,
                            o_ref, cs_ref, ext_ref, c_ref, act_ref, mid_ref, *, tile_m, tiles_per_stream):
    t = pl.program_id(0)
    ext_rows = tile_m + HALO
    slot = t % 2

    @pl.when(t % tiles_per_stream == 0)
    def _():
        ext_ref[0:HALO, :] = jnp.zeros((HALO, D_MODEL), F32)
        ext_ref[ext_rows:ext_rows + SUBLANES, :] = jnp.zeros((SUBLANES, D_MODEL), F32)

    @pl.when(t == 0)
    def _():
        mid_ref[1] = jnp.zeros((tile_m, D_MODEL), F32)

    x = x_ref[...]
    h = _rms(x, _row(vec_ref, V_NORM_MIX)).astype(BF16)

    def glu_piece(n):
        cols = slice(n * MXU_COLS, (n + 1) * MXU_COLS)
        gcols = slice(D_MODEL + n * MXU_COLS, D_MODEL + (n + 1) * MXU_COLS)
        a = _dot(h, pw1_ref[:, cols]) + vec_ref[V_PW1_B_LO:V_PW1_B_LO + 1, cols]
        g = _dot(h, pw1_ref[:, gcols]) + vec_ref[V_PW1_B_HI:V_PW1_B_HI + 1, cols]
        ext_ref[HALO:ext_rows, cols] = a * _sigmoid(g)

    glu_piece(0)
    conv_units = [functools.partial(_depthwise_unit, ext_ref, dwb_ref, c_ref, r * CONV_ROWS, l)
                  for l in range(D_MODEL // LANES) for r in range(tile_m // CONV_ROWS)]
    pieces = D_MODEL // MXU_COLS

    def norm_part(i):
        rows = slice(i * tile_m // pieces, (i + 1) * tile_m // pieces)
        act = _layernorm_silu(c_ref[rows, :] + _row(vec_ref, V_DW_B), _row(vec_ref, V_LN_G), _row(vec_ref, V_LN_B))
        act_ref[rows, :] = act.astype(BF16)

    o_ref[...] = _mlp_ple_rows(mid_ref[1 - slot], p_ref[...], vec_ref, w1_ref, w2_ref, gate_ref, proj_ref, False,
                               filler=conv_units,
                               lead=[functools.partial(glu_piece, n) for n in range(1, pieces)],
                               tail=[functools.partial(norm_part, i) for i in range(pieces)])
    cs_ref[0] = ext_ref[ext_rows - (CONV_WIDTH - 1):ext_rows, :]
    mid_ref[slot] = x + _dot(act_ref[...], pw2_ref[...]) + _row(vec_ref, V_PW2_B)

    @pl.when(t >= 0)
    def _():
        ext_ref[0:HALO, :] = ext_ref[tile_m:ext_rows, :]


def _conv_mlp_prompt(x, p3d, layer, lw, tile_m):
    b, s, _ = x.shape
    tiles_per_stream = s // tile_m
    n_tiles = b * tiles_per_stream
    mixer_tile = lambda t: jnp.minimum(t, n_tiles - 1)
    mlp_tile = lambda t: jnp.maximum(t - 1, 0)
    y, cs = pl.pallas_call(
        functools.partial(_conv_mlp_prompt_kernel, tile_m=tile_m, tiles_per_stream=tiles_per_stream),
        grid=(n_tiles + 1,),
        in_specs=[
            pl.BlockSpec((tile_m, D_MODEL), lambda t: (mixer_tile(t), 0)),
            pl.BlockSpec((None, tile_m, PLE_DIM), lambda t: (layer, mlp_tile(t), 0)),
            _const_spec((VEC_ROWS, D_MODEL)),
            _const_spec((CONV_WIDTH, SUBLANES, D_MODEL)),
            _const_spec((D_MODEL, 2 * D_MODEL)),
            _const_spec((D_MODEL, D_MODEL)),
            _const_spec((D_MODEL, D_FF)),
            _const_spec((D_FF, D_MODEL)),
            _const_spec((D_MODEL, D_MODEL)),
            _const_spec((PLE_DIM, D_MODEL)),
        ],
        out_specs=[
            pl.BlockSpec((tile_m, D_MODEL), lambda t: (mlp_tile(t), 0)),
            pl.BlockSpec((1, CONV_WIDTH - 1, D_MODEL), lambda t: (mixer_tile(t) // tiles_per_stream, 0, 0)),
        ],
        out_shape=[
            jax.ShapeDtypeStruct((b * s, D_MODEL), F32),
            jax.ShapeDtypeStruct((b, CONV_WIDTH - 1, D_MODEL), F32),
        ],
        scratch_shapes=[
            pltpu.VMEM((tile_m + HALO + SUBLANES, D_MODEL), F32),
            pltpu.VMEM((tile_m, D_MODEL), F32),
            pltpu.VMEM((tile_m, D_MODEL), BF16),
            pltpu.VMEM((2, tile_m, D_MODEL), F32),
        ],
        compiler_params=pltpu.CompilerParams(
            dimension_semantics=("arbitrary",), vmem_limit_bytes=VMEM_LIMIT),
        name="conv_mlp_prompt",
    )(x.reshape(b * s, D_MODEL), p3d, lw["vec"], lw["dwb"], lw["pw1"], lw["pw2"],
      lw["w1"], lw["w2"], lw["gate"], lw["proj"])
    return y.reshape(b, s, D_MODEL), cs


def _conv_sample_kernel(x_ref, left_ref, vec_ref, dw_ref, pw1_ref, pw2_ref, o_ref, cs_ref, ext_ref, *, streams, steps):
    x = x_ref[...]
    ext_ref[:, 0:HALO - (CONV_WIDTH - 1), :] = jnp.zeros((streams, HALO - (CONV_WIDTH - 1), D_MODEL), F32)
    ext_ref[:, HALO - (CONV_WIDTH - 1):HALO, :] = left_ref[...]
    ext_ref[:, HALO:HALO + steps, :] = _glu(x, vec_ref, pw1_ref).reshape(streams, steps, D_MODEL)
    acc = jnp.broadcast_to(_row(vec_ref, V_DW_B).reshape(1, 1, D_MODEL), (streams, steps, D_MODEL))
    for k in range(CONV_WIDTH):
        lo = k + HALO - (CONV_WIDTH - 1)
        acc = acc + dw_ref[k:k + 1, :].reshape(1, 1, D_MODEL) * ext_ref[:, lo:lo + steps, :]
    act = _layernorm_silu(acc.reshape(streams * steps, D_MODEL), _row(vec_ref, V_LN_G), _row(vec_ref, V_LN_B))
    cs_ref[...] = ext_ref[:, HALO + steps - (CONV_WIDTH - 1):HALO + steps, :]
    o_ref[...] = x + _dot(act.astype(BF16), pw2_ref[...]) + _row(vec_ref, V_PW2_B)


def _conv_sample(x2d, left, lw, streams, steps):
    rows = streams * steps
    return pl.pallas_call(
        functools.partial(_conv_sample_kernel, streams=streams, steps=steps),
        grid=(1,),
        in_specs=[
            _const_spec((rows, D_MODEL)),
            _const_spec((streams, CONV_WIDTH - 1, D_MODEL)),
            _const_spec((VEC_ROWS, D_MODEL)),
            _const_spec((CONV_WIDTH + 1, D_MODEL)),
            _const_spec((D_MODEL, 2 * D_MODEL)),
            _const_spec((D_MODEL, D_MODEL)),
        ],
        out_specs=[
            pl.BlockSpec((rows, D_MODEL), lambda t: (0, 0)),
            pl.BlockSpec((streams, CONV_WIDTH - 1, D_MODEL), lambda t: (0, 0, 0)),
        ],
        out_shape=[
            jax.ShapeDtypeStruct((rows, D_MODEL), F32),
            jax.ShapeDtypeStruct((streams, CONV_WIDTH - 1, D_MODEL), F32),
        ],
        scratch_shapes=[pltpu.VMEM((streams, HALO + steps, D_MODEL), F32)],
        compiler_params=pltpu.CompilerParams(
            dimension_semantics=("arbitrary",), vmem_limit_bytes=VMEM_LIMIT),
        name="conv_sample",
    )(x2d, left, lw["vec"], lw["dw"], lw["pw1"], lw["pw2"])


def _attn_mlp_prompt_kernel(x_ref, p_ref, vec_ref, bias_ref, wq_ref, wkv_ref, wo_ref, w1_ref, w2_ref, gate_ref,
                            proj_ref, o_ref, ks_ref, vs_ref, q_ref, kb_ref, vb_ref, att_ref, mid_ref,
                            *, tile_m, tiles_per_stream, final):
    t = pl.program_id(0)
    slot = t % 2
    tile_in_stream = t % tiles_per_stream

    @pl.when(tile_in_stream == 0)
    def _():
        kb_ref[0:WINDOW, :] = jnp.zeros((WINDOW, KV_DIM), BF16)
        vb_ref[0:WINDOW, :] = jnp.zeros((WINDOW, KV_DIM), BF16)

    @pl.when(t == 0)
    def _():
        mid_ref[1] = jnp.zeros((tile_m, D_MODEL), F32)

    x = x_ref[...]
    h = _rms(x, _row(vec_ref, V_NORM_MIX)).astype(BF16)
    q_ref[...] = _dot(h, wq_ref[...]).astype(BF16)
    kv = _dot(h, wkv_ref[...])
    k_new = kv[:, :KV_DIM]
    v_new = kv[:, KV_DIM:]
    kb_ref[WINDOW:WINDOW + tile_m, :] = k_new.astype(BF16)
    vb_ref[WINDOW:WINDOW + tile_m, :] = v_new.astype(BF16)
    ks_ref[0] = k_new[tile_m - WINDOW:, :]
    vs_ref[0] = v_new[tile_m - WINDOW:, :]

    rows = GROUP * CHUNK
    pad = jnp.zeros((KEYS_EXT - BAND, KV_DIM), BF16)
    weights = {}

    def chunk_logits(c0):
        q = jnp.concatenate(
            [q_ref[c0:c0 + CHUNK, g * KV_DIM:(g + 1) * KV_DIM] for g in range(GROUP)], axis=0)
        qm = jnp.concatenate(
            [jnp.where(_kv_lane_mask(q.shape, kh), q, jnp.zeros_like(q)) for kh in range(N_KV_HEADS)], axis=0)
        kb = jnp.concatenate([kb_ref[c0:c0 + BAND, :], pad], axis=0)
        logits = lax.dot_general(qm, kb, (((1,), (1,)), ((), ())), preferred_element_type=F32)
        logits = logits + bias_ref[...]
        key_pos = tile_in_stream * tile_m + c0 - WINDOW + lax.broadcasted_iota(jnp.int32, (1, KEYS_EXT), 1)
        logits = jnp.where(key_pos >= 0, logits, NEG_INF)
        weights[c0] = jnp.exp(logits - jnp.max(logits, axis=-1, keepdims=True)).astype(BF16)

    def chunk_values(c0):
        e = weights.pop(c0)
        vb = vb_ref[c0:c0 + BAND, :]
        half = LANES // HEAD_DIM
        krow = lax.broadcasted_iota(jnp.int32, (KEYS_EXT, LANES), 0)
        klane = lax.broadcasted_iota(jnp.int32, (KEYS_EXT, LANES), 1)
        outs = []
        for pair in range(N_KV_HEADS // half):
            num = den = None
            for sub in range(half):
                kh = pair * half + sub
                in_head = (klane >= sub * HEAD_DIM) & (klane < (sub + 1) * HEAD_DIM)
                v_kh = jnp.concatenate([vb[:, pair * LANES:(pair + 1) * LANES], pad[:, 0:LANES]], axis=0)
                v_kh = jnp.where(in_head, v_kh, jnp.zeros_like(v_kh))
                ones = jnp.where(in_head & (krow <= BAND), 1.0, 0.0).astype(BF16)
                pv = _dot(e[kh * rows:(kh + 1) * rows, :], jnp.concatenate([v_kh, ones], axis=1))
                num = pv[:, 0:LANES] if num is None else num + pv[:, 0:LANES]
                den = pv[:, LANES:] if den is None else den + pv[:, LANES:]
            outs.append(num / den)
        out = jnp.concatenate(outs, axis=1)
        for g in range(GROUP):
            att_ref[c0:c0 + CHUNK, g * KV_DIM:(g + 1) * KV_DIM] = out[g * CHUNK:(g + 1) * CHUNK, :].astype(BF16)

    chunk_units = []
    for c0 in range(0, tile_m, CHUNK):
        chunk_units += [functools.partial(chunk_logits, c0), functools.partial(chunk_values, c0)]
    o_ref[...] = _mlp_ple_rows(mid_ref[1 - slot], p_ref[...], vec_ref, w1_ref, w2_ref, gate_ref, proj_ref, final,
                               filler=chunk_units)

    mid_ref[slot] = x + _dot(att_ref[...], wo_ref[...])

    @pl.when(t >= 0)
    def _():
        kb_ref[0:WINDOW, :] = kb_ref[tile_m:tile_m + WINDOW, :]
        vb_ref[0:WINDOW, :] = vb_ref[tile_m:tile_m + WINDOW, :]


def _attn_mlp_prompt(x, p3d, layer, lw, bias, tile_m, final):
    b, s, _ = x.shape
    tiles_per_stream = s // tile_m
    n_tiles = b * tiles_per_stream
    mixer_tile = lambda t: jnp.minimum(t, n_tiles - 1)
    mlp_tile = lambda t: jnp.maximum(t - 1, 0)
    state_spec = pl.BlockSpec((1, WINDOW, KV_DIM), lambda t: (mixer_tile(t) // tiles_per_stream, 0, 0))
    y, ks, vs = pl.pallas_call(
        functools.partial(_attn_mlp_prompt_kernel, tile_m=tile_m, tiles_per_stream=tiles_per_stream, final=final),
        grid=(n_tiles + 1,),
        in_specs=[
            pl.BlockSpec((tile_m, D_MODEL), lambda t: (mixer_tile(t), 0)),
            pl.BlockSpec((None, tile_m, PLE_DIM), lambda t: (layer, mlp_tile(t), 0)),
            _const_spec((VEC_ROWS, D_MODEL)),
            _const_spec(bias.shape),
            _const_spec((D_MODEL, D_MODEL)),
            _const_spec((D_MODEL, 2 * KV_DIM)),
            _const_spec((D_MODEL, D_MODEL)),
            _const_spec((D_MODEL, D_FF)),
            _const_spec((D_FF, D_MODEL)),
            _const_spec((D_MODEL, D_MODEL)),
            _const_spec((PLE_DIM, D_MODEL)),
        ],
        out_specs=[
            pl.BlockSpec((tile_m, D_MODEL), lambda t: (mlp_tile(t), 0)),
            state_spec,
            state_spec,
        ],
        out_shape=[
            jax.ShapeDtypeStruct((b * s, D_MODEL), F32),
            jax.ShapeDtypeStruct((b, WINDOW, KV_DIM), F32),
            jax.ShapeDtypeStruct((b, WINDOW, KV_DIM), F32),
        ],
        scratch_shapes=[
            pltpu.VMEM((tile_m, D_MODEL), BF16),
            pltpu.VMEM((tile_m + WINDOW, KV_DIM), BF16),
            pltpu.VMEM((tile_m + WINDOW, KV_DIM), BF16),
            pltpu.VMEM((tile_m, D_MODEL), BF16),
            pltpu.VMEM((2, tile_m, D_MODEL), F32),
        ],
        compiler_params=pltpu.CompilerParams(
            dimension_semantics=("arbitrary",), vmem_limit_bytes=VMEM_LIMIT),
        name="attn_mlp_prompt",
    )(x.reshape(b * s, D_MODEL), p3d, lw["vec"], bias, lw["wq"], lw["wkv"], lw["wo"],
      lw["w1"], lw["w2"], lw["gate"], lw["proj"])
    return y.reshape(b, s, D_MODEL), ks, vs


def _attn_sample_kernel(x_ref, kc_ref, vc_ref, vec_ref, bias_ref, wq_ref, wkv_ref, wo_ref,
                        o_ref, ks_ref, vs_ref, kf_ref, vf_ref, *, streams, steps):
    cache = kc_ref.shape[1]
    keys = cache + steps
    x = x_ref[...]
    h = _rms(x, _row(vec_ref, V_NORM_MIX)).astype(BF16)
    q = _dot(h, wq_ref[...]).reshape(streams, steps, D_MODEL)
    kv = _dot(h, wkv_ref[...])
    kf_ref[:, 0:cache, :] = kc_ref[...]
    vf_ref[:, 0:cache, :] = vc_ref[...]
    kf_ref[:, cache:keys, :] = kv[:, :KV_DIM].reshape(streams, steps, KV_DIM)
    vf_ref[:, cache:keys, :] = kv[:, KV_DIM:].reshape(streams, steps, KV_DIM)
    ks_ref[...] = kf_ref[:, steps:keys, :]
    vs_ref[...] = vf_ref[:, steps:keys, :]

    qs = jnp.concatenate([q[:, :, g * KV_DIM:(g + 1) * KV_DIM] for g in range(GROUP)], axis=1).astype(BF16)
    kf = kf_ref[...].astype(BF16)
    vf = vf_ref[...].astype(BF16)
    out = jnp.zeros((streams, GROUP * steps, KV_DIM), F32)
    for kh in range(N_KV_HEADS):
        k_kh = jnp.where(_kv_lane_mask(kf.shape, kh), kf, jnp.zeros_like(kf))
        logits = jnp.einsum("bqd,bkd->bqk", qs, k_kh, preferred_element_type=F32)
        logits = logits + bias_ref[kh, :, 0:keys][None]
        probs = _sink_softmax(logits, bias_ref[kh, :, keys:keys + 1][None])
        o_kh = jnp.einsum("bqk,bkd->bqd", probs.astype(BF16), vf, preferred_element_type=F32)
        out = jnp.where(_kv_lane_mask(out.shape, kh), o_kh, out)
    att = jnp.concatenate([out[:, g * steps:(g + 1) * steps, :] for g in range(GROUP)], axis=2)
    o_ref[...] = x + _dot(att.reshape(streams * steps, D_MODEL).astype(BF16), wo_ref[...])


def _attn_sample(x2d, k_left, v_left, lw, bias, streams, steps):
    rows = streams * steps
    cache = k_left.shape[1]
    return pl.pallas_call(
        functools.partial(_attn_sample_kernel, streams=streams, steps=steps),
        grid=(1,),
        in_specs=[
            _const_spec((rows, D_MODEL)),
            _const_spec(k_left.shape),
            _const_spec(v_left.shape),
            _const_spec((VEC_ROWS, D_MODEL)),
            _const_spec(bias.shape),
            _const_spec((D_MODEL, D_MODEL)),
            _const_spec((D_MODEL, 2 * KV_DIM)),
            _const_spec((D_MODEL, D_MODEL)),
        ],
        out_specs=[
            pl.BlockSpec((rows, D_MODEL), lambda t: (0, 0)),
            pl.BlockSpec((streams, cache, KV_DIM), lambda t: (0, 0, 0)),
            pl.BlockSpec((streams, cache, KV_DIM), lambda t: (0, 0, 0)),
        ],
        out_shape=[
            jax.ShapeDtypeStruct((rows, D_MODEL), F32),
            jax.ShapeDtypeStruct((streams, cache, KV_DIM), F32),
            jax.ShapeDtypeStruct((streams, cache, KV_DIM), F32),
        ],
        scratch_shapes=[
            pltpu.VMEM((streams, cache + steps, KV_DIM), F32),
            pltpu.VMEM((streams, cache + steps, KV_DIM), F32),
        ],
        compiler_params=pltpu.CompilerParams(
            dimension_semantics=("arbitrary",), vmem_limit_bytes=VMEM_LIMIT),
        name="attn_sample",
    )(x2d, k_left, v_left, lw["vec"], bias, lw["wq"], lw["wkv"], lw["wo"])


def _t5_bucket(rel):
    nb = NUM_BUCKETS // 2
    max_exact = nb // 2
    ret = (rel > 0).astype(jnp.int32) * nb
    n = jnp.abs(rel)
    nf = jnp.maximum(n, 1).astype(F32)
    large = max_exact + (jnp.log(nf / max_exact) / math.log(MAX_DISTANCE / max_exact)
                         * (nb - max_exact)).astype(jnp.int32)
    large = jnp.minimum(large, nb - 1)
    return ret + jnp.where(n < max_exact, n, large)


def _bias_table(rel_bias, sinks, q_len, k_len, offset):
    rel = jnp.arange(k_len, dtype=jnp.int32)[None, :] - offset - jnp.arange(q_len, dtype=jnp.int32)[:, None]
    onehot = (_t5_bucket(rel)[:, :, None] == jnp.arange(NUM_BUCKETS, dtype=jnp.int32)).astype(F32)
    b = jnp.einsum("qkn,nh->hqk", onehot, rel_bias.astype(F32), precision=lax.Precision.HIGHEST)
    b = b.reshape(N_KV_HEADS, GROUP * q_len, k_len)
    s = jnp.repeat(sinks.astype(F32).reshape(N_KV_HEADS, GROUP), q_len, axis=1)[:, :, None]
    pad = jnp.full((N_KV_HEADS, GROUP * q_len, KEYS_EXT - k_len - 1), NEG_INF, F32)
    return jnp.concatenate([b, s, pad], axis=2)


def _pack_vec(rows):
    zero = jnp.zeros((D_MODEL,), F32)
    return jnp.stack([rows[i].astype(F32) if i in rows else zero for i in range(VEC_ROWS)])


def kernel(x_prompt, x_sample, cache_conv, cache_k, cache_v, p_prompt, p_sample, rel_bias, norm_mix, norm_mlp, norm_ple, norm_final, conv_pw1, conv_pw1_b, conv_dw, conv_dw_b, conv_ln_g, conv_ln_b, conv_pw2, conv_pw2_b, attn_wq, attn_wk, attn_wv, attn_wo, attn_sinks, mlp_w1, mlp_w2, ple_proj, ple_gate):
    batch, seq, _ = x_prompt.shape
    streams, steps, _ = x_sample.shape
    cache = cache_k.shape[2]
    assert seq % TILE_M == 0 and TILE_M % CHUNK == 0 and TILE_M >= WINDOW and steps % SUBLANES == 0

    layers = []
    for i in range(DEPTH):
        j = i // 2
        rows = {V_NORM_MIX: norm_mix[i], V_NORM_MLP: norm_mlp[i], V_NORM_PLE: norm_ple[i], V_NORM_FINAL: norm_final}
        lw = {"w1": mlp_w1[i].astype(BF16), "w2": mlp_w2[i].astype(BF16),
              "gate": ple_gate[i].astype(BF16), "proj": ple_proj[i].astype(BF16)}
        if i % 2 == 0:
            rows.update({V_PW1_B_LO: conv_pw1_b[j, :D_MODEL], V_PW1_B_HI: conv_pw1_b[j, D_MODEL:],
                         V_DW_B: conv_dw_b[j], V_LN_G: conv_ln_g[j], V_LN_B: conv_ln_b[j], V_PW2_B: conv_pw2_b[j]})
            lw["pw1"] = conv_pw1[j].astype(BF16)
            lw["pw2"] = conv_pw2[j].astype(BF16)
            lw["dw"] = jnp.concatenate([conv_dw[j].astype(F32), jnp.zeros((1, D_MODEL), F32)], axis=0)
            lw["dwb"] = jnp.broadcast_to(conv_dw[j].astype(F32)[:, None, :], (CONV_WIDTH, SUBLANES, D_MODEL))
        else:
            wq = (attn_wq[j] * (HEAD_DIM ** -0.5)).reshape(D_MODEL, N_KV_HEADS, GROUP, HEAD_DIM).transpose(0, 2, 1, 3)
            lw["wq"] = wq.reshape(D_MODEL, D_MODEL).astype(BF16)
            lw["wkv"] = jnp.concatenate([attn_wk[j], attn_wv[j]], axis=1).astype(BF16)
            wo = attn_wo[j].reshape(N_KV_HEADS, GROUP, HEAD_DIM, D_MODEL).transpose(1, 0, 2, 3)
            lw["wo"] = wo.reshape(D_MODEL, D_MODEL).astype(BF16)
            lw["bias_prompt"] = _bias_table(rel_bias, attn_sinks[j], CHUNK, BAND, WINDOW).reshape(N_HEADS * CHUNK, 256)
            lw["bias_sample"] = _bias_table(rel_bias, attn_sinks[j], steps, cache + steps, cache)
        lw["vec"] = _pack_vec(rows)
        layers.append(lw)

    p_prompt2 = p_prompt.reshape(DEPTH, batch * seq, PLE_DIM)
    p_sample2 = p_sample.reshape(DEPTH, streams * steps, PLE_DIM)

    xp = x_prompt
    xs = x_sample.reshape(streams * steps, D_MODEL)
    conv_p, k_p, v_p, conv_s, k_s, v_s = [], [], [], [], [], []
    for i, lw in enumerate(layers):
        j = i // 2
        final = i == DEPTH - 1
        if i % 2 == 0:
            xp, cs = _conv_mlp_prompt(xp, p_prompt2, i, lw, TILE_M)
            conv_p.append(cs)
            xs, cs = _conv_sample(xs, cache_conv[j], lw, streams, steps)
            conv_s.append(cs)
        else:
            xp, ks, vs = _attn_mlp_prompt(xp, p_prompt2, i, lw, lw["bias_prompt"], TILE_M, final)
            k_p.append(ks)
            v_p.append(vs)
            xs, ks, vs = _attn_sample(xs, cache_k[j].reshape(streams, cache, KV_DIM),
                                      cache_v[j].reshape(streams, cache, KV_DIM), lw, lw["bias_sample"], streams, steps)
            k_s.append(ks)
            v_s.append(vs)
        xs = _mlp_ple(xs, p_sample2, i, lw, streams * steps, final)

    def heads(ts, n):
        return jnp.stack(ts).reshape(len(ts), n, -1, N_KV_HEADS, HEAD_DIM)

    return (xp, xs.reshape(streams, steps, D_MODEL), jnp.stack(conv_p), heads(k_p, batch), heads(v_p, batch),
            jnp.stack(conv_s), heads(k_s, streams), heads(v_s, streams))
```

```python
import functools
import math

import jax
import jax.numpy as jnp
from jax import lax
from jax.experimental import pallas as pl
from jax.experimental.pallas import tpu as pltpu

D_MODEL = 1024
DEPTH = 4
CHUNK = 64
CONV_WIDTH = 31
WINDOW = 128
N_HEADS = 16
N_KV_HEADS = 4
GROUP = N_HEADS // N_KV_HEADS
HEAD_DIM = 64
KV_DIM = N_KV_HEADS * HEAD_DIM
D_FF = 4 * D_MODEL
PLE_DIM = 256
NUM_BUCKETS = 32
MAX_DISTANCE = 128
EPS = 1e-6
NEG_INF = -1e30

BAND = WINDOW + CHUNK
KEYS_EXT = 256
HALO = 32
SUBLANES = 8
LANES = 128
CONV_ROWS = 64
FF_CHUNK = 1024
MXU_COLS = 256
LEAD_WEIGHT = 2
TILE_M = 512
VMEM_LIMIT = 60 * 1024 * 1024

V_NORM_MIX, V_PW1_B_LO, V_PW1_B_HI, V_DW_B, V_LN_G, V_LN_B, V_PW2_B, V_NORM_MLP, V_NORM_PLE, V_NORM_FINAL = range(10)
VEC_ROWS = 16

F32 = jnp.float32
BF16 = jnp.bfloat16


def _dot(a, b):
    return jnp.dot(a, b, preferred_element_type=F32)


def _sigmoid(x):
    return 1.0 / (1.0 + jnp.exp(-x))


def _rms(x, g):
    return x * lax.rsqrt(jnp.mean(x * x, axis=-1, keepdims=True) + EPS) * g


def _row(vec_ref, i):
    return vec_ref[i:i + 1, :]


def _layernorm_silu(c, g, b):
    mu = jnp.mean(c, axis=-1, keepdims=True)
    d = c - mu
    var = jnp.mean(d * d, axis=-1, keepdims=True)
    y = d * lax.rsqrt(var + EPS) * g + b
    return y * _sigmoid(y)


def _sink_softmax(logits, sink):
    m = jnp.maximum(jnp.max(logits, axis=-1, keepdims=True), sink)
    e = jnp.exp(logits - m)
    return e / (jnp.sum(e, axis=-1, keepdims=True) + jnp.exp(sink - m))


def _kv_lane_mask(shape, kh):
    lane = lax.broadcasted_iota(jnp.int32, shape, len(shape) - 1)
    return (lane >= kh * HEAD_DIM) & (lane < (kh + 1) * HEAD_DIM)


def _mlp_ple_rows(x, p, vec_ref, w1_ref, w2_ref, gate_ref, proj_ref, final, filler=(), lead=(), tail=()):
    filler = list(filler)
    pieces = D_MODEL // MXU_COLS
    n_pieces = (D_FF // FF_CHUNK) * (FF_CHUNK // MXU_COLS + pieces) + LEAD_WEIGHT * len(lead)
    done = [0, 0]

    def fill(weight=1):
        done[0] += weight
        while done[1] < len(filler) * done[0] // n_pieces:
            filler[done[1]]()
            done[1] += 1

    for piece in lead:
        piece()
        fill(LEAD_WEIGHT)
    h = _rms(x, _row(vec_ref, V_NORM_MLP)).astype(BF16)
    xs = [x[:, n * MXU_COLS:(n + 1) * MXU_COLS] for n in range(pieces)]
    for c in range(D_FF // FF_CHUNK):
        hid = []
        for n in range(FF_CHUNK // MXU_COLS):
            lo = c * FF_CHUNK + n * MXU_COLS
            a = _dot(h, w1_ref[:, lo:lo + MXU_COLS])
            hid.append(jnp.square(jnp.maximum(a, 0.0)).astype(BF16))
            fill()
        hid = jnp.concatenate(hid, axis=1)
        for n in range(pieces):
            xs[n] = xs[n] + _dot(hid, w2_ref[c * FF_CHUNK:(c + 1) * FF_CHUNK, n * MXU_COLS:(n + 1) * MXU_COLS])
            fill()
    assert done == [n_pieces, len(filler)]
    x = jnp.concatenate(xs, axis=1)
    hg = _rms(x, _row(vec_ref, V_NORM_PLE)).astype(BF16)
    if tail:
        assert len(tail) == pieces
        gate = []
        for n in range(pieces):
            gate.append(_dot(hg, gate_ref[:, n * MXU_COLS:(n + 1) * MXU_COLS]))
            tail[n]()
        gate = _sigmoid(jnp.concatenate(gate, axis=1))
    else:
        gate = _sigmoid(_dot(hg, gate_ref[...]))
    x = x + _dot(p.astype(BF16), proj_ref[...]) * gate
    if final:
        x = _rms(x, _row(vec_ref, V_NORM_FINAL))
    return x


def _const_spec(shape):
    nd = len(shape)
    return pl.BlockSpec(shape, lambda *_: (0,) * nd, pipeline_mode=pl.Buffered(1))


def _glu(x, vec_ref, pw1_ref):
    h = _rms(x, _row(vec_ref, V_NORM_MIX)).astype(BF16)
    a = _dot(h, pw1_ref[:, :D_MODEL]) + _row(vec_ref, V_PW1_B_LO)
    g = _dot(h, pw1_ref[:, D_MODEL:]) + _row(vec_ref, V_PW1_B_HI)
    return a * _sigmoid(g)


def _depthwise_unit(ext_ref, dwb_ref, c_ref, r0, l):
    first = HALO - (CONV_WIDTH - 1)
    groups = CONV_ROWS // SUBLANES
    spans = -(-(first + CONV_WIDTH) // SUBLANES)
    lanes = slice(l * LANES, (l + 1) * LANES)
    win = [ext_ref[r0 + SUBLANES * g:r0 + SUBLANES * (g + 1), lanes] for g in range(groups + spans)]
    out = None
    for s in range(SUBLANES):
        taps = [(j, SUBLANES * j + s - first) for j in range(spans)
                if 0 <= SUBLANES * j + s - first < CONV_WIDTH]
        phase = []
        for g in range(groups + (1 if s else 0)):
            p = None
            for j, k in taps:
                term = dwb_ref[k, :, lanes] * win[g + j]
                p = term if p is None else p + term
            phase.append(p)
        shifted = jnp.concatenate(phase, axis=0)[s:s + CONV_ROWS, :]
        out = shifted if out is None else out + shifted
    c_ref[r0:r0 + CONV_ROWS, lanes] = out


def _conv_mlp_prompt_kernel(x_ref, p_ref, vec_ref, dwb_ref, pw1_ref, pw2_ref, w1_ref, w2_ref, gate_ref, proj_ref,
                            o_ref, cs_ref, ext_ref, c_ref, act_ref, mid_ref, *, tile_m, tiles_per_stream):
    t = pl.program_id(0)
    ext_rows = tile_m + HALO
    slot = t % 2

    @pl.when(t % tiles_per_stream == 0)
    def _():
        ext_ref[0:HALO, :] = jnp.zeros((HALO, D_MODEL), F32)
        ext_ref[ext_rows:ext_rows + SUBLANES, :] = jnp.zeros((SUBLANES, D_MODEL), F32)

    @pl.when(t == 0)
    def _():
        mid_ref[1] = jnp.zeros((tile_m, D_MODEL), F32)

    x = x_ref[...]
    h = _rms(x, _row(vec_ref, V_NORM_MIX)).astype(BF16)

    def glu_piece(n):
        cols = slice(n * MXU_COLS, (n + 1) * MXU_COLS)
        gcols = slice(D_MODEL + n * MXU_COLS, D_MODEL + (n + 1) * MXU_COLS)
        a = _dot(h, pw1_ref[:, cols]) + vec_ref[V_PW1_B_LO:V_PW1_B_LO + 1, cols]
        g = _dot(h, pw1_ref[:, gcols]) + vec_ref[V_PW1_B_HI:V_PW1_B_HI + 1, cols]
        ext_ref[HALO:ext_rows, cols] = a * _sigmoid(g)

    glu_piece(0)
    conv_units = [functools.partial(_depthwise_unit, ext_ref, dwb_ref, c_ref, r * CONV_ROWS, l)
                  for l in range(D_MODEL // LANES) for r in range(tile_m // CONV_ROWS)]
    pieces = D_MODEL // MXU_COLS

    def norm_part(i):
        rows = slice(i * tile_m // pieces, (i + 1) * tile_m // pieces)
        act = _layernorm_silu(c_ref[rows, :] + _row(vec_ref, V_DW_B), _row(vec_ref, V_LN_G), _row(vec_ref, V_LN_B))
        act_ref[rows, :] = act.astype(BF16)

    o_ref[...] = _mlp_ple_rows(mid_ref[1 - slot], p_ref[...], vec_ref, w1_ref, w2_ref, gate_ref, proj_ref, False,
                               filler=conv_units,
                               lead=[functools.partial(glu_piece, n) for n in range(1, pieces)],
                               tail=[functools.partial(norm_part, i) for i in range(pieces)])
    cs_ref[0] = ext_ref[ext_rows - (CONV_WIDTH - 1):ext_rows, :]
    mid_ref[slot] = x + _dot(act_ref[...], pw2_ref[...]) + _row(vec_ref, V_PW2_B)

    @pl.when(t >= 0)
    def _():
        ext_ref[0:HALO, :] = ext_ref[tile_m:ext_rows, :]


def _conv_mlp_prompt(x, p3d, layer, lw, tile_m):
    b, s, _ = x.shape
    tiles_per_stream = s // tile_m
    n_tiles = b * tiles_per_stream
    mixer_tile = lambda t: jnp.minimum(t, n_tiles - 1)
    mlp_tile = lambda t: jnp.maximum(t - 1, 0)
    y, cs = pl.pallas_call(
        functools.partial(_conv_mlp_prompt_kernel, tile_m=tile_m, tiles_per_stream=tiles_per_stream),
        grid=(n_tiles + 1,),
        in_specs=[
            pl.BlockSpec((tile_m, D_MODEL), lambda t: (mixer_tile(t), 0)),
            pl.BlockSpec((None, tile_m, PLE_DIM), lambda t: (layer, mlp_tile(t), 0)),
            _const_spec((VEC_ROWS, D_MODEL)),
            _const_spec((CONV_WIDTH, SUBLANES, D_MODEL)),
            _const_spec((D_MODEL, 2 * D_MODEL)),
            _const_spec((D_MODEL, D_MODEL)),
            _const_spec((D_MODEL, D_FF)),
            _const_spec((D_FF, D_MODEL)),
            _const_spec((D_MODEL, D_MODEL)),
            _const_spec((PLE_DIM, D_MODEL)),
        ],
        out_specs=[
            pl.BlockSpec((tile_m, D_MODEL), lambda t: (mlp_tile(t), 0)),
            pl.BlockSpec((1, CONV_WIDTH - 1, D_MODEL), lambda t: (mixer_tile(t) // tiles_per_stream, 0, 0)),
        ],
        out_shape=[
            jax.ShapeDtypeStruct((b * s, D_MODEL), F32),
            jax.ShapeDtypeStruct((b, CONV_WIDTH - 1, D_MODEL), F32),
        ],
        scratch_shapes=[
            pltpu.VMEM((tile_m + HALO + SUBLANES, D_MODEL), F32),
            pltpu.VMEM((tile_m, D_MODEL), F32),
            pltpu.VMEM((tile_m, D_MODEL), BF16),
            pltpu.VMEM((2, tile_m, D_MODEL), F32),
        ],
        compiler_params=pltpu.CompilerParams(
            dimension_semantics=("arbitrary",), vmem_limit_bytes=VMEM_LIMIT),
        name="conv_mlp_prompt",
    )(x.reshape(b * s, D_MODEL), p3d, lw["vec"], lw["dwb"], lw["pw1"], lw["pw2"],
      lw["w1"], lw["w2"], lw["gate"], lw["proj"])
    return y.reshape(b, s, D_MODEL), cs


def _conv_sample_kernel(x_ref, left_ref, vec_ref, dw_ref, pw1_ref, pw2_ref, p_ref, w1_ref, w2_ref, gate_ref,
                        proj_ref, o_ref, cs_ref, ext_ref, *, streams, steps):
    x = x_ref[...]
    ext_ref[:, 0:HALO - (CONV_WIDTH - 1), :] = jnp.zeros((streams, HALO - (CONV_WIDTH - 1), D_MODEL), F32)
    ext_ref[:, HALO - (CONV_WIDTH - 1):HALO, :] = left_ref[...]
    ext_ref[:, HALO:HALO + steps, :] = _glu(x, vec_ref, pw1_ref).reshape(streams, steps, D_MODEL)
    acc = jnp.broadcast_to(_row(vec_ref, V_DW_B).reshape(1, 1, D_MODEL), (streams, steps, D_MODEL))
    for k in range(CONV_WIDTH):
        lo = k + HALO - (CONV_WIDTH - 1)
        acc = acc + dw_ref[k:k + 1, :].reshape(1, 1, D_MODEL) * ext_ref[:, lo:lo + steps, :]
    act = _layernorm_silu(acc.reshape(streams * steps, D_MODEL), _row(vec_ref, V_LN_G), _row(vec_ref, V_LN_B))
    cs_ref[...] = ext_ref[:, HALO + steps - (CONV_WIDTH - 1):HALO + steps, :]
    mixed = x + _dot(act.astype(BF16), pw2_ref[...]) + _row(vec_ref, V_PW2_B)
    o_ref[...] = _mlp_ple_rows(mixed, p_ref[...], vec_ref, w1_ref, w2_ref, gate_ref, proj_ref, False)


def _mlp_specs(layer, rows):
    return [
        pl.BlockSpec((None, rows, PLE_DIM), lambda t: (layer, 0, 0), pipeline_mode=pl.Buffered(1)),
        _const_spec((D_MODEL, D_FF)),
        _const_spec((D_FF, D_MODEL)),
        _const_spec((D_MODEL, D_MODEL)),
        _const_spec((PLE_DIM, D_MODEL)),
    ]


def _conv_mlp_sample(x2d, left, p3d, layer, lw, streams, steps):
    rows = streams * steps
    return pl.pallas_call(
        functools.partial(_conv_sample_kernel, streams=streams, steps=steps),
        grid=(1,),
        in_specs=[
            _const_spec((rows, D_MODEL)),
            _const_spec((streams, CONV_WIDTH - 1, D_MODEL)),
            _const_spec((VEC_ROWS, D_MODEL)),
            _const_spec((CONV_WIDTH + 1, D_MODEL)),
            _const_spec((D_MODEL, 2 * D_MODEL)),
            _const_spec((D_MODEL, D_MODEL)),
        ] + _mlp_specs(layer, rows),
        out_specs=[
            pl.BlockSpec((rows, D_MODEL), lambda t: (0, 0)),
            pl.BlockSpec((streams, CONV_WIDTH - 1, D_MODEL), lambda t: (0, 0, 0)),
        ],
        out_shape=[
            jax.ShapeDtypeStruct((rows, D_MODEL), F32),
            jax.ShapeDtypeStruct((streams, CONV_WIDTH - 1, D_MODEL), F32),
        ],
        scratch_shapes=[pltpu.VMEM((streams, HALO + steps, D_MODEL), F32)],
        compiler_params=pltpu.CompilerParams(
            dimension_semantics=("arbitrary",), vmem_limit_bytes=VMEM_LIMIT),
        name="conv_mlp_sample",
    )(x2d, left, lw["vec"], lw["dw"], lw["pw1"], lw["pw2"], p3d, lw["w1"], lw["w2"], lw["gate"], lw["proj"])


def _attn_mlp_prompt_kernel(x_ref, p_ref, vec_ref, bias_ref, wq_ref, wkv_ref, wo_ref, w1_ref, w2_ref, gate_ref,
                            proj_ref, o_ref, ks_ref, vs_ref, q_ref, kb_ref, vb_ref, att_ref, mid_ref,
                            *, tile_m, tiles_per_stream, final):
    t = pl.program_id(0)
    slot = t % 2
    tile_in_stream = t % tiles_per_stream

    @pl.when(tile_in_stream == 0)
    def _():
        kb_ref[0:WINDOW, :] = jnp.zeros((WINDOW, KV_DIM), BF16)
        vb_ref[0:WINDOW, :] = jnp.zeros((WINDOW, KV_DIM), BF16)

    @pl.when(t == 0)
    def _():
        mid_ref[1] = jnp.zeros((tile_m, D_MODEL), F32)

    x = x_ref[...]
    h = _rms(x, _row(vec_ref, V_NORM_MIX)).astype(BF16)
    q_ref[...] = _dot(h, wq_ref[...]).astype(BF16)
    kv = _dot(h, wkv_ref[...])
    k_new = kv[:, :KV_DIM]
    v_new = kv[:, KV_DIM:]
    kb_ref[WINDOW:WINDOW + tile_m, :] = k_new.astype(BF16)
    vb_ref[WINDOW:WINDOW + tile_m, :] = v_new.astype(BF16)
    ks_ref[0] = k_new[tile_m - WINDOW:, :]
    vs_ref[0] = v_new[tile_m - WINDOW:, :]

    rows = GROUP * CHUNK
    pad = jnp.zeros((KEYS_EXT - BAND, KV_DIM), BF16)
    weights = {}

    def chunk_logits(c0):
        q = jnp.concatenate(
            [q_ref[c0:c0 + CHUNK, g * KV_DIM:(g + 1) * KV_DIM] for g in range(GROUP)], axis=0)
        qm = jnp.concatenate(
            [jnp.where(_kv_lane_mask(q.shape, kh), q, jnp.zeros_like(q)) for kh in range(N_KV_HEADS)], axis=0)
        kb = jnp.concatenate([kb_ref[c0:c0 + BAND, :], pad], axis=0)
        logits = lax.dot_general(qm, kb, (((1,), (1,)), ((), ())), preferred_element_type=F32)
        logits = logits + bias_ref[...]
        key_pos = tile_in_stream * tile_m + c0 - WINDOW + lax.broadcasted_iota(jnp.int32, (1, KEYS_EXT), 1)
        logits = jnp.where(key_pos >= 0, logits, NEG_INF)
        weights[c0] = jnp.exp(logits - jnp.max(logits, axis=-1, keepdims=True)).astype(BF16)

    def chunk_values(c0):
        e = weights.pop(c0)
        vb = vb_ref[c0:c0 + BAND, :]
        half = LANES // HEAD_DIM
        krow = lax.broadcasted_iota(jnp.int32, (KEYS_EXT, LANES), 0)
        klane = lax.broadcasted_iota(jnp.int32, (KEYS_EXT, LANES), 1)
        outs = []
        for pair in range(N_KV_HEADS // half):
            num = den = None
            for sub in range(half):
                kh = pair * half + sub
                in_head = (klane >= sub * HEAD_DIM) & (klane < (sub + 1) * HEAD_DIM)
                v_kh = jnp.concatenate([vb[:, pair * LANES:(pair + 1) * LANES], pad[:, 0:LANES]], axis=0)
                v_kh = jnp.where(in_head, v_kh, jnp.zeros_like(v_kh))
                ones = jnp.where(in_head & (krow <= BAND), 1.0, 0.0).astype(BF16)
                pv = _dot(e[kh * rows:(kh + 1) * rows, :], jnp.concatenate([v_kh, ones], axis=1))
                num = pv[:, 0:LANES] if num is None else num + pv[:, 0:LANES]
                den = pv[:, LANES:] if den is None else den + pv[:, LANES:]
            outs.append(num / den)
        out = jnp.concatenate(outs, axis=1)
        for g in range(GROUP):
            att_ref[c0:c0 + CHUNK, g * KV_DIM:(g + 1) * KV_DIM] = out[g * CHUNK:(g + 1) * CHUNK, :].astype(BF16)

    chunk_units = []
    for c0 in range(0, tile_m, CHUNK):
        chunk_units += [functools.partial(chunk_logits, c0), functools.partial(chunk_values, c0)]
    o_ref[...] = _mlp_ple_rows(mid_ref[1 - slot], p_ref[...], vec_ref, w1_ref, w2_ref, gate_ref, proj_ref, final,
                               filler=chunk_units)

    mid_ref[slot] = x + _dot(att_ref[...], wo_ref[...])

    @pl.when(t >= 0)
    def _():
        kb_ref[0:WINDOW, :] = kb_ref[tile_m:tile_m + WINDOW, :]
        vb_ref[0:WINDOW, :] = vb_ref[tile_m:tile_m + WINDOW, :]


def _attn_mlp_prompt(x, p3d, layer, lw, bias, tile_m, final):
    b, s, _ = x.shape
    tiles_per_stream = s // tile_m
    n_tiles = b * tiles_per_stream
    mixer_tile = lambda t: jnp.minimum(t, n_tiles - 1)
    mlp_tile = lambda t: jnp.maximum(t - 1, 0)
    state_spec = pl.BlockSpec((1, WINDOW, KV_DIM), lambda t: (mixer_tile(t) // tiles_per_stream, 0, 0))
    y, ks, vs = pl.pallas_call(
        functools.partial(_attn_mlp_prompt_kernel, tile_m=tile_m, tiles_per_stream=tiles_per_stream, final=final),
        grid=(n_tiles + 1,),
        in_specs=[
            pl.BlockSpec((tile_m, D_MODEL), lambda t: (mixer_tile(t), 0)),
            pl.BlockSpec((None, tile_m, PLE_DIM), lambda t: (layer, mlp_tile(t), 0)),
            _const_spec((VEC_ROWS, D_MODEL)),
            _const_spec(bias.shape),
            _const_spec((D_MODEL, D_MODEL)),
            _const_spec((D_MODEL, 2 * KV_DIM)),
            _const_spec((D_MODEL, D_MODEL)),
            _const_spec((D_MODEL, D_FF)),
            _const_spec((D_FF, D_MODEL)),
            _const_spec((D_MODEL, D_MODEL)),
            _const_spec((PLE_DIM, D_MODEL)),
        ],
        out_specs=[
            pl.BlockSpec((tile_m, D_MODEL), lambda t: (mlp_tile(t), 0)),
            state_spec,
            state_spec,
        ],
        out_shape=[
            jax.ShapeDtypeStruct((b * s, D_MODEL), F32),
            jax.ShapeDtypeStruct((b, WINDOW, KV_DIM), F32),
            jax.ShapeDtypeStruct((b, WINDOW, KV_DIM), F32),
        ],
        scratch_shapes=[
            pltpu.VMEM((tile_m, D_MODEL), BF16),
            pltpu.VMEM((tile_m + WINDOW, KV_DIM), BF16),
            pltpu.VMEM((tile_m + WINDOW, KV_DIM), BF16),
            pltpu.VMEM((tile_m, D_MODEL), BF16),
            pltpu.VMEM((2, tile_m, D_MODEL), F32),
        ],
        compiler_params=pltpu.CompilerParams(
            dimension_semantics=("arbitrary",), vmem_limit_bytes=VMEM_LIMIT),
        name="attn_mlp_prompt",
    )(x.reshape(b * s, D_MODEL), p3d, lw["vec"], bias, lw["wq"], lw["wkv"], lw["wo"],
      lw["w1"], lw["w2"], lw["gate"], lw["proj"])
    return y.reshape(b, s, D_MODEL), ks, vs


def _attn_sample_kernel(x_ref, kc_ref, vc_ref, vec_ref, bias_ref, wq_ref, wkv_ref, wo_ref, p_ref, w1_ref, w2_ref,
                        gate_ref, proj_ref, o_ref, ks_ref, vs_ref, kf_ref, vf_ref, *, streams, steps, final):
    cache = kc_ref.shape[1]
    keys = cache + steps
    x = x_ref[...]
    h = _rms(x, _row(vec_ref, V_NORM_MIX)).astype(BF16)
    q = _dot(h, wq_ref[...]).reshape(streams, steps, D_MODEL)
    kv = _dot(h, wkv_ref[...])
    kf_ref[:, 0:cache, :] = kc_ref[...]
    vf_ref[:, 0:cache, :] = vc_ref[...]
    kf_ref[:, cache:keys, :] = kv[:, :KV_DIM].reshape(streams, steps, KV_DIM)
    vf_ref[:, cache:keys, :] = kv[:, KV_DIM:].reshape(streams, steps, KV_DIM)
    ks_ref[...] = kf_ref[:, steps:keys, :]
    vs_ref[...] = vf_ref[:, steps:keys, :]

    qs = jnp.concatenate([q[:, :, g * KV_DIM:(g + 1) * KV_DIM] for g in range(GROUP)], axis=1).astype(BF16)
    kf = kf_ref[...].astype(BF16)
    vf = vf_ref[...].astype(BF16)
    out = jnp.zeros((streams, GROUP * steps, KV_DIM), F32)
    for kh in range(N_KV_HEADS):
        k_kh = jnp.where(_kv_lane_mask(kf.shape, kh), kf, jnp.zeros_like(kf))
        logits = jnp.einsum("bqd,bkd->bqk", qs, k_kh, preferred_element_type=F32)
        logits = logits + bias_ref[kh, :, 0:keys][None]
        probs = _sink_softmax(logits, bias_ref[kh, :, keys:keys + 1][None])
        o_kh = jnp.einsum("bqk,bkd->bqd", probs.astype(BF16), vf, preferred_element_type=F32)
        out = jnp.where(_kv_lane_mask(out.shape, kh), o_kh, out)
    att = jnp.concatenate([out[:, g * steps:(g + 1) * steps, :] for g in range(GROUP)], axis=2)
    mixed = x + _dot(att.reshape(streams * steps, D_MODEL).astype(BF16), wo_ref[...])
    o_ref[...] = _mlp_ple_rows(mixed, p_ref[...], vec_ref, w1_ref, w2_ref, gate_ref, proj_ref, final)


def _attn_mlp_sample(x2d, k_left, v_left, p3d, layer, lw, bias, streams, steps, final):
    rows = streams * steps
    cache = k_left.shape[1]
    return pl.pallas_call(
        functools.partial(_attn_sample_kernel, streams=streams, steps=steps, final=final),
        grid=(1,),
        in_specs=[
            _const_spec((rows, D_MODEL)),
            _const_spec(k_left.shape),
            _const_spec(v_left.shape),
            _const_spec((VEC_ROWS, D_MODEL)),
            _const_spec(bias.shape),
            _const_spec((D_MODEL, D_MODEL)),
            _const_spec((D_MODEL, 2 * KV_DIM)),
            _const_spec((D_MODEL, D_MODEL)),
        ] + _mlp_specs(layer, rows),
        out_specs=[
            pl.BlockSpec((rows, D_MODEL), lambda t: (0, 0)),
            pl.BlockSpec((streams, cache, KV_DIM), lambda t: (0, 0, 0)),
            pl.BlockSpec((streams, cache, KV_DIM), lambda t: (0, 0, 0)),
        ],
        out_shape=[
            jax.ShapeDtypeStruct((rows, D_MODEL), F32),
            jax.ShapeDtypeStruct((streams, cache, KV_DIM), F32),
            jax.ShapeDtypeStruct((streams, cache, KV_DIM), F32),
        ],
        scratch_shapes=[
            pltpu.VMEM((streams, cache + steps, KV_DIM), F32),
            pltpu.VMEM((streams, cache + steps, KV_DIM), F32),
        ],
        compiler_params=pltpu.CompilerParams(
            dimension_semantics=("arbitrary",), vmem_limit_bytes=VMEM_LIMIT),
        name="attn_mlp_sample",
    )(x2d, k_left, v_left, lw["vec"], bias, lw["wq"], lw["wkv"], lw["wo"],
      p3d, lw["w1"], lw["w2"], lw["gate"], lw["proj"])


def _t5_bucket(rel):
    nb = NUM_BUCKETS // 2
    max_exact = nb // 2
    ret = (rel > 0).astype(jnp.int32) * nb
    n = jnp.abs(rel)
    nf = jnp.maximum(n, 1).astype(F32)
    large = max_exact + (jnp.log(nf / max_exact) / math.log(MAX_DISTANCE / max_exact)
                         * (nb - max_exact)).astype(jnp.int32)
    large = jnp.minimum(large, nb - 1)
    return ret + jnp.where(n < max_exact, n, large)


def _bias_table(rel_bias, sinks, q_len, k_len, offset):
    rel = jnp.arange(k_len, dtype=jnp.int32)[None, :] - offset - jnp.arange(q_len, dtype=jnp.int32)[:, None]
    onehot = (_t5_bucket(rel)[:, :, None] == jnp.arange(NUM_BUCKETS, dtype=jnp.int32)).astype(F32)
    b = jnp.einsum("qkn,nh->hqk", onehot, rel_bias.astype(F32), precision=lax.Precision.HIGHEST)
    b = b.reshape(N_KV_HEADS, GROUP * q_len, k_len)
    s = jnp.repeat(sinks.astype(F32).reshape(N_KV_HEADS, GROUP), q_len, axis=1)[:, :, None]
    pad = jnp.full((N_KV_HEADS, GROUP * q_len, KEYS_EXT - k_len - 1), NEG_INF, F32)
    return jnp.concatenate([b, s, pad], axis=2)


def _pack_vec(rows):
    zero = jnp.zeros((D_MODEL,), F32)
    return jnp.stack([rows[i].astype(F32) if i in rows else zero for i in range(VEC_ROWS)])


def kernel(x_prompt, x_sample, cache_conv, cache_k, cache_v, p_prompt, p_sample, rel_bias, norm_mix, norm_mlp, norm_ple, norm_final, conv_pw1, conv_pw1_b, conv_dw, conv_dw_b, conv_ln_g, conv_ln_b, conv_pw2, conv_pw2_b, attn_wq, attn_wk, attn_wv, attn_wo, attn_sinks, mlp_w1, mlp_w2, ple_proj, ple_gate):
    batch, seq, _ = x_prompt.shape
    streams, steps, _ = x_sample.shape
    cache = cache_k.shape[2]
    assert seq % TILE_M == 0 and TILE_M % CHUNK == 0 and TILE_M >= WINDOW and steps % SUBLANES == 0

    layers = []
    for i in range(DEPTH):
        j = i // 2
        rows = {V_NORM_MIX: norm_mix[i], V_NORM_MLP: norm_mlp[i], V_NORM_PLE: norm_ple[i], V_NORM_FINAL: norm_final}
        lw = {"w1": mlp_w1[i].astype(BF16), "w2": mlp_w2[i].astype(BF16),
              "gate": ple_gate[i].astype(BF16), "proj": ple_proj[i].astype(BF16)}
        if i % 2 == 0:
            rows.update({V_PW1_B_LO: conv_pw1_b[j, :D_MODEL], V_PW1_B_HI: conv_pw1_b[j, D_MODEL:],
                         V_DW_B: conv_dw_b[j], V_LN_G: conv_ln_g[j], V_LN_B: conv_ln_b[j], V_PW2_B: conv_pw2_b[j]})
            lw["pw1"] = conv_pw1[j].astype(BF16)
            lw["pw2"] = conv_pw2[j].astype(BF16)
            lw["dw"] = jnp.concatenate([conv_dw[j].astype(F32), jnp.zeros((1, D_MODEL), F32)], axis=0)
            lw["dwb"] = jnp.broadcast_to(conv_dw[j].astype(F32)[:, None, :], (CONV_WIDTH, SUBLANES, D_MODEL))
        else:
            wq = (attn_wq[j] * (HEAD_DIM ** -0.5)).reshape(D_MODEL, N_KV_HEADS, GROUP, HEAD_DIM).transpose(0, 2, 1, 3)
            lw["wq"] = wq.reshape(D_MODEL, D_MODEL).astype(BF16)
            lw["wkv"] = jnp.concatenate([attn_wk[j], attn_wv[j]], axis=1).astype(BF16)
            wo = attn_wo[j].reshape(N_KV_HEADS, GROUP, HEAD_DIM, D_MODEL).transpose(1, 0, 2, 3)
            lw["wo"] = wo.reshape(D_MODEL, D_MODEL).astype(BF16)
            lw["bias_prompt"] = _bias_table(rel_bias, attn_sinks[j], CHUNK, BAND, WINDOW).reshape(N_HEADS * CHUNK, 256)
            lw["bias_sample"] = _bias_table(rel_bias, attn_sinks[j], steps, cache + steps, cache)
        lw["vec"] = _pack_vec(rows)
        layers.append(lw)

    p_prompt2 = p_prompt.reshape(DEPTH, batch * seq, PLE_DIM)
    p_sample2 = p_sample.reshape(DEPTH, streams * steps, PLE_DIM)

    xp = x_prompt
    xs = x_sample.reshape(streams * steps, D_MODEL)
    conv_p, k_p, v_p, conv_s, k_s, v_s = [], [], [], [], [], []
    for i, lw in enumerate(layers):
        j = i // 2
        final = i == DEPTH - 1
        if i % 2 == 0:
            xp, cs = _conv_mlp_prompt(xp, p_prompt2, i, lw, TILE_M)
            conv_p.append(cs)
            xs, cs = _conv_mlp_sample(xs, cache_conv[j], p_sample2, i, lw, streams, steps)
            conv_s.append(cs)
        else:
            xp, ks, vs = _attn_mlp_prompt(xp, p_prompt2, i, lw, lw["bias_prompt"], TILE_M, final)
            k_p.append(ks)
            v_p.append(vs)
            xs, ks, vs = _attn_mlp_sample(xs, cache_k[j].reshape(streams, cache, KV_DIM),
                                          cache_v[j].reshape(streams, cache, KV_DIM), p_sample2, i, lw,
                                          lw["bias_sample"], streams, steps, final)
            k_s.append(ks)
            v_s.append(vs)

    def heads(ts, n):
        return jnp.stack(ts).reshape(len(ts), n, -1, N_KV_HEADS, HEAD_DIM)

    return (xp, xs.reshape(streams, steps, D_MODEL), jnp.stack(conv_p), heads(k_p, batch), heads(v_p, batch),
            jnp.stack(conv_s), heads(k_s, streams), heads(v_s, streams))
```
